```python
import math
import jax
import jax.numpy as jnp
from jax import lax
import numpy as np

D_MODEL = 4096
BATCH = 1
SEQ = 8192
DEPTH = 1

HEAD_DIM = 128
DIL_PAIRS = ((128, 1), (512, 4), (2048, 16))
N_GROUPS = len(DIL_PAIRS)
HEADS_PER_GROUP = 4
A_HEADS = N_GROUPS * HEADS_PER_GROUP
A_WIDTH = A_HEADS * HEAD_DIM
A_OUT = HEADS_PER_GROUP * HEAD_DIM
B_HEADS = 8
B_QK = B_HEADS * 2 * HEAD_DIM
B_V = B_HEADS * 2 * HEAD_DIM
SPLIT_SIZES = (A_WIDTH, A_WIDTH, A_WIDTH, B_QK, B_QK, B_V, D_MODEL, D_MODEL)
IN_COLS = sum(SPLIT_SIZES)
SPLITS = tuple(int(v) for v in np.cumsum(SPLIT_SIZES)[:-1])
D_FF = 4 * D_MODEL
ROPE_THETA = 500000.0
ROT_DIM = HEAD_DIM // 4
Q_BLOCK = 128
EPS = 1e-6

kernel_name = 'hybrid_dilated_diff_attn_gated_block'


def rmsnorm(x, g):
    xf = x.astype(jnp.float32)
    y = xf * lax.rsqrt(jnp.mean(xf * xf, axis=-1, keepdims=True) + EPS)
    return (y * g.astype(jnp.float32)).astype(x.dtype)


def rope_tables(seq):
    pos = jnp.arange(seq, dtype=jnp.float32)
    inv = ROPE_THETA ** (-jnp.arange(0, ROT_DIM, 2, dtype=jnp.float32) / ROT_DIM)
    ang = pos[:, None] * inv[None, :]
    return jnp.cos(ang), jnp.sin(ang)


def apply_rope(x, cos, sin):
    half = ROT_DIM // 2
    shp = (1, x.shape[1]) + (1,) * (x.ndim - 3) + (half,)
    c = cos.reshape(shp).astype(x.dtype)
    s = sin.reshape(shp).astype(x.dtype)
    x1, x2, rest = x[..., :half], x[..., half:ROT_DIM], x[..., ROT_DIM:]
    return jnp.concatenate([x1 * c - x2 * s, x2 * c + x1 * s, rest], axis=-1)


def dilated_window_attn(q, k, v, window, dilation):
    B, S, H, d = q.shape
    W = window // dilation
    L = S // dilation
    nb = -(-L // W)
    Lp = nb * W

    def to_sub(t):
        t = t.reshape(B, L, dilation, H, d).transpose(0, 3, 2, 1, 4)
        t = jnp.pad(t, ((0, 0), (0, 0), (0, 0), (0, Lp - L), (0, 0)))
        return t.reshape(B, H, dilation, nb, W, d)

    qb, kb, vb = to_sub(q), to_sub(k), to_sub(v)
    k_prev = jnp.concatenate([jnp.zeros_like(kb[:, :, :, :1]), kb[:, :, :, :-1]], axis=3)
    v_prev = jnp.concatenate([jnp.zeros_like(vb[:, :, :, :1]), vb[:, :, :, :-1]], axis=3)
    kk = jnp.concatenate([k_prev, kb], axis=4)
    vv = jnp.concatenate([v_prev, vb], axis=4)
    s = jnp.einsum('bhrnqd,bhrnkd->bhrnqk', qb, kk,
                   preferred_element_type=jnp.float32) * (1.0 / math.sqrt(d))
    qp = jnp.arange(W)[:, None]
    kp = jnp.arange(2 * W)[None, :]
    band = (kp >= qp) & (kp <= qp + W)
    has_prev = (jnp.arange(nb) > 0)[:, None, None]
    mask = band[None] & (has_prev | (kp >= W)[None])
    s = jnp.where(mask, s, -jnp.inf)
    lse = jax.nn.logsumexp(s, axis=-1, keepdims=True)
    p = jnp.exp(s - lse)
    out = jnp.einsum('bhrnqk,bhrnkd->bhrnqd', p, vv.astype(jnp.float32))
    out = out.reshape(B, H, dilation, Lp, d)[:, :, :, :L]
    out = out.transpose(0, 3, 2, 1, 4).reshape(B, S, H, d)
    lse = lse[..., 0].reshape(B, H, dilation, Lp)[:, :, :, :L]
    lse = lse.transpose(0, 3, 2, 1).reshape(B, S, H)
    return out, lse


def dilated_mixture(qa, ka, va):
    B, S, _, d = qa.shape
    outs, lses = [], []
    for g, (window, dilation) in enumerate(DIL_PAIRS):
        sl = slice(g * HEADS_PER_GROUP, (g + 1) * HEADS_PER_GROUP)
        o, l = dilated_window_attn(qa[:, :, sl], ka[:, :, sl], va[:, :, sl], window, dilation)
        outs.append(o)
        lses.append(l)
    alpha = jax.nn.softmax(jnp.stack(lses, axis=0), axis=0)
    mixed = jnp.sum(alpha[..., None] * jnp.stack(outs, axis=0), axis=0)
    return mixed.reshape(B, S, A_OUT).astype(qa.dtype)


def lambda_init(layer_idx):
    return 0.8 - 0.6 * math.exp(-0.3 * layer_idx)


def diff_attention(q, k, v, lam, lam_init, subln_g):
    B, S, H, _, d = q.shape
    nb = S // Q_BLOCK
    kT = k.transpose(0, 2, 3, 1, 4)
    vT = v.transpose(0, 2, 1, 3).astype(jnp.float32)
    qb = q.reshape(B, nb, Q_BLOCK, H, 2, d).transpose(1, 0, 3, 4, 2, 5)
    kpos = jnp.arange(S)
    scale = 1.0 / math.sqrt(d)

    def block(args):
        qblk, n = args
        s = jnp.einsum('bhiqd,bhikd->bhiqk', qblk, kT,
                       preferred_element_type=jnp.float32) * scale
        qpos = n * Q_BLOCK + jnp.arange(Q_BLOCK)
        s = jnp.where(kpos[None, :] <= qpos[:, None], s, -jnp.inf)
        p = jax.nn.softmax(s, axis=-1)
        a = p[:, :, 0] - lam * p[:, :, 1]
        return jnp.einsum('bhqk,bhkv->bhqv', a, vT)

    o = lax.map(block, (qb, jnp.arange(nb)))
    o = o.transpose(1, 0, 3, 2, 4).reshape(B, S, H, 2 * d)
    o = rmsnorm(o, subln_g) * (1.0 - lam_init)
    return o.reshape(B, S, H * 2 * d).astype(q.dtype)


def setup_inputs(seed: int = 0) -> dict:
    key = jax.random.key(seed)
    ks = jax.random.split(key, 16)
    f32 = jnp.float32

    def nrm(k, shape, scale):
        return jax.random.normal(k, shape, f32) * scale

    return {
        'x': nrm(ks[0], (BATCH, SEQ, D_MODEL), 1.0),
        'norm_mix': 1.0 + nrm(ks[1], (DEPTH, D_MODEL), 0.01),
        'w_in': nrm(ks[2], (DEPTH, D_MODEL, IN_COLS), D_MODEL ** -0.5),
        'b_gate': nrm(ks[3], (DEPTH, 2 * D_MODEL), 0.01),
        'w_proj_a': nrm(ks[4], (DEPTH, A_OUT, D_MODEL), A_OUT ** -0.5),
        'w_proj_b': nrm(ks[5], (DEPTH, B_V, D_MODEL), B_V ** -0.5),
        'w_out': nrm(ks[6], (DEPTH, D_MODEL, D_MODEL), D_MODEL ** -0.5),
        'lambda_q1': nrm(ks[7], (DEPTH, HEAD_DIM), 0.1),
        'lambda_k1': nrm(ks[8], (DEPTH, HEAD_DIM), 0.1),
        'lambda_q2': nrm(ks[9], (DEPTH, HEAD_DIM), 0.1),
        'lambda_k2': nrm(ks[10], (DEPTH, HEAD_DIM), 0.1),
        'subln_g': 1.0 + nrm(ks[11], (DEPTH, 2 * HEAD_DIM), 0.01),
        'norm_mlp': 1.0 + nrm(ks[12], (DEPTH, D_MODEL), 0.01),
        'w_ff1': nrm(ks[13], (DEPTH, D_MODEL, D_FF), D_MODEL ** -0.5),
        'w_ff2': nrm(ks[14], (DEPTH, D_FF, D_MODEL), D_FF ** -0.5),
        'norm_final': 1.0 + nrm(ks[15], (D_MODEL,), 0.01),
    }


def reference(x, norm_mix, w_in, b_gate, w_proj_a, w_proj_b, w_out, lambda_q1, lambda_k1,
              lambda_q2, lambda_k2, subln_g, norm_mlp, w_ff1, w_ff2, norm_final):
    B, S, _ = x.shape
    cos, sin = rope_tables(S)
    for l in range(DEPTH):
        h = rmsnorm(x, norm_mix[l])
        proj = jnp.einsum('bsd,de->bse', h, w_in[l])
        qa, ka, va, qb, kb, vb, ga, gb = jnp.split(proj, SPLITS, axis=-1)
        qa = apply_rope(qa.reshape(B, S, A_HEADS, HEAD_DIM), cos, sin)
        ka = apply_rope(ka.reshape(B, S, A_HEADS, HEAD_DIM), cos, sin)
        va = va.reshape(B, S, A_HEADS, HEAD_DIM)
        ya = dilated_mixture(qa, ka, va)
        qb = apply_rope(qb.reshape(B, S, B_HEADS, 2, HEAD_DIM), cos, sin)
        kb = apply_rope(kb.reshape(B, S, B_HEADS, 2, HEAD_DIM), cos, sin)
        vb = vb.reshape(B, S, B_HEADS, 2 * HEAD_DIM)
        lam_init = lambda_init(l)
        lam = (jnp.exp(jnp.sum(lambda_q1[l].astype(jnp.float32) * lambda_k1[l].astype(jnp.float32)))
               - jnp.exp(jnp.sum(lambda_q2[l].astype(jnp.float32) * lambda_k2[l].astype(jnp.float32)))
               + lam_init)
        yb = diff_attention(qb, kb, vb, lam, lam_init, subln_g[l])
        gate_a = jax.nn.sigmoid(ga + b_gate[l, :D_MODEL])
        gate_b = jax.nn.sigmoid(gb + b_gate[l, D_MODEL:])
        merged = (gate_a * jnp.einsum('bsc,cd->bsd', ya, w_proj_a[l])
                  + gate_b * jnp.einsum('bsc,cd->bsd', yb, w_proj_b[l]))
        x = x + jnp.einsum('bsd,de->bse', merged, w_out[l])
        h = rmsnorm(x, norm_mlp[l])
        u = jax.nn.relu(jnp.einsum('bsd,df->bsf', h, w_ff1[l]))
        x = x + jnp.einsum('bsf,fd->bsd', jnp.square(u), w_ff2[l])
    return rmsnorm(x, norm_final)
```

```python
import functools
import math

import jax
import jax.numpy as jnp
from jax import lax
from jax.experimental import pallas as pl
from jax.experimental.pallas import tpu as pltpu

D_MODEL = 4096
HEAD_DIM = 128
DIL_PAIRS = ((128, 1), (512, 4), (2048, 16))
N_GROUPS = len(DIL_PAIRS)
HEADS_PER_GROUP = 4
GROUP_COLS = HEADS_PER_GROUP * HEAD_DIM
A_WIDTH = N_GROUPS * GROUP_COLS
A_OUT = GROUP_COLS
B_HEADS = 8
B_QK = B_HEADS * 2 * HEAD_DIM
B_V = B_HEADS * 2 * HEAD_DIM
D_FF = 4 * D_MODEL
ROPE_THETA = 500000.0
ROT_DIM = HEAD_DIM // 4
ROT_HALF = ROT_DIM // 2
EPS = 1e-6
QK_SCALE = 1.0 / math.sqrt(HEAD_DIM)
WINDOW_STEPS = 128

COL_QA = 0
COL_KA = A_WIDTH
COL_VA = 2 * A_WIDTH
COL_QB = 3 * A_WIDTH
COL_KB = COL_QB + B_QK
COL_VB = COL_KB + B_QK
COL_GATE = COL_VB + B_V
IN_COLS = COL_GATE + 2 * D_MODEL

VMEM_LIMIT_BYTES = 48 * 1024 * 1024

F32 = jnp.float32
BF16 = jnp.bfloat16


def _params(*semantics):
    return pltpu.CompilerParams(dimension_semantics=semantics,
                                vmem_limit_bytes=VMEM_LIMIT_BYTES)


def _rmsnorm_kernel(x_ref, g_ref, o_ref):
    x = x_ref[...]
    ms = jnp.mean(x * x, axis=-1, keepdims=True)
    o_ref[...] = ((x * lax.rsqrt(ms + EPS)) * g_ref[...]).astype(o_ref.dtype)


def _rmsnorm(x, g, out_dtype, rows=256):
    s, d = x.shape
    return pl.pallas_call(
        _rmsnorm_kernel,
        grid=(s // rows,),
        in_specs=[pl.BlockSpec((rows, d), lambda i: (i, 0)),
                  pl.BlockSpec((1, d), lambda i: (0, 0))],
        out_specs=pl.BlockSpec((rows, d), lambda i: (i, 0)),
        out_shape=jax.ShapeDtypeStruct((s, d), out_dtype),
        compiler_params=_params("parallel"),
        name="rmsnorm",
    )(x, g.reshape(1, d))


def _rope_tables(seq):
    pos = jnp.arange(seq, dtype=F32)
    inv = ROPE_THETA ** (-jnp.arange(0, ROT_DIM, 2, dtype=F32) / ROT_DIM)
    ang = pos[:, None] * inv[None, :]
    cos, sin = jnp.cos(ang), jnp.sin(ang)
    ones = jnp.ones((seq, HEAD_DIM - ROT_DIM), F32)
    zeros = jnp.zeros((seq, HEAD_DIM - ROT_DIM), F32)
    zh = jnp.zeros((seq, ROT_HALF), F32)
    cos_t = jnp.concatenate([cos, cos, ones], axis=1)
    sin_lo = jnp.concatenate([-sin, zh, zeros], axis=1)
    sin_hi = jnp.concatenate([zh, sin, zeros], axis=1)
    return cos_t, sin_lo, sin_hi


def _rope_head(y, cos, sin_lo, sin_hi):
    x2 = pltpu.roll(y, HEAD_DIM - ROT_HALF, 1)
    x1 = pltpu.roll(y, ROT_HALF, 1)
    return y * cos + x2 * sin_lo + x1 * sin_hi


def _proj_a_kernel(h_ref, w_ref, cos_ref, slo_ref, shi_ref, o_ref, acc_ref, *, dilation):
    j = pl.program_id(1)
    tm = h_ref.shape[0]
    acc = jnp.dot(h_ref[...], w_ref[...], preferred_element_type=F32)

    @pl.when(j < 2)
    def _():
        scale = jnp.where(j == 0, QK_SCALE, 1.0).astype(F32)
        cos, slo, shi = cos_ref[...], slo_ref[...], shi_ref[...]
        for hh in range(HEADS_PER_GROUP):
            y = acc[:, hh * HEAD_DIM:(hh + 1) * HEAD_DIM]
            acc_ref[hh] = _rope_head(y, cos, slo, shi) * scale

    @pl.when(j == 2)
    def _():
        for hh in range(HEADS_PER_GROUP):
            acc_ref[hh] = acc[:, hh * HEAD_DIM:(hh + 1) * HEAD_DIM]

    n = tm // dilation
    for hh in range(HEADS_PER_GROUP):
        for rho in range(dilation):
            rows = pl.ds(rho, n, stride=dilation) if dilation > 1 else pl.ds(0, n)
            o_ref[0, hh, rho] = acc_ref[hh, rows, :].astype(o_ref.dtype)


def _proj_a(h, w_in, tables, group, tm=1024):
    s, d = h.shape
    dilation = DIL_PAIRS[group][1]
    sub_len = s // dilation
    blocks_per_section = A_WIDTH // GROUP_COLS
    tab_spec = pl.BlockSpec((tm, HEAD_DIM), lambda i, j: (i, 0))
    return pl.pallas_call(
        functools.partial(_proj_a_kernel, dilation=dilation),
        grid=(s // tm, 3),
        in_specs=[pl.BlockSpec((tm, d), lambda i, j: (i, 0)),
                  pl.BlockSpec((d, GROUP_COLS), lambda i, j: (0, j * blocks_per_section + group)),
                  tab_spec, tab_spec, tab_spec],
        out_specs=pl.BlockSpec((1, HEADS_PER_GROUP, dilation, tm // dilation, HEAD_DIM),
                               lambda i, j: (j, 0, 0, i, 0)),
        out_shape=jax.ShapeDtypeStruct((3, HEADS_PER_GROUP, dilation, sub_len, HEAD_DIM), BF16),
        scratch_shapes=[pltpu.VMEM((HEADS_PER_GROUP, tm, HEAD_DIM), F32)],
        compiler_params=_params("parallel", "arbitrary"),
        name=f"proj_a_g{group}",
    )(h, w_in, *tables)


def _proj_bqk_kernel(h_ref, w_ref, cos_ref, slo_ref, shi_ref, o_ref, *, scale):
    acc = jnp.dot(h_ref[...], w_ref[...], preferred_element_type=F32)
    cos, slo, shi = cos_ref[...], slo_ref[...], shi_ref[...]
    for hh in range(o_ref.shape[0]):
        y = _rope_head(acc[:, hh * HEAD_DIM:(hh + 1) * HEAD_DIM], cos, slo, shi)
        o_ref[hh] = (y * scale).astype(o_ref.dtype)


def _proj_bqk(h, w_in, tables, col0, scale, tm=1024, tn=512):
    s, d = h.shape
    heads_per_tile = tn // HEAD_DIM
    tab_spec = pl.BlockSpec((tm, HEAD_DIM), lambda i, j: (i, 0))
    return pl.pallas_call(
        functools.partial(_proj_bqk_kernel, scale=scale),
        grid=(s // tm, B_QK // tn),
        in_specs=[pl.BlockSpec((tm, d), lambda i, j: (i, 0)),
                  pl.BlockSpec((d, tn), lambda i, j: (0, col0 // tn + j)),
                  tab_spec, tab_spec, tab_spec],
        out_specs=pl.BlockSpec((heads_per_tile, tm, HEAD_DIM), lambda i, j: (j, i, 0)),
        out_shape=jax.ShapeDtypeStruct((B_QK // HEAD_DIM, s, HEAD_DIM), BF16),
        compiler_params=_params("parallel", "arbitrary"),
        name="proj_b_qk",
    )(h, w_in, *tables)


def _proj_bv_kernel(h_ref, w_ref, o_ref):
    acc = jnp.dot(h_ref[...], w_ref[...], preferred_element_type=F32)
    width = o_ref.shape[2]
    for hv in range(o_ref.shape[0]):
        o_ref[hv] = acc[:, hv * width:(hv + 1) * width].astype(o_ref.dtype)


def _proj_bv(h, w_in, tm=1024, tn=512):
    s, d = h.shape
    width = 2 * HEAD_DIM
    return pl.pallas_call(
        _proj_bv_kernel,
        grid=(s // tm, B_V // tn),
        in_specs=[pl.BlockSpec((tm, d), lambda i, j: (i, 0)),
                  pl.BlockSpec((d, tn), lambda i, j: (0, COL_VB // tn + j))],
        out_specs=pl.BlockSpec((tn // width, tm, width), lambda i, j: (j, i, 0)),
        out_shape=jax.ShapeDtypeStruct((B_HEADS, s, width), BF16),
        compiler_params=_params("parallel", "arbitrary"),
        name="proj_b_v",
    )(h, w_in)


def _proj_gate_kernel(h_ref, w_ref, b_ref, o_ref):
    acc = jnp.dot(h_ref[...], w_ref[...], preferred_element_type=F32)
    o_ref[...] = jax.nn.sigmoid(acc + b_ref[...]).astype(o_ref.dtype)


def _proj_gate(h, w_in, b_gate, tm=1024, tn=512):
    s, d = h.shape
    n = 2 * D_MODEL
    return pl.pallas_call(
        _proj_gate_kernel,
        grid=(s // tm, n // tn),
        in_specs=[pl.BlockSpec((tm, d), lambda i, j: (i, 0)),
                  pl.BlockSpec((d, tn), lambda i, j: (0, COL_GATE // tn + j)),
                  pl.BlockSpec((1, tn), lambda i, j: (0, j))],
        out_specs=pl.BlockSpec((tm, tn), lambda i, j: (i, j)),
        out_shape=jax.ShapeDtypeStruct((s, n), BF16),
        compiler_params=_params("parallel", "arbitrary"),
        name="proj_gate",
    )(h, w_in, b_gate.reshape(1, n))


def _attn_a_kernel(q_ref, kc_ref, kp_ref, vc_ref, vp_ref, o_ref, lse_ref, kk_ref, vv_ref,
                   *, chunks_per_seq):
    w = WINDOW_STEPS
    t = q_ref.shape[1]
    c = pl.program_id(0)
    kk_ref[0:w] = kp_ref[0]
    kk_ref[w:] = kc_ref[0]
    vv_ref[0:w] = vp_ref[0]
    vv_ref[w:] = vc_ref[0]
    has_prev = (c % chunks_per_seq) != 0
    qp = lax.broadcasted_iota(jnp.int32, (w, 2 * w), 0)
    kp = lax.broadcasted_iota(jnp.int32, (w, 2 * w), 1)
    band = (kp >= qp) & (kp <= qp + w)
    band_first = band & (kp >= jnp.where(has_prev, 0, w))
    for b in range(t // w):
        q = q_ref[0, b * w:(b + 1) * w]
        k = kk_ref[b * w:(b + 2) * w]
        v = vv_ref[b * w:(b + 2) * w]
        s = lax.dot_general(q, k, (((1,), (1,)), ((), ())), preferred_element_type=F32)
        s = jnp.where(band_first if b == 0 else band, s, -jnp.inf)
        m = jnp.max(s, axis=-1, keepdims=True)
        p = jnp.exp(s - m)
        l = jnp.sum(p, axis=-1, keepdims=True)
        o = jnp.dot(p.astype(v.dtype), v, preferred_element_type=F32)
        o_ref[b * w:(b + 1) * w] = o / l
        lse_ref[b * w:(b + 1) * w] = jnp.broadcast_to(m + jnp.log(l), (w, HEAD_DIM))


def _attn_a(qkv, t=512):
    _, heads, dilation, sub_len, hd = qkv.shape
    rows = heads * dilation * sub_len
    flat = qkv.reshape(3, rows, hd)
    w = WINDOW_STEPS
    t = min(t, sub_len)
    bpc = t // w
    cur = lambda sec: pl.BlockSpec((1, t, hd), lambda c: (sec, c, 0))
    prev = lambda sec: pl.BlockSpec((1, w, hd), lambda c: (sec, jnp.maximum(c * bpc - 1, 0), 0))
    out_spec = pl.BlockSpec((t, hd), lambda c: (c, 0))
    out, lse = pl.pallas_call(
        functools.partial(_attn_a_kernel, chunks_per_seq=sub_len // t),
        grid=(rows // t,),
        in_specs=[cur(0), cur(1), prev(1), cur(2), prev(2)],
        out_specs=[out_spec, out_spec],
        out_shape=[jax.ShapeDtypeStruct((rows, hd), F32)] * 2,
        scratch_shapes=[pltpu.VMEM((t + w, hd), BF16)] * 2,
        compiler_params=_params("parallel"),
        name=f"attn_a_r{dilation}",
    )(flat, flat, flat, flat, flat)
    shape = (heads, dilation, sub_len, hd)
    return out.reshape(shape), lse.reshape(shape)


def _mix_a_kernel(o0_ref, l0_ref, o1_ref, l1_ref, o2_ref, l2_ref, y_ref, nat_ref):
    tm = y_ref.shape[0]
    refs = ((o0_ref, l0_ref), (o1_ref, l1_ref), (o2_ref, l2_ref))
    r_max = DIL_PAIRS[-1][1]
    n = tm // r_max
    for hh in range(HEADS_PER_GROUP):
        for rho in range(r_max):
            outs, lses = [], []
            for (o_ref, l_ref), (_, r) in zip(refs, DIL_PAIRS):
                step = r_max // r
                rows = pl.ds(rho // r, n, stride=step) if step > 1 else pl.ds(0, n)
                outs.append(o_ref[hh, rho % r, rows, :])
                lses.append(l_ref[hh, rho % r, rows, :])
            m = jnp.maximum(jnp.maximum(lses[0], lses[1]), lses[2])
            ws = [jnp.exp(l - m) for l in lses]
            num = ws[0] * outs[0] + ws[1] * outs[1] + ws[2] * outs[2]
            den = ws[0] + ws[1] + ws[2]
            nat_ref[hh, pl.ds(rho, n, stride=r_max), :] = num / den
        y_ref[:, hh * HEAD_DIM:(hh + 1) * HEAD_DIM] = nat_ref[hh].astype(y_ref.dtype)


def _mix_a(group_outs, seq, tm=512):
    in_specs, args = [], []
    for (o, l), (_, r) in zip(group_outs, DIL_PAIRS):
        spec = pl.BlockSpec((HEADS_PER_GROUP, r, tm // r, HEAD_DIM), lambda i: (0, 0, i, 0))
        in_specs += [spec, spec]
        args += [o, l]
    return pl.pallas_call(
        _mix_a_kernel,
        grid=(seq // tm,),
        in_specs=in_specs,
        out_specs=pl.BlockSpec((tm, A_OUT), lambda i: (i, 0)),
        out_shape=jax.ShapeDtypeStruct((seq, A_OUT), BF16),
        scratch_shapes=[pltpu.VMEM((HEADS_PER_GROUP, tm, HEAD_DIM), F32)],
        compiler_params=_params("parallel"),
        name="mix_a",
    )(*args)


def _attn_b_kernel(lq1_ref, lk1_ref, lq2_ref, lk2_ref, g_ref, q_ref, k_ref, v_ref, o_ref,
                   m_ref, l_ref, acc_ref, *, lam_init):
    tq = q_ref.shape[1]
    qi = pl.program_id(1)
    m_ref[...] = jnp.full(m_ref.shape, -jnp.inf, F32)
    l_ref[...] = jnp.zeros(l_ref.shape, F32)
    acc_ref[...] = jnp.zeros(acc_ref.shape, F32)

    def step(kv, masked):
        start = pl.multiple_of(kv * tq, tq)
        v = v_ref[0, pl.ds(start, tq), :]
        for i in range(2):
            k = k_ref[i, pl.ds(start, tq), :]
            s = lax.dot_general(q_ref[i], k, (((1,), (1,)), ((), ())),
                                preferred_element_type=F32)
            if masked:
                row = lax.broadcasted_iota(jnp.int32, s.shape, 0)
                col = lax.broadcasted_iota(jnp.int32, s.shape, 1)
                s = jnp.where(col <= row, s, -jnp.inf)
            m_prev = m_ref[i]
            m_new = jnp.maximum(m_prev, jnp.max(s, axis=-1, keepdims=True))
            alpha = jnp.exp(m_prev - m_new)
            p = jnp.exp(s - m_new)
            l_ref[i] = alpha * l_ref[i] + jnp.sum(p, axis=-1, keepdims=True)
            acc_ref[i] = alpha * acc_ref[i] + jnp.dot(p.astype(v.dtype), v,
                                                      preferred_element_type=F32)
            m_ref[i] = m_new

    def body(kv, carry):
        step(kv, masked=False)
        return carry

    lax.fori_loop(0, qi, body, 0)
    step(qi, masked=True)

    lam = (jnp.exp(jnp.sum(lq1_ref[...] * lk1_ref[...], axis=-1, keepdims=True))
           - jnp.exp(jnp.sum(lq2_ref[...] * lk2_ref[...], axis=-1, keepdims=True))
           + lam_init)
    o = acc_ref[0] / l_ref[0] - lam * (acc_ref[1] / l_ref[1])
    ms = jnp.mean(o * o, axis=-1, keepdims=True)
    y = (o * lax.rsqrt(ms + EPS)) * g_ref[...]
    o_ref[...] = (y * (1.0 - lam_init)).astype(o_ref.dtype)


def _attn_b(q, k, v, lam_vecs, subln_g, lam_init, tq=512):
    _, s, hd = q.shape
    width = 2 * hd
    vec = pl.BlockSpec((1, hd), lambda h, i: (0, 0))
    return pl.pallas_call(
        functools.partial(_attn_b_kernel, lam_init=lam_init),
        grid=(B_HEADS, s // tq),
        in_specs=[vec, vec, vec, vec,
                  pl.BlockSpec((1, width), lambda h, i: (0, 0)),
                  pl.BlockSpec((2, tq, hd), lambda h, i: (h, i, 0)),
                  pl.BlockSpec((2, s, hd), lambda h, i: (h, 0, 0)),
                  pl.BlockSpec((1, s, width), lambda h, i: (h, 0, 0))],
        out_specs=pl.BlockSpec((tq, width), lambda h, i: (i, h)),
        out_shape=jax.ShapeDtypeStruct((s, B_HEADS * width), BF16),
        scratch_shapes=[pltpu.VMEM((2, tq, 1), F32), pltpu.VMEM((2, tq, 1), F32),
                        pltpu.VMEM((2, tq, width), F32)],
        compiler_params=_params("parallel", "arbitrary"),
        name="attn_b",
    )(*[x.reshape(1, hd) for x in lam_vecs], subln_g.reshape(1, width), q, k, v)


def _merge_kernel(ya_ref, yb_ref, wa_ref, wb_ref, ga_ref, gb_ref, o_ref):
    pa = jnp.dot(ya_ref[...], wa_ref[...], preferred_element_type=F32)
    pb = jnp.dot(yb_ref[...], wb_ref[...], preferred_element_type=F32)
    o_ref[...] = (ga_ref[...].astype(F32) * pa + gb_ref[...].astype(F32) * pb).astype(o_ref.dtype)


def _merge(ya, yb, wa, wb, gates, tm=1024, tn=1024):
    s = ya.shape[0]
    n = wa.shape[1]
    return pl.pallas_call(
        _merge_kernel,
        grid=(s // tm, n // tn),
        in_specs=[pl.BlockSpec((tm, ya.shape[1]), lambda i, j: (i, 0)),
                  pl.BlockSpec((tm, yb.shape[1]), lambda i, j: (i, 0)),
                  pl.BlockSpec((wa.shape[0], tn), lambda i, j: (0, j)),
                  pl.BlockSpec((wb.shape[0], tn), lambda i, j: (0, j)),
                  pl.BlockSpec((tm, tn), lambda i, j: (i, j)),
                  pl.BlockSpec((tm, tn), lambda i, j: (i, n // tn + j))],
        out_specs=pl.BlockSpec((tm, tn), lambda i, j: (i, j)),
        out_shape=jax.ShapeDtypeStruct((s, n), BF16),
        compiler_params=_params("parallel", "arbitrary"),
        name="merge",
    )(ya, yb, wa, wb, gates, gates)


def _matmul_residual_kernel(a_ref, b_ref, x_ref, o_ref):
    o_ref[...] = x_ref[...] + jnp.dot(a_ref[...], b_ref[...], preferred_element_type=F32)


def _matmul_residual(a, b, x, tm=1024, tn=512):
    s, kdim = a.shape
    n = b.shape[1]
    return pl.pallas_call(
        _matmul_residual_kernel,
        grid=(s // tm, n // tn),
        in_specs=[pl.BlockSpec((tm, kdim), lambda i, j: (i, 0)),
                  pl.BlockSpec((kdim, tn), lambda i, j: (0, j)),
                  pl.BlockSpec((tm, tn), lambda i, j: (i, j))],
        out_specs=pl.BlockSpec((tm, tn), lambda i, j: (i, j)),
        out_shape=jax.ShapeDtypeStruct((s, n), F32),
        compiler_params=_params("parallel", "arbitrary"),
        name="out_proj",
    )(a, b, x)


def _ff1_kernel(a_ref, b_ref, o_ref):
    u = jnp.maximum(jnp.dot(a_ref[...], b_ref[...], preferred_element_type=F32), 0.0)
    o_ref[...] = (u * u).astype(o_ref.dtype)


def _ff1(a, b, tm=1024, tn=512):
    s, kdim = a.shape
    n = b.shape[1]
    return pl.pallas_call(
        _ff1_kernel,
        grid=(s // tm, n // tn),
        in_specs=[pl.BlockSpec((tm, kdim), lambda i, j: (i, 0)),
                  pl.BlockSpec((kdim, tn), lambda i, j: (0, j))],
        out_specs=pl.BlockSpec((tm, tn), lambda i, j: (i, j)),
        out_shape=jax.ShapeDtypeStruct((s, n), BF16),
        compiler_params=_params("parallel", "arbitrary"),
        name="ff1",
    )(a, b)


def _ff2_kernel(a_ref, b_ref, x_ref, o_ref):
    k = pl.program_id(2)
    part = jnp.dot(a_ref[...], b_ref[...], preferred_element_type=F32)

    @pl.when(k == 0)
    def _():
        o_ref[...] = x_ref[...] + part

    @pl.when(k > 0)
    def _():
        o_ref[...] += part


def _ff2(a, b, x, tm=1024, tn=1024, tk=2048):
    s, kdim = a.shape
    n = b.shape[1]
    return pl.pallas_call(
        _ff2_kernel,
        grid=(s // tm, n // tn, kdim // tk),
        in_specs=[pl.BlockSpec((tm, tk), lambda i, j, k: (i, k)),
                  pl.BlockSpec((tk, tn), lambda i, j, k: (k, j)),
                  pl.BlockSpec((tm, tn), lambda i, j, k: (i, j))],
        out_specs=pl.BlockSpec((tm, tn), lambda i, j, k: (i, j)),
        out_shape=jax.ShapeDtypeStruct((s, n), F32),
        compiler_params=_params("parallel", "parallel", "arbitrary"),
        name="ff2",
    )(a, b, x)


def _lambda_init(layer_idx):
    return 0.8 - 0.6 * math.exp(-0.3 * layer_idx)


def kernel(x, norm_mix, w_in, b_gate, w_proj_a, w_proj_b, w_out, lambda_q1, lambda_k1,
           lambda_q2, lambda_k2, subln_g, norm_mlp, w_ff1, w_ff2, norm_final):
    batch, seq, d = x.shape
    depth = w_in.shape[0]
    tables = _rope_tables(seq)
    outs = []
    for b in range(batch):
        xb = x.reshape(seq, d) if batch == 1 else x[b]
        for l in range(depth):
            w_in_l = w_in[l].astype(BF16)
            h = _rmsnorm(xb, norm_mix[l], BF16)
            groups = []
            for g in range(N_GROUPS):
                groups.append(_attn_a(_proj_a(h, w_in_l, tables, g)))
            ya = _mix_a(groups, seq)
            qb = _proj_bqk(h, w_in_l, tables, COL_QB, QK_SCALE)
            kb = _proj_bqk(h, w_in_l, tables, COL_KB, 1.0)
            vb = _proj_bv(h, w_in_l)
            lam_init = _lambda_init(l)
            yb = _attn_b(qb, kb, vb, (lambda_q1[l], lambda_k1[l], lambda_q2[l], lambda_k2[l]),
                         subln_g[l], lam_init)
            gates = _proj_gate(h, w_in_l, b_gate[l])
            merged = _merge(ya, yb, w_proj_a[l].astype(BF16), w_proj_b[l].astype(BF16), gates)
            xb = _matmul_residual(merged, w_out[l].astype(BF16), xb)
            h2 = _rmsnorm(xb, norm_mlp[l], BF16)
            u = _ff1(h2, w_ff1[l].astype(BF16))
            xb = _ff2(u, w_ff2[l].astype(BF16), xb)
        outs.append(_rmsnorm(xb, norm_final, x.dtype))
    if batch == 1:
        return outs[0].reshape(1, seq, d)
    return jnp.stack(outs, axis=0)
```

```python
import functools
import math

import jax
import jax.numpy as jnp
from jax import lax
from jax.experimental import pallas as pl
from jax.experimental.pallas import tpu as pltpu

D_MODEL = 4096
HEAD_DIM = 128
DIL_PAIRS = ((128, 1), (512, 4), (2048, 16))
N_GROUPS = len(DIL_PAIRS)
HEADS_PER_GROUP = 4
GROUP_COLS = HEADS_PER_GROUP * HEAD_DIM
A_WIDTH = N_GROUPS * GROUP_COLS
A_OUT = GROUP_COLS
B_HEADS = 8
B_QK = B_HEADS * 2 * HEAD_DIM
B_V = B_HEADS * 2 * HEAD_DIM
D_FF = 4 * D_MODEL
ROPE_THETA = 500000.0
ROT_DIM = HEAD_DIM // 4
ROT_HALF = ROT_DIM // 2
EPS = 1e-6
QK_SCALE = 1.0 / math.sqrt(HEAD_DIM)
LOG2_E = math.log2(math.e)
WINDOW_STEPS = 128
ONES_ROWS = 16

COL_QA = 0
COL_KA = A_WIDTH
COL_VA = 2 * A_WIDTH
COL_QB = 3 * A_WIDTH
COL_KB = COL_QB + B_QK
COL_VB = COL_KB + B_QK
COL_GATE = COL_VB + B_V
IN_COLS = COL_GATE + 2 * D_MODEL

VMEM_LIMIT_BYTES = 48 * 1024 * 1024

F32 = jnp.float32
BF16 = jnp.bfloat16


def _params(*semantics):
    return pltpu.CompilerParams(dimension_semantics=semantics,
                                vmem_limit_bytes=VMEM_LIMIT_BYTES)


def _rmsnorm_kernel(x_ref, g_ref, o_ref):
    x = x_ref[...]
    ms = jnp.mean(x * x, axis=-1, keepdims=True)
    o_ref[...] = ((x * lax.rsqrt(ms + EPS)) * g_ref[...]).astype(o_ref.dtype)


def _rmsnorm(x, g, out_dtype, rows=256):
    s, d = x.shape
    return pl.pallas_call(
        _rmsnorm_kernel,
        grid=(s // rows,),
        in_specs=[pl.BlockSpec((rows, d), lambda i: (i, 0)),
                  pl.BlockSpec((1, d), lambda i: (0, 0))],
        out_specs=pl.BlockSpec((rows, d), lambda i: (i, 0)),
        out_shape=jax.ShapeDtypeStruct((s, d), out_dtype),
        compiler_params=_params("parallel"),
        name="rmsnorm",
    )(x, g.reshape(1, d))


def _rope_tables(seq):
    pos = jnp.arange(seq, dtype=F32)
    inv = ROPE_THETA ** (-jnp.arange(0, ROT_DIM, 2, dtype=F32) / ROT_DIM)
    ang = pos[:, None] * inv[None, :]
    cos, sin = jnp.cos(ang), jnp.sin(ang)
    ones = jnp.ones((seq, HEAD_DIM - ROT_DIM), F32)
    zeros = jnp.zeros((seq, HEAD_DIM - ROT_DIM), F32)
    zh = jnp.zeros((seq, ROT_HALF), F32)
    cos_t = jnp.concatenate([cos, cos, ones], axis=1)
    sin_lo = jnp.concatenate([-sin, zh, zeros], axis=1)
    sin_hi = jnp.concatenate([zh, sin, zeros], axis=1)
    return cos_t, sin_lo, sin_hi


def _rope_head(y, cos, sin_lo, sin_hi):
    x2 = pltpu.roll(y, HEAD_DIM - ROT_HALF, 1)
    x1 = pltpu.roll(y, ROT_HALF, 1)
    return y * cos + x2 * sin_lo + x1 * sin_hi


def _proj_a_kernel(h_ref, w_ref, cos_ref, slo_ref, shi_ref, o_ref, acc_ref, *, dilation):
    j = pl.program_id(1)
    tm = h_ref.shape[0]
    acc = jnp.dot(h_ref[...], w_ref[...].astype(BF16), preferred_element_type=F32)

    @pl.when(j < 2)
    def _():
        scale = jnp.where(j == 0, QK_SCALE, 1.0).astype(F32)
        cos, slo, shi = cos_ref[...], slo_ref[...], shi_ref[...]
        for hh in range(HEADS_PER_GROUP):
            y = acc[:, hh * HEAD_DIM:(hh + 1) * HEAD_DIM]
            acc_ref[hh] = _rope_head(y, cos, slo, shi) * scale

    @pl.when(j == 2)
    def _():
        for hh in range(HEADS_PER_GROUP):
            acc_ref[hh] = acc[:, hh * HEAD_DIM:(hh + 1) * HEAD_DIM]

    n = tm // dilation
    for hh in range(HEADS_PER_GROUP):
        for rho in range(dilation):
            rows = pl.ds(rho, n, stride=dilation) if dilation > 1 else pl.ds(0, n)
            o_ref[0, hh, rho] = acc_ref[hh, rows, :].astype(o_ref.dtype)


def _proj_a(h, w_in, tables, group, tm=1024):
    s, d = h.shape
    dilation = DIL_PAIRS[group][1]
    sub_len = s // dilation
    blocks_per_section = A_WIDTH // GROUP_COLS
    tab_spec = pl.BlockSpec((tm, HEAD_DIM), lambda i, j: (i, 0))
    return pl.pallas_call(
        functools.partial(_proj_a_kernel, dilation=dilation),
        grid=(s // tm, 3),
        in_specs=[pl.BlockSpec((tm, d), lambda i, j: (i, 0)),
                  pl.BlockSpec((d, GROUP_COLS), lambda i, j: (0, j * blocks_per_section + group)),
                  tab_spec, tab_spec, tab_spec],
        out_specs=pl.BlockSpec((1, HEADS_PER_GROUP, dilation, tm // dilation, HEAD_DIM),
                               lambda i, j: (j, 0, 0, i, 0)),
        out_shape=jax.ShapeDtypeStruct((3, HEADS_PER_GROUP, dilation, sub_len, HEAD_DIM), BF16),
        scratch_shapes=[pltpu.VMEM((HEADS_PER_GROUP, tm, HEAD_DIM), F32)],
        compiler_params=_params("parallel", "arbitrary"),
        name=f"proj_a_g{group}",
    )(h, w_in, *tables)


def _proj_bqk_kernel(h_ref, w_ref, cos_ref, slo_ref, shi_ref, o_ref, *, scale):
    acc = jnp.dot(h_ref[...], w_ref[...].astype(BF16), preferred_element_type=F32)
    cos, slo, shi = cos_ref[...], slo_ref[...], shi_ref[...]
    for hh in range(o_ref.shape[0]):
        y = _rope_head(acc[:, hh * HEAD_DIM:(hh + 1) * HEAD_DIM], cos, slo, shi)
        o_ref[hh] = (y * scale).astype(o_ref.dtype)


def _proj_bqk(h, w_in, tables, col0, scale, tm=1024, tn=512):
    s, d = h.shape
    heads_per_tile = tn // HEAD_DIM
    tab_spec = pl.BlockSpec((tm, HEAD_DIM), lambda i, j: (i, 0))
    return pl.pallas_call(
        functools.partial(_proj_bqk_kernel, scale=scale),
        grid=(s // tm, B_QK // tn),
        in_specs=[pl.BlockSpec((tm, d), lambda i, j: (i, 0)),
                  pl.BlockSpec((d, tn), lambda i, j: (0, col0 // tn + j)),
                  tab_spec, tab_spec, tab_spec],
        out_specs=pl.BlockSpec((heads_per_tile, tm, HEAD_DIM), lambda i, j: (j, i, 0)),
        out_shape=jax.ShapeDtypeStruct((B_QK // HEAD_DIM, s, HEAD_DIM), BF16),
        compiler_params=_params("parallel", "arbitrary"),
        name="proj_b_qk",
    )(h, w_in, *tables)


def _proj_bv_kernel(h_ref, w_ref, o_ref):
    acc = jnp.dot(h_ref[...], w_ref[...].astype(BF16), preferred_element_type=F32)
    heads, blocks, rows, tk = o_ref.shape
    width = rows - ONES_ROWS
    for hv in range(heads):
        vt = acc[:, hv * width:(hv + 1) * width].T
        for c in range(blocks):
            o_ref[hv, c, :width, :] = vt[:, c * tk:(c + 1) * tk].astype(o_ref.dtype)
            o_ref[hv, c, width:, :] = jnp.ones((ONES_ROWS, tk), o_ref.dtype)


def _proj_bv(h, w_in, tm=1024, tn=512, tk=512):
    s, d = h.shape
    width = 2 * HEAD_DIM
    rows = width + ONES_ROWS
    return pl.pallas_call(
        _proj_bv_kernel,
        grid=(s // tm, B_V // tn),
        in_specs=[pl.BlockSpec((tm, d), lambda i, j: (i, 0)),
                  pl.BlockSpec((d, tn), lambda i, j: (0, COL_VB // tn + j))],
        out_specs=pl.BlockSpec((tn // width, tm // tk, rows, tk), lambda i, j: (j, i, 0, 0)),
        out_shape=jax.ShapeDtypeStruct((B_HEADS, s // tk, rows, tk), BF16),
        compiler_params=_params("parallel", "arbitrary"),
        name="proj_b_v",
    )(h, w_in)


def _proj_gate_kernel(h_ref, w_ref, b_ref, o_ref):
    acc = jnp.dot(h_ref[...], w_ref[...].astype(BF16), preferred_element_type=F32)
    o_ref[...] = jax.nn.sigmoid(acc + b_ref[...]).astype(o_ref.dtype)


def _proj_gate(h, w_in, b_gate, tm=1024, tn=512):
    s, d = h.shape
    n = 2 * D_MODEL
    return pl.pallas_call(
        _proj_gate_kernel,
        grid=(s // tm, n // tn),
        in_specs=[pl.BlockSpec((tm, d), lambda i, j: (i, 0)),
                  pl.BlockSpec((d, tn), lambda i, j: (0, COL_GATE // tn + j)),
                  pl.BlockSpec((1, tn), lambda i, j: (0, j))],
        out_specs=pl.BlockSpec((tm, tn), lambda i, j: (i, j)),
        out_shape=jax.ShapeDtypeStruct((s, n), BF16),
        compiler_params=_params("parallel", "arbitrary"),
        name="proj_gate",
    )(h, w_in, b_gate.reshape(1, n))


def _attn_a_kernel(q_ref, kc_ref, kp_ref, vc_ref, vp_ref, o_ref, lse_ref, kk_ref, vv_ref,
                   *, chunks_per_seq):
    w = WINDOW_STEPS
    t = q_ref.shape[1]
    c = pl.program_id(0)
    kk_ref[0:w] = kp_ref[0]
    kk_ref[w:] = kc_ref[0]
    vv_ref[0:w] = vp_ref[0]
    vv_ref[w:] = vc_ref[0]
    has_prev = (c % chunks_per_seq) != 0
    qp = lax.broadcasted_iota(jnp.int32, (w, 2 * w), 0)
    kp = lax.broadcasted_iota(jnp.int32, (w, 2 * w), 1)
    band = (kp >= qp) & (kp <= qp + w)
    band_first = band & (kp >= jnp.where(has_prev, 0, w))
    for b in range(t // w):
        q = q_ref[0, b * w:(b + 1) * w]
        k = kk_ref[b * w:(b + 2) * w]
        v = vv_ref[b * w:(b + 2) * w]
        s = lax.dot_general(q, k, (((1,), (1,)), ((), ())), preferred_element_type=F32)
        s = jnp.where(band_first if b == 0 else band, s, -jnp.inf)
        m = jnp.max(s, axis=-1, keepdims=True)
        p = jnp.exp(s - m)
        l = jnp.sum(p, axis=-1, keepdims=True)
        o = jnp.dot(p.astype(v.dtype), v, preferred_element_type=F32)
        o_ref[b * w:(b + 1) * w] = o / l
        lse_ref[b * w:(b + 1) * w] = jnp.broadcast_to(m + jnp.log(l), (w, HEAD_DIM))


def _attn_a(qkv, t=512):
    _, heads, dilation, sub_len, hd = qkv.shape
    rows = heads * dilation * sub_len
    flat = qkv.reshape(3, rows, hd)
    w = WINDOW_STEPS
    t = min(t, sub_len)
    bpc = t // w
    cur = lambda sec: pl.BlockSpec((1, t, hd), lambda c: (sec, c, 0))
    prev = lambda sec: pl.BlockSpec((1, w, hd), lambda c: (sec, jnp.maximum(c * bpc - 1, 0), 0))
    out_spec = pl.BlockSpec((t, hd), lambda c: (c, 0))
    out, lse = pl.pallas_call(
        functools.partial(_attn_a_kernel, chunks_per_seq=sub_len // t),
        grid=(rows // t,),
        in_specs=[cur(0), cur(1), prev(1), cur(2), prev(2)],
        out_specs=[out_spec, out_spec],
        out_shape=[jax.ShapeDtypeStruct((rows, hd), F32)] * 2,
        scratch_shapes=[pltpu.VMEM((t + w, hd), BF16)] * 2,
        compiler_params=_params("parallel"),
        name=f"attn_a_r{dilation}",
    )(flat, flat, flat, flat, flat)
    shape = (heads, dilation, sub_len, hd)
    return out.reshape(shape), lse.reshape(shape)


def _mix_a_kernel(o0_ref, l0_ref, o1_ref, l1_ref, o2_ref, l2_ref, y_ref, nat_ref):
    tm = y_ref.shape[0]
    refs = ((o0_ref, l0_ref), (o1_ref, l1_ref), (o2_ref, l2_ref))
    r_max = DIL_PAIRS[-1][1]
    n = tm // r_max
    for hh in range(HEADS_PER_GROUP):
        for rho in range(r_max):
            outs, lses = [], []
            for (o_ref, l_ref), (_, r) in zip(refs, DIL_PAIRS):
                step = r_max // r
                rows = pl.ds(rho // r, n, stride=step) if step > 1 else pl.ds(0, n)
                outs.append(o_ref[hh, rho % r, rows, :])
                lses.append(l_ref[hh, rho % r, rows, :])
            m = jnp.maximum(jnp.maximum(lses[0], lses[1]), lses[2])
            ws = [jnp.exp(l - m) for l in lses]
            num = ws[0] * outs[0] + ws[1] * outs[1] + ws[2] * outs[2]
            den = ws[0] + ws[1] + ws[2]
            nat_ref[hh, pl.ds(rho, n, stride=r_max), :] = num / den
        y_ref[:, hh * HEAD_DIM:(hh + 1) * HEAD_DIM] = nat_ref[hh].astype(y_ref.dtype)


def _mix_a(group_outs, seq, tm=512):
    in_specs, args = [], []
    for (o, l), (_, r) in zip(group_outs, DIL_PAIRS):
        spec = pl.BlockSpec((HEADS_PER_GROUP, r, tm // r, HEAD_DIM), lambda i: (0, 0, i, 0))
        in_specs += [spec, spec]
        args += [o, l]
    return pl.pallas_call(
        _mix_a_kernel,
        grid=(seq // tm,),
        in_specs=in_specs,
        out_specs=pl.BlockSpec((tm, A_OUT), lambda i: (i, 0)),
        out_shape=jax.ShapeDtypeStruct((seq, A_OUT), BF16),
        scratch_shapes=[pltpu.VMEM((HEADS_PER_GROUP, tm, HEAD_DIM), F32)],
        compiler_params=_params("parallel"),
        name="mix_a",
    )(*args)


def _attn_b_kernel(lq1_ref, lk1_ref, lq2_ref, lk2_ref, g_ref, q_ref, k_ref, vt_ref, o_ref,
                   m_ref, acc_ref, st_ref, *, lam_init):
    tq = q_ref.shape[1]
    tk = vt_ref.shape[3]
    width = vt_ref.shape[2] - ONES_ROWS
    qi = pl.program_id(1)
    m_ref[...] = jnp.full(m_ref.shape, -jnp.inf, F32)
    acc_ref[...] = jnp.zeros(acc_ref.shape, F32)

    def scores_to(slot, kv):
        start = pl.multiple_of(kv * tk, tk)
        for i in range(2):
            st_ref[slot, i] = lax.dot_general(
                k_ref[i, pl.ds(start, tk), :], q_ref[i], (((1,), (1,)), ((), ())),
                preferred_element_type=F32)

    def softmax_pv(slot, kv, masked):
        vt = vt_ref[0, kv]
        for i in range(2):
            st = st_ref[slot, i]
            if masked:
                kpos = kv * tk + lax.broadcasted_iota(jnp.int32, st.shape, 0)
                qpos = qi * tq + lax.broadcasted_iota(jnp.int32, st.shape, 1)
                st = jnp.where(kpos <= qpos, st, -jnp.inf)
            m_prev = m_ref[i]
            m_new = jnp.maximum(m_prev, jnp.max(st, axis=0, keepdims=True))
            alpha = jnp.exp2(m_prev - m_new)
            p = jnp.exp2((st - m_new).astype(vt.dtype))
            acc_ref[i] = alpha * acc_ref[i] + jnp.dot(vt, p, preferred_element_type=F32)
            m_ref[i] = m_new

    scores_to(0, 0)

    def pair(j, carry):
        scores_to(1, 2 * j + 1)
        softmax_pv(0, 2 * j, masked=False)
        scores_to(0, 2 * j + 2)
        softmax_pv(1, 2 * j + 1, masked=False)
        return carry

    lax.fori_loop(0, qi // 2, pair, 0)

    @pl.when(qi % 2 == 0)
    def _():
        softmax_pv(0, qi, masked=True)

    @pl.when(qi % 2 == 1)
    def _():
        scores_to(1, qi)
        softmax_pv(0, qi - 1, masked=False)
        softmax_pv(1, qi, masked=True)

    lam = (jnp.exp(jnp.sum(lq1_ref[...] * lk1_ref[...], axis=-1, keepdims=True))
           - jnp.exp(jnp.sum(lq2_ref[...] * lk2_ref[...], axis=-1, keepdims=True))
           + lam_init)
    inv_l = [1.0 / acc_ref[i, width:width + 1, :] for i in range(2)]
    ot = acc_ref[0, :width, :] * inv_l[0] - lam * (acc_ref[1, :width, :] * inv_l[1])
    ms = jnp.mean(ot * ot, axis=0, keepdims=True)
    yt = (ot * lax.rsqrt(ms + EPS)) * g_ref[...]
    o_ref[...] = (yt * (1.0 - lam_init)).T.astype(o_ref.dtype)


def _attn_b(q, k, vt, lam_vecs, subln_g, lam_init):
    _, s, hd = q.shape
    _, n_kv, rows, tk = vt.shape
    width = rows - ONES_ROWS
    tq = tk
    vec = pl.BlockSpec((1, hd), lambda h, i: (0, 0))
    return pl.pallas_call(
        functools.partial(_attn_b_kernel, lam_init=lam_init),
        grid=(B_HEADS, s // tq),
        in_specs=[vec, vec, vec, vec,
                  pl.BlockSpec((width, 1), lambda h, i: (0, 0)),
                  pl.BlockSpec((2, tq, hd), lambda h, i: (h, i, 0)),
                  pl.BlockSpec((2, s, hd), lambda h, i: (h, 0, 0)),
                  pl.BlockSpec((1, n_kv, rows, tk), lambda h, i: (h, 0, 0, 0))],
        out_specs=pl.BlockSpec((tq, width), lambda h, i: (i, h)),
        out_shape=jax.ShapeDtypeStruct((s, B_HEADS * width), BF16),
        scratch_shapes=[pltpu.VMEM((2, 1, tq), F32),
                        pltpu.VMEM((2, rows, tq), F32),
                        pltpu.VMEM((2, 2, tk, tq), F32)],
        compiler_params=_params("parallel", "arbitrary"),
        name="attn_b",
    )(*[x.reshape(1, hd) for x in lam_vecs], subln_g.reshape(width, 1), q, k, vt)


def _merge_kernel(ya_ref, yb_ref, wa_ref, wb_ref, ga_ref, gb_ref, o_ref):
    pa = jnp.dot(ya_ref[...], wa_ref[...].astype(BF16), preferred_element_type=F32)
    pb = jnp.dot(yb_ref[...], wb_ref[...].astype(BF16), preferred_element_type=F32)
    o_ref[...] = (ga_ref[...].astype(F32) * pa + gb_ref[...].astype(F32) * pb).astype(o_ref.dtype)


def _merge(ya, yb, wa, wb, gates, tm=1024, tn=512):
    s = ya.shape[0]
    n = wa.shape[1]
    return pl.pallas_call(
        _merge_kernel,
        grid=(s // tm, n // tn),
        in_specs=[pl.BlockSpec((tm, ya.shape[1]), lambda i, j: (i, 0)),
                  pl.BlockSpec((tm, yb.shape[1]), lambda i, j: (i, 0)),
                  pl.BlockSpec((wa.shape[0], tn), lambda i, j: (0, j)),
                  pl.BlockSpec((wb.shape[0], tn), lambda i, j: (0, j)),
                  pl.BlockSpec((tm, tn), lambda i, j: (i, j)),
                  pl.BlockSpec((tm, tn), lambda i, j: (i, n // tn + j))],
        out_specs=pl.BlockSpec((tm, tn), lambda i, j: (i, j)),
        out_shape=jax.ShapeDtypeStruct((s, n), BF16),
        compiler_params=_params("parallel", "arbitrary"),
        name="merge",
    )(ya, yb, wa, wb, gates, gates)


def _matmul_residual_kernel(a_ref, b_ref, x_ref, o_ref):
    o_ref[...] = x_ref[...] + jnp.dot(a_ref[...], b_ref[...].astype(BF16), preferred_element_type=F32)


def _matmul_residual(a, b, x, tm=1024, tn=512):
    s, kdim = a.shape
    n = b.shape[1]
    return pl.pallas_call(
        _matmul_residual_kernel,
        grid=(s // tm, n // tn),
        in_specs=[pl.BlockSpec((tm, kdim), lambda i, j: (i, 0)),
                  pl.BlockSpec((kdim, tn), lambda i, j: (0, j)),
                  pl.BlockSpec((tm, tn), lambda i, j: (i, j))],
        out_specs=pl.BlockSpec((tm, tn), lambda i, j: (i, j)),
        out_shape=jax.ShapeDtypeStruct((s, n), F32),
        compiler_params=_params("parallel", "arbitrary"),
        name="out_proj",
    )(a, b, x)


def _ff1_kernel(a_ref, b_ref, o_ref):
    u = jnp.maximum(jnp.dot(a_ref[...], b_ref[...].astype(BF16), preferred_element_type=F32), 0.0)
    o_ref[...] = (u * u).astype(o_ref.dtype)


def _ff1(a, b, tm=1024, tn=512):
    s, kdim = a.shape
    n = b.shape[1]
    return pl.pallas_call(
        _ff1_kernel,
        grid=(s // tm, n // tn),
        in_specs=[pl.BlockSpec((tm, kdim), lambda i, j: (i, 0)),
                  pl.BlockSpec((kdim, tn), lambda i, j: (0, j))],
        out_specs=pl.BlockSpec((tm, tn), lambda i, j: (i, j)),
        out_shape=jax.ShapeDtypeStruct((s, n), BF16),
        compiler_params=_params("parallel", "arbitrary"),
        name="ff1",
    )(a, b)


def _ff2_kernel(a_ref, b_ref, x_ref, o_ref):
    k = pl.program_id(2)
    part = jnp.dot(a_ref[...], b_ref[...].astype(BF16), preferred_element_type=F32)

    @pl.when(k == 0)
    def _():
        o_ref[...] = x_ref[...] + part

    @pl.when(k > 0)
    def _():
        o_ref[...] += part


def _ff2(a, b, x, tm=1024, tn=1024, tk=2048):
    s, kdim = a.shape
    n = b.shape[1]
    return pl.pallas_call(
        _ff2_kernel,
        grid=(s // tm, n // tn, kdim // tk),
        in_specs=[pl.BlockSpec((tm, tk), lambda i, j, k: (i, k)),
                  pl.BlockSpec((tk, tn), lambda i, j, k: (k, j)),
                  pl.BlockSpec((tm, tn), lambda i, j, k: (i, j))],
        out_specs=pl.BlockSpec((tm, tn), lambda i, j, k: (i, j)),
        out_shape=jax.ShapeDtypeStruct((s, n), F32),
        compiler_params=_params("parallel", "parallel", "arbitrary"),
        name="ff2",
    )(a, b, x)


def _lambda_init(layer_idx):
    return 0.8 - 0.6 * math.exp(-0.3 * layer_idx)


def kernel(x, norm_mix, w_in, b_gate, w_proj_a, w_proj_b, w_out, lambda_q1, lambda_k1,
           lambda_q2, lambda_k2, subln_g, norm_mlp, w_ff1, w_ff2, norm_final):
    batch, seq, d = x.shape
    depth = w_in.shape[0]
    tables = _rope_tables(seq)
    outs = []
    for b in range(batch):
        xb = x.reshape(seq, d) if batch == 1 else x[b]
        for l in range(depth):
            w_in_l = w_in[l]
            h = _rmsnorm(xb, norm_mix[l], BF16)
            groups = []
            for g in range(N_GROUPS):
                groups.append(_attn_a(_proj_a(h, w_in_l, tables, g)))
            ya = _mix_a(groups, seq)
            qb = _proj_bqk(h, w_in_l, tables, COL_QB, QK_SCALE * LOG2_E)
            kb = _proj_bqk(h, w_in_l, tables, COL_KB, 1.0)
            vb = _proj_bv(h, w_in_l)
            lam_init = _lambda_init(l)
            yb = _attn_b(qb, kb, vb, (lambda_q1[l], lambda_k1[l], lambda_q2[l], lambda_k2[l]),
                         subln_g[l], lam_init)
            gates = _proj_gate(h, w_in_l, b_gate[l])
            merged = _merge(ya, yb, w_proj_a[l], w_proj_b[l], gates)
            xb = _matmul_residual(merged, w_out[l], xb)
            h2 = _rmsnorm(xb, norm_mlp[l], BF16)
            u = _ff1(h2, w_ff1[l])
            xb = _ff2(u, w_ff2[l], xb)
        outs.append(_rmsnorm(xb, norm_final, x.dtype))
    if batch == 1:
        return outs[0].reshape(1, seq, d)
    return jnp.stack(outs, axis=0)
```

```python
import functools
import math

import jax
import jax.numpy as jnp
from jax import lax
from jax.experimental import pallas as pl
from jax.experimental.pallas import tpu as pltpu

D_MODEL = 4096
HEAD_DIM = 128
DIL_PAIRS = ((128, 1), (512, 4), (2048, 16))
N_GROUPS = len(DIL_PAIRS)
HEADS_PER_GROUP = 4
GROUP_COLS = HEADS_PER_GROUP * HEAD_DIM
A_WIDTH = N_GROUPS * GROUP_COLS
A_OUT = GROUP_COLS
B_HEADS = 8
B_QK = B_HEADS * 2 * HEAD_DIM
B_V = B_HEADS * 2 * HEAD_DIM
D_FF = 4 * D_MODEL
ROPE_THETA = 500000.0
ROT_DIM = HEAD_DIM // 4
ROT_HALF = ROT_DIM // 2
EPS = 1e-6
QK_SCALE = 1.0 / math.sqrt(HEAD_DIM)
LOG2_E = math.log2(math.e)
WINDOW_STEPS = 128
ONES_ROWS = 16
MXU_COLS = 256

COL_QA = 0
COL_KA = A_WIDTH
COL_VA = 2 * A_WIDTH
COL_QB = 3 * A_WIDTH
COL_KB = COL_QB + B_QK
COL_VB = COL_KB + B_QK
COL_GATE = COL_VB + B_V
IN_COLS = COL_GATE + 2 * D_MODEL

VMEM_LIMIT_BYTES = 48 * 1024 * 1024

F32 = jnp.float32
BF16 = jnp.bfloat16


def _params(*semantics):
    return pltpu.CompilerParams(dimension_semantics=semantics,
                                vmem_limit_bytes=VMEM_LIMIT_BYTES)


def _rmsnorm_kernel(x_ref, g_ref, o_ref):
    x = x_ref[...]
    ms = jnp.mean(x * x, axis=-1, keepdims=True)
    o_ref[...] = ((x * lax.rsqrt(ms + EPS)) * g_ref[...]).astype(o_ref.dtype)


def _rmsnorm(x, g, out_dtype, rows=256):
    s, d = x.shape
    return pl.pallas_call(
        _rmsnorm_kernel,
        grid=(s // rows,),
        in_specs=[pl.BlockSpec((rows, d), lambda i: (i, 0)),
                  pl.BlockSpec((1, d), lambda i: (0, 0))],
        out_specs=pl.BlockSpec((rows, d), lambda i: (i, 0)),
        out_shape=jax.ShapeDtypeStruct((s, d), out_dtype),
        compiler_params=_params("parallel"),
        name="rmsnorm",
    )(x, g.reshape(1, d))


def _rope_tables(seq, scales):
    pos = jnp.arange(seq, dtype=F32)
    inv = ROPE_THETA ** (-jnp.arange(0, ROT_DIM, 2, dtype=F32) / ROT_DIM)
    ang = pos[:, None] * inv[None, :]
    cos, sin = jnp.cos(ang), jnp.sin(ang)
    ones = jnp.ones((seq, HEAD_DIM - ROT_DIM), F32)
    zeros = jnp.zeros((seq, HEAD_DIM - ROT_DIM), F32)
    zh = jnp.zeros((seq, ROT_HALF), F32)
    cos_t = jnp.concatenate([cos, cos, ones], axis=1)
    sin_lo = jnp.concatenate([-sin, zh, zeros], axis=1)
    sin_hi = jnp.concatenate([zh, sin, zeros], axis=1)
    rot = jnp.stack([cos_t, sin_lo, sin_hi])
    ident = jnp.stack([jnp.ones_like(cos_t), jnp.zeros_like(cos_t), jnp.zeros_like(cos_t)])
    return jnp.stack([ident if sc is None else rot * sc for sc in scales])


def _rope_head(y, tab_ref):
    x2 = pltpu.roll(y, HEAD_DIM - ROT_HALF, 1)
    x1 = pltpu.roll(y, ROT_HALF, 1)
    return y * tab_ref[0, 0] + x2 * tab_ref[0, 1] + x1 * tab_ref[0, 2]


def _proj_a_kernel(h_ref, w_ref, tab_ref, o_ref, acc_ref, *, dilation):
    tm = h_ref.shape[0]
    n = tm // dilation
    heads_per_chunk = MXU_COLS // HEAD_DIM
    h = h_ref[...]
    for c in range(GROUP_COLS // MXU_COLS):
        acc = jnp.dot(h, w_ref[:, c * MXU_COLS:(c + 1) * MXU_COLS].astype(BF16),
                      preferred_element_type=F32)
        for hc in range(heads_per_chunk):
            hh = c * heads_per_chunk + hc
            y = _rope_head(acc[:, hc * HEAD_DIM:(hc + 1) * HEAD_DIM], tab_ref)
            if dilation == 1:
                o_ref[0, hh, 0] = y.astype(o_ref.dtype)
                continue
            acc_ref[hh] = y
            for rho in range(dilation):
                o_ref[0, hh, rho] = acc_ref[hh, pl.ds(rho, n, stride=dilation), :].astype(o_ref.dtype)


def _proj_a(h, w_in, tables, group, tm=1024):
    s, d = h.shape
    dilation = DIL_PAIRS[group][1]
    sub_len = s // dilation
    blocks_per_section = A_WIDTH // GROUP_COLS
    return pl.pallas_call(
        functools.partial(_proj_a_kernel, dilation=dilation),
        grid=(s // tm, 3),
        in_specs=[pl.BlockSpec((tm, d), lambda i, j: (i, 0)),
                  pl.BlockSpec((d, GROUP_COLS), lambda i, j: (0, j * blocks_per_section + group)),
                  pl.BlockSpec((1, 3, tm, HEAD_DIM), lambda i, j: (j, 0, i, 0))],
        out_specs=pl.BlockSpec((1, HEADS_PER_GROUP, dilation, tm // dilation, HEAD_DIM),
                               lambda i, j: (j, 0, 0, i, 0)),
        out_shape=jax.ShapeDtypeStruct((3, HEADS_PER_GROUP, dilation, sub_len, HEAD_DIM), BF16),
        scratch_shapes=[pltpu.VMEM((HEADS_PER_GROUP, tm, HEAD_DIM), F32)],
        compiler_params=_params("parallel", "arbitrary"),
        name=f"proj_a_g{group}",
    )(h, w_in, tables)


def _proj_bqk_kernel(h_ref, w_ref, tab_ref, o_ref):
    heads_per_chunk = MXU_COLS // HEAD_DIM
    h = h_ref[...]
    for c in range(o_ref.shape[0] // heads_per_chunk):
        acc = jnp.dot(h, w_ref[:, c * MXU_COLS:(c + 1) * MXU_COLS].astype(BF16),
                      preferred_element_type=F32)
        for hc in range(heads_per_chunk):
            y = _rope_head(acc[:, hc * HEAD_DIM:(hc + 1) * HEAD_DIM], tab_ref)
            o_ref[c * heads_per_chunk + hc] = y.astype(o_ref.dtype)


def _proj_bqk(h, w_in, tables, col0, tm=1024, tn=512):
    s, d = h.shape
    heads_per_tile = tn // HEAD_DIM
    return pl.pallas_call(
        _proj_bqk_kernel,
        grid=(s // tm, B_QK // tn),
        in_specs=[pl.BlockSpec((tm, d), lambda i, j: (i, 0)),
                  pl.BlockSpec((d, tn), lambda i, j: (0, col0 // tn + j)),
                  pl.BlockSpec((1, 3, tm, HEAD_DIM), lambda i, j: (0, 0, i, 0))],
        out_specs=pl.BlockSpec((heads_per_tile, tm, HEAD_DIM), lambda i, j: (j, i, 0)),
        out_shape=jax.ShapeDtypeStruct((B_QK // HEAD_DIM, s, HEAD_DIM), BF16),
        compiler_params=_params("parallel", "arbitrary"),
        name="proj_b_qk",
    )(h, w_in, tables)


def _proj_bv_kernel(h_ref, w_ref, o_ref):
    acc = jnp.dot(h_ref[...], w_ref[...].astype(BF16), preferred_element_type=F32)
    heads, blocks, rows, tk = o_ref.shape
    width = rows - ONES_ROWS
    for hv in range(heads):
        vt = acc[:, hv * width:(hv + 1) * width].T
        for c in range(blocks):
            o_ref[hv, c, :width, :] = vt[:, c * tk:(c + 1) * tk].astype(o_ref.dtype)
            o_ref[hv, c, width:, :] = jnp.ones((ONES_ROWS, tk), o_ref.dtype)


def _proj_bv(h, w_in, tm=1024, tn=512, tk=512):
    s, d = h.shape
    width = 2 * HEAD_DIM
    rows = width + ONES_ROWS
    return pl.pallas_call(
        _proj_bv_kernel,
        grid=(s // tm, B_V // tn),
        in_specs=[pl.BlockSpec((tm, d), lambda i, j: (i, 0)),
                  pl.BlockSpec((d, tn), lambda i, j: (0, COL_VB // tn + j))],
        out_specs=pl.BlockSpec((tn // width, tm // tk, rows, tk), lambda i, j: (j, i, 0, 0)),
        out_shape=jax.ShapeDtypeStruct((B_HEADS, s // tk, rows, tk), BF16),
        compiler_params=_params("parallel", "arbitrary"),
        name="proj_b_v",
    )(h, w_in)


def _proj_gate_kernel(h_ref, w_ref, b_ref, o_ref):
    h = h_ref[...]
    for c in range(o_ref.shape[1] // MXU_COLS):
        cols = slice(c * MXU_COLS, (c + 1) * MXU_COLS)
        z = jnp.dot(h, w_ref[:, cols].astype(BF16), preferred_element_type=F32) + b_ref[:, cols]
        o_ref[:, cols] = (0.5 * jnp.tanh(0.5 * z) + 0.5).astype(o_ref.dtype)


def _proj_gate(h, w_in, b_gate, tm=1024, tn=512):
    s, d = h.shape
    n = 2 * D_MODEL
    return pl.pallas_call(
        _proj_gate_kernel,
        grid=(s // tm, n // tn),
        in_specs=[pl.BlockSpec((tm, d), lambda i, j: (i, 0)),
                  pl.BlockSpec((d, tn), lambda i, j: (0, COL_GATE // tn + j)),
                  pl.BlockSpec((1, tn), lambda i, j: (0, j))],
        out_specs=pl.BlockSpec((tm, tn), lambda i, j: (i, j)),
        out_shape=jax.ShapeDtypeStruct((s, n), BF16),
        compiler_params=_params("parallel", "arbitrary"),
        name="proj_gate",
    )(h, w_in, b_gate.reshape(1, n))


def _attn_a_kernel(q_ref, kc_ref, kp_ref, vc_ref, vp_ref, o_ref, lse_ref, kk_ref, vv_ref,
                   *, chunks_per_seq):
    w = WINDOW_STEPS
    t = q_ref.shape[1]
    c = pl.program_id(0)
    kk_ref[0:w] = kp_ref[0]
    kk_ref[w:] = kc_ref[0]
    vv_ref[0:w] = vp_ref[0]
    vv_ref[w:] = vc_ref[0]
    has_prev = (c % chunks_per_seq) != 0
    qp = lax.broadcasted_iota(jnp.int32, (w, 2 * w), 0)
    kp = lax.broadcasted_iota(jnp.int32, (w, 2 * w), 1)
    band = (kp >= qp) & (kp <= qp + w)
    band_first = band & (kp >= jnp.where(has_prev, 0, w))
    for b in range(t // w):
        q = q_ref[0, b * w:(b + 1) * w]
        k = kk_ref[b * w:(b + 2) * w]
        v = vv_ref[b * w:(b + 2) * w]
        s = lax.dot_general(q, k, (((1,), (1,)), ((), ())), preferred_element_type=F32)
        s = jnp.where(band_first if b == 0 else band, s, -jnp.inf)
        m = jnp.max(s, axis=-1, keepdims=True)
        p = jnp.exp(s - m)
        l = jnp.sum(p, axis=-1, keepdims=True)
        o = jnp.dot(p.astype(v.dtype), v, preferred_element_type=F32)
        o_ref[b * w:(b + 1) * w] = o / l
        lse_ref[b * w:(b + 1) * w] = jnp.broadcast_to(m + jnp.log(l), (w, HEAD_DIM))


def _attn_a(qkv, t=2048):
    _, heads, dilation, sub_len, hd = qkv.shape
    rows = heads * dilation * sub_len
    flat = qkv.reshape(3, rows, hd)
    w = WINDOW_STEPS
    t = min(t, sub_len)
    bpc = t // w
    cur = lambda sec: pl.BlockSpec((1, t, hd), lambda c: (sec, c, 0))
    prev = lambda sec: pl.BlockSpec((1, w, hd), lambda c: (sec, jnp.maximum(c * bpc - 1, 0), 0))
    out_spec = pl.BlockSpec((t, hd), lambda c: (c, 0))
    out, lse = pl.pallas_call(
        functools.partial(_attn_a_kernel, chunks_per_seq=sub_len // t),
        grid=(rows // t,),
        in_specs=[cur(0), cur(1), prev(1), cur(2), prev(2)],
        out_specs=[out_spec, out_spec],
        out_shape=[jax.ShapeDtypeStruct((rows, hd), F32)] * 2,
        scratch_shapes=[pltpu.VMEM((t + w, hd), BF16)] * 2,
        compiler_params=_params("parallel"),
        name=f"attn_a_r{dilation}",
    )(flat, flat, flat, flat, flat)
    shape = (heads, dilation, sub_len, hd)
    return out.reshape(shape), lse.reshape(shape)


def _mix_a_kernel(o0_ref, l0_ref, o1_ref, l1_ref, o2_ref, l2_ref, y_ref, nat_ref):
    tm = y_ref.shape[0]
    refs = ((o0_ref, l0_ref), (o1_ref, l1_ref), (o2_ref, l2_ref))
    r_max = DIL_PAIRS[-1][1]
    n = tm // r_max
    for hh in range(HEADS_PER_GROUP):
        for rho in range(r_max):
            outs, lses = [], []
            for (o_ref, l_ref), (_, r) in zip(refs, DIL_PAIRS):
                step = r_max // r
                rows = pl.ds(rho // r, n, stride=step) if step > 1 else pl.ds(0, n)
                outs.append(o_ref[hh, rho % r, rows, :])
                lses.append(l_ref[hh, rho % r, rows, :])
            m = jnp.maximum(jnp.maximum(lses[0], lses[1]), lses[2])
            ws = [jnp.exp(l - m) for l in lses]
            num = ws[0] * outs[0] + ws[1] * outs[1] + ws[2] * outs[2]
            den = ws[0] + ws[1] + ws[2]
            nat_ref[hh, pl.ds(rho, n, stride=r_max), :] = num / den
        y_ref[:, hh * HEAD_DIM:(hh + 1) * HEAD_DIM] = nat_ref[hh].astype(y_ref.dtype)


def _mix_a(group_outs, seq, tm=512):
    in_specs, args = [], []
    for (o, l), (_, r) in zip(group_outs, DIL_PAIRS):
        spec = pl.BlockSpec((HEADS_PER_GROUP, r, tm // r, HEAD_DIM), lambda i: (0, 0, i, 0))
        in_specs += [spec, spec]
        args += [o, l]
    return pl.pallas_call(
        _mix_a_kernel,
        grid=(seq // tm,),
        in_specs=in_specs,
        out_specs=pl.BlockSpec((tm, A_OUT), lambda i: (i, 0)),
        out_shape=jax.ShapeDtypeStruct((seq, A_OUT), BF16),
        scratch_shapes=[pltpu.VMEM((HEADS_PER_GROUP, tm, HEAD_DIM), F32)],
        compiler_params=_params("parallel"),
        name="mix_a",
    )(*args)


def _attn_b_kernel(lq1_ref, lk1_ref, lq2_ref, lk2_ref, g_ref, q_ref, k_ref, vt_ref, o_ref,
                   m_ref, acc_ref, st_ref, mb_ref, *, lam_init):
    tq = q_ref.shape[1]
    tk = vt_ref.shape[3]
    width = vt_ref.shape[2] - ONES_ROWS
    qi = pl.program_id(1)
    m_ref[...] = jnp.full(m_ref.shape, -jnp.inf, F32)
    acc_ref[...] = jnp.zeros(acc_ref.shape, F32)

    def scores_to(slot, kv):
        start = pl.multiple_of(kv * tk, tk)
        for i in range(2):
            st = lax.dot_general(
                k_ref[i, pl.ds(start, tk), :], q_ref[i], (((1,), (1,)), ((), ())),
                preferred_element_type=F32)
            st_ref[slot, i] = st
            mb_ref[slot, i] = jnp.max(st, axis=0, keepdims=True)

    def softmax_pv(slot, kv, masked):
        vt = vt_ref[0, kv]
        for i in range(2):
            st = st_ref[slot, i]
            if masked:
                kpos = kv * tk + lax.broadcasted_iota(jnp.int32, st.shape, 0)
                qpos = qi * tq + lax.broadcasted_iota(jnp.int32, st.shape, 1)
                st = jnp.where(kpos <= qpos, st, -jnp.inf)
                m_blk = jnp.max(st, axis=0, keepdims=True)
            else:
                m_blk = mb_ref[slot, i]
            m_prev = m_ref[i]
            m_new = jnp.maximum(m_prev, m_blk)
            alpha = jnp.exp2(m_prev - m_new)
            p = jnp.exp2((st - m_new).astype(vt.dtype))
            acc_ref[i] = alpha * acc_ref[i] + jnp.dot(vt, p, preferred_element_type=F32)
            m_ref[i] = m_new

    scores_to(0, 0)

    def pair(j, carry):
        scores_to(1, 2 * j + 1)
        softmax_pv(0, 2 * j, masked=False)
        scores_to(0, 2 * j + 2)
        softmax_pv(1, 2 * j + 1, masked=False)
        return carry

    lax.fori_loop(0, qi // 2, pair, 0)

    @pl.when(qi % 2 == 0)
    def _():
        softmax_pv(0, qi, masked=True)

    @pl.when(qi % 2 == 1)
    def _():
        scores_to(1, qi)
        softmax_pv(0, qi - 1, masked=False)
        softmax_pv(1, qi, masked=True)

    lam = (jnp.exp(jnp.sum(lq1_ref[...] * lk1_ref[...], axis=-1, keepdims=True))
           - jnp.exp(jnp.sum(lq2_ref[...] * lk2_ref[...], axis=-1, keepdims=True))
           + lam_init)
    inv_l = [1.0 / acc_ref[i, width:width + 1, :] for i in range(2)]
    ot = acc_ref[0, :width, :] * inv_l[0] - lam * (acc_ref[1, :width, :] * inv_l[1])
    ms = jnp.mean(ot * ot, axis=0, keepdims=True)
    yt = (ot * lax.rsqrt(ms + EPS)) * g_ref[...]
    o_ref[...] = (yt * (1.0 - lam_init)).T.astype(o_ref.dtype)


def _attn_b(q, k, vt, lam_vecs, subln_g, lam_init):
    _, s, hd = q.shape
    _, n_kv, rows, tk = vt.shape
    width = rows - ONES_ROWS
    tq = tk
    vec = pl.BlockSpec((1, hd), lambda h, i: (0, 0))
    return pl.pallas_call(
        functools.partial(_attn_b_kernel, lam_init=lam_init),
        grid=(B_HEADS, s // tq),
        in_specs=[vec, vec, vec, vec,
                  pl.BlockSpec((width, 1), lambda h, i: (0, 0)),
                  pl.BlockSpec((2, tq, hd), lambda h, i: (h, i, 0)),
                  pl.BlockSpec((2, s, hd), lambda h, i: (h, 0, 0)),
                  pl.BlockSpec((1, n_kv, rows, tk), lambda h, i: (h, 0, 0, 0))],
        out_specs=pl.BlockSpec((tq, width), lambda h, i: (i, h)),
        out_shape=jax.ShapeDtypeStruct((s, B_HEADS * width), BF16),
        scratch_shapes=[pltpu.VMEM((2, 1, tq), F32),
                        pltpu.VMEM((2, rows, tq), F32),
                        pltpu.VMEM((2, 2, tk, tq), F32),
                        pltpu.VMEM((2, 2, 1, tq), F32)],
        compiler_params=_params("parallel", "arbitrary"),
        name="attn_b",
    )(*[x.reshape(1, hd) for x in lam_vecs], subln_g.reshape(width, 1), q, k, vt)


def _merge_kernel(ya_ref, yb_ref, wa_ref, wb_ref, ga_ref, gb_ref, o_ref):
    pa = jnp.dot(ya_ref[...], wa_ref[...].astype(BF16), preferred_element_type=F32)
    pb = jnp.dot(yb_ref[...], wb_ref[...].astype(BF16), preferred_element_type=F32)
    o_ref[...] = (ga_ref[...].astype(F32) * pa + gb_ref[...].astype(F32) * pb).astype(o_ref.dtype)


def _merge(ya, yb, wa, wb, gates, tm=1024, tn=512):
    s = ya.shape[0]
    n = wa.shape[1]
    return pl.pallas_call(
        _merge_kernel,
        grid=(s // tm, n // tn),
        in_specs=[pl.BlockSpec((tm, ya.shape[1]), lambda i, j: (i, 0)),
                  pl.BlockSpec((tm, yb.shape[1]), lambda i, j: (i, 0)),
                  pl.BlockSpec((wa.shape[0], tn), lambda i, j: (0, j)),
                  pl.BlockSpec((wb.shape[0], tn), lambda i, j: (0, j)),
                  pl.BlockSpec((tm, tn), lambda i, j: (i, j)),
                  pl.BlockSpec((tm, tn), lambda i, j: (i, n // tn + j))],
        out_specs=pl.BlockSpec((tm, tn), lambda i, j: (i, j)),
        out_shape=jax.ShapeDtypeStruct((s, n), BF16),
        compiler_params=_params("parallel", "arbitrary"),
        name="merge",
    )(ya, yb, wa, wb, gates, gates)


def _matmul_residual_kernel(a_ref, b_ref, x_ref, o_ref):
    o_ref[...] = x_ref[...] + jnp.dot(a_ref[...], b_ref[...].astype(BF16), preferred_element_type=F32)


def _matmul_residual(a, b, x, tm=1024, tn=512):
    s, kdim = a.shape
    n = b.shape[1]
    return pl.pallas_call(
        _matmul_residual_kernel,
        grid=(s // tm, n // tn),
        in_specs=[pl.BlockSpec((tm, kdim), lambda i, j: (i, 0)),
                  pl.BlockSpec((kdim, tn), lambda i, j: (0, j)),
                  pl.BlockSpec((tm, tn), lambda i, j: (i, j))],
        out_specs=pl.BlockSpec((tm, tn), lambda i, j: (i, j)),
        out_shape=jax.ShapeDtypeStruct((s, n), F32),
        compiler_params=_params("parallel", "arbitrary"),
        name="out_proj",
    )(a, b, x)


def _ff1_kernel(a_ref, b_ref, o_ref):
    u = jnp.maximum(jnp.dot(a_ref[...], b_ref[...].astype(BF16), preferred_element_type=F32), 0.0)
    o_ref[...] = (u * u).astype(o_ref.dtype)


def _ff1(a, b, tm=1024, tn=512):
    s, kdim = a.shape
    n = b.shape[1]
    return pl.pallas_call(
        _ff1_kernel,
        grid=(s // tm, n // tn),
        in_specs=[pl.BlockSpec((tm, kdim), lambda i, j: (i, 0)),
                  pl.BlockSpec((kdim, tn), lambda i, j: (0, j))],
        out_specs=pl.BlockSpec((tm, tn), lambda i, j: (i, j)),
        out_shape=jax.ShapeDtypeStruct((s, n), BF16),
        compiler_params=_params("parallel", "arbitrary"),
        name="ff1",
    )(a, b)


def _ff2_kernel(a_ref, b_ref, x_ref, o_ref):
    @pl.when(pl.program_id(2) == 0)
    def _():
        o_ref[...] = x_ref[...]

    o_ref[...] += jnp.dot(a_ref[...], b_ref[...].astype(BF16), preferred_element_type=F32)


def _ff2(a, b, x, tm=1024, tn=1024, tk=2048):
    s, kdim = a.shape
    n = b.shape[1]
    return pl.pallas_call(
        _ff2_kernel,
        grid=(s // tm, n // tn, kdim // tk),
        in_specs=[pl.BlockSpec((tm, tk), lambda i, j, k: (i, k)),
                  pl.BlockSpec((tk, tn), lambda i, j, k: (k, j)),
                  pl.BlockSpec((tm, tn), lambda i, j, k: (i, j))],
        out_specs=pl.BlockSpec((tm, tn), lambda i, j, k: (i, j)),
        out_shape=jax.ShapeDtypeStruct((s, n), F32),
        compiler_params=_params("parallel", "parallel", "arbitrary"),
        name="ff2",
    )(a, b, x)


def _lambda_init(layer_idx):
    return 0.8 - 0.6 * math.exp(-0.3 * layer_idx)


def kernel(x, norm_mix, w_in, b_gate, w_proj_a, w_proj_b, w_out, lambda_q1, lambda_k1,
           lambda_q2, lambda_k2, subln_g, norm_mlp, w_ff1, w_ff2, norm_final):
    batch, seq, d = x.shape
    depth = w_in.shape[0]
    tables_a = _rope_tables(seq, (QK_SCALE, 1.0, None))
    tables_qb = _rope_tables(seq, (QK_SCALE * LOG2_E,))
    tables_kb = _rope_tables(seq, (1.0,))
    outs = []
    for b in range(batch):
        xb = x.reshape(seq, d) if batch == 1 else x[b]
        for l in range(depth):
            w_in_l = w_in[l]
            h = _rmsnorm(xb, norm_mix[l], BF16)
            groups = []
            for g in range(N_GROUPS):
                groups.append(_attn_a(_proj_a(h, w_in_l, tables_a, g)))
            ya = _mix_a(groups, seq)
            qb = _proj_bqk(h, w_in_l, tables_qb, COL_QB)
            kb = _proj_bqk(h, w_in_l, tables_kb, COL_KB)
            vb = _proj_bv(h, w_in_l)
            lam_init = _lambda_init(l)
            yb = _attn_b(qb, kb, vb, (lambda_q1[l], lambda_k1[l], lambda_q2[l], lambda_k2[l]),
                         subln_g[l], lam_init)
            gates = _proj_gate(h, w_in_l, b_gate[l])
            merged = _merge(ya, yb, w_proj_a[l], w_proj_b[l], gates)
            xb = _matmul_residual(merged, w_out[l], xb)
            h2 = _rmsnorm(xb, norm_mlp[l], BF16)
            u = _ff1(h2, w_ff1[l])
            xb = _ff2(u, w_ff2[l], xb)
        outs.append(_rmsnorm(xb, norm_final, x.dtype))
    if batch == 1:
        return outs[0].reshape(1, seq, d)
    return jnp.stack(outs, axis=0)
```

```python
import functools
import math

import jax
import jax.numpy as jnp
from jax import lax
from jax.experimental import pallas as pl
from jax.experimental.pallas import tpu as pltpu

D_MODEL = 4096
HEAD_DIM = 128
DIL_PAIRS = ((128, 1), (512, 4), (2048, 16))
N_GROUPS = len(DIL_PAIRS)
HEADS_PER_GROUP = 4
GROUP_COLS = HEADS_PER_GROUP * HEAD_DIM
A_WIDTH = N_GROUPS * GROUP_COLS
A_OUT = GROUP_COLS
B_HEADS = 8
B_QK = B_HEADS * 2 * HEAD_DIM
B_V = B_HEADS * 2 * HEAD_DIM
D_FF = 4 * D_MODEL
ROPE_THETA = 500000.0
ROT_DIM = HEAD_DIM // 4
ROT_HALF = ROT_DIM // 2
EPS = 1e-6
QK_SCALE = 1.0 / math.sqrt(HEAD_DIM)
LOG2_E = math.log2(math.e)
WINDOW_STEPS = 128
ONES_ROWS = 16
MXU_COLS = 256

COL_QA = 0
COL_KA = A_WIDTH
COL_VA = 2 * A_WIDTH
COL_QB = 3 * A_WIDTH
COL_KB = COL_QB + B_QK
COL_VB = COL_KB + B_QK
COL_GATE = COL_VB + B_V
IN_COLS = COL_GATE + 2 * D_MODEL

VMEM_LIMIT_BYTES = 48 * 1024 * 1024

F32 = jnp.float32
BF16 = jnp.bfloat16


def _params(*semantics):
    return pltpu.CompilerParams(dimension_semantics=semantics,
                                vmem_limit_bytes=VMEM_LIMIT_BYTES)


def _rmsnorm_kernel(x_ref, g_ref, o_ref):
    x = x_ref[...]
    ms = jnp.mean(x * x, axis=-1, keepdims=True)
    o_ref[...] = ((x * lax.rsqrt(ms + EPS)) * g_ref[...]).astype(o_ref.dtype)


def _rmsnorm(x, g, out_dtype, rows=256):
    s, d = x.shape
    return pl.pallas_call(
        _rmsnorm_kernel,
        grid=(s // rows,),
        in_specs=[pl.BlockSpec((rows, d), lambda i: (i, 0)),
                  pl.BlockSpec((1, d), lambda i: (0, 0))],
        out_specs=pl.BlockSpec((rows, d), lambda i: (i, 0)),
        out_shape=jax.ShapeDtypeStruct((s, d), out_dtype),
        compiler_params=_params("parallel"),
        name="rmsnorm",
    )(x, g.reshape(1, d))


def _rope_tables(seq):
    pos = jnp.arange(seq, dtype=F32)
    inv = ROPE_THETA ** (-jnp.arange(0, ROT_DIM, 2, dtype=F32) / ROT_DIM)
    ang = pos[:, None] * inv[None, :]
    cos, sin = jnp.cos(ang), jnp.sin(ang)
    ones = jnp.ones((seq, HEAD_DIM - ROT_DIM), F32)
    zeros = jnp.zeros((seq, HEAD_DIM - ROT_DIM), F32)
    zh = jnp.zeros((seq, ROT_HALF), F32)
    cos_t = jnp.concatenate([cos, cos, ones], axis=1)
    sin_lo = jnp.concatenate([-sin, zh, zeros], axis=1)
    sin_hi = jnp.concatenate([zh, sin, zeros], axis=1)
    rot = jnp.stack([cos_t, sin_lo, sin_hi])
    ident = jnp.stack([jnp.ones_like(cos_t), jnp.zeros_like(cos_t), jnp.zeros_like(cos_t)])
    return jnp.stack([rot, ident])


def _rope_head(y, tab_ref, scale):
    x2 = pltpu.roll(y, HEAD_DIM - ROT_HALF, 1)
    x1 = pltpu.roll(y, ROT_HALF, 1)
    return (y * tab_ref[0, 0] + x2 * tab_ref[0, 1] + x1 * tab_ref[0, 2]) * scale


def _proj_a_kernel(h_ref, w_ref, tab_ref, o_ref, acc_ref, *, dilation):
    tm = h_ref.shape[0]
    n = tm // dilation
    heads_per_chunk = MXU_COLS // HEAD_DIM
    scale = jnp.where(pl.program_id(1) == 0, QK_SCALE, 1.0).astype(F32)
    h = h_ref[...]
    for c in range(GROUP_COLS // MXU_COLS):
        acc = jnp.dot(h, w_ref[:, c * MXU_COLS:(c + 1) * MXU_COLS].astype(BF16),
                      preferred_element_type=F32)
        for hc in range(heads_per_chunk):
            hh = c * heads_per_chunk + hc
            y = _rope_head(acc[:, hc * HEAD_DIM:(hc + 1) * HEAD_DIM], tab_ref, scale)
            if dilation == 1:
                o_ref[0, hh, 0] = y.astype(o_ref.dtype)
                continue
            acc_ref[hh] = y
            for rho in range(dilation):
                o_ref[0, hh, rho] = acc_ref[hh, pl.ds(rho, n, stride=dilation), :].astype(o_ref.dtype)


def _proj_a(h, w_in, tables, group, tm=1024):
    s, d = h.shape
    dilation = DIL_PAIRS[group][1]
    sub_len = s // dilation
    blocks_per_section = A_WIDTH // GROUP_COLS
    return pl.pallas_call(
        functools.partial(_proj_a_kernel, dilation=dilation),
        grid=(s // tm, 3),
        in_specs=[pl.BlockSpec((tm, d), lambda i, j: (i, 0)),
                  pl.BlockSpec((d, GROUP_COLS), lambda i, j: (0, j * blocks_per_section + group)),
                  pl.BlockSpec((1, 3, tm, HEAD_DIM), lambda i, j: (j // 2, 0, i, 0))],
        out_specs=pl.BlockSpec((1, HEADS_PER_GROUP, dilation, tm // dilation, HEAD_DIM),
                               lambda i, j: (j, 0, 0, i, 0)),
        out_shape=jax.ShapeDtypeStruct((3, HEADS_PER_GROUP, dilation, sub_len, HEAD_DIM), BF16),
        scratch_shapes=[pltpu.VMEM((HEADS_PER_GROUP, tm, HEAD_DIM), F32)],
        compiler_params=_params("parallel", "arbitrary"),
        name=f"proj_a_g{group}",
    )(h, w_in, tables)


def _proj_bqk_kernel(h_ref, w_ref, tab_ref, o_ref, *, q_tiles):
    heads_per_chunk = MXU_COLS // HEAD_DIM
    scale = jnp.where(pl.program_id(1) < q_tiles, QK_SCALE * LOG2_E, 1.0).astype(F32)
    h = h_ref[...]
    for c in range(o_ref.shape[0] // heads_per_chunk):
        acc = jnp.dot(h, w_ref[:, c * MXU_COLS:(c + 1) * MXU_COLS].astype(BF16),
                      preferred_element_type=F32)
        for hc in range(heads_per_chunk):
            y = _rope_head(acc[:, hc * HEAD_DIM:(hc + 1) * HEAD_DIM], tab_ref, scale)
            o_ref[c * heads_per_chunk + hc] = y.astype(o_ref.dtype)


def _proj_bqk(h, w_in, tables, tm=1024, tn=512):
    s, d = h.shape
    heads_per_tile = tn // HEAD_DIM
    n = B_QK + B_QK
    return pl.pallas_call(
        functools.partial(_proj_bqk_kernel, q_tiles=B_QK // tn),
        grid=(s // tm, n // tn),
        in_specs=[pl.BlockSpec((tm, d), lambda i, j: (i, 0)),
                  pl.BlockSpec((d, tn), lambda i, j: (0, COL_QB // tn + j)),
                  pl.BlockSpec((1, 3, tm, HEAD_DIM), lambda i, j: (0, 0, i, 0))],
        out_specs=pl.BlockSpec((heads_per_tile, tm, HEAD_DIM), lambda i, j: (j, i, 0)),
        out_shape=jax.ShapeDtypeStruct((n // HEAD_DIM, s, HEAD_DIM), BF16),
        compiler_params=_params("parallel", "arbitrary"),
        name="proj_b_qk",
    )(h, w_in, tables)


def _proj_bv_kernel(h_ref, w_ref, o_ref):
    acc = jnp.dot(h_ref[...], w_ref[...].astype(BF16), preferred_element_type=F32)
    heads, blocks, rows, tk = o_ref.shape
    width = rows - ONES_ROWS
    for hv in range(heads):
        vt = acc[:, hv * width:(hv + 1) * width].T
        for c in range(blocks):
            o_ref[hv, c, :width, :] = vt[:, c * tk:(c + 1) * tk].astype(o_ref.dtype)
            o_ref[hv, c, width:, :] = jnp.ones((ONES_ROWS, tk), o_ref.dtype)


def _proj_bv(h, w_in, tm=1024, tn=512, tk=512):
    s, d = h.shape
    width = 2 * HEAD_DIM
    rows = width + ONES_ROWS
    return pl.pallas_call(
        _proj_bv_kernel,
        grid=(s // tm, B_V // tn),
        in_specs=[pl.BlockSpec((tm, d), lambda i, j: (i, 0)),
                  pl.BlockSpec((d, tn), lambda i, j: (0, COL_VB // tn + j))],
        out_specs=pl.BlockSpec((tn // width, tm // tk, rows, tk), lambda i, j: (j, i, 0, 0)),
        out_shape=jax.ShapeDtypeStruct((B_HEADS, s // tk, rows, tk), BF16),
        compiler_params=_params("parallel", "arbitrary"),
        name="proj_b_v",
    )(h, w_in)


def _proj_gate_kernel(h_ref, w_ref, b_ref, o_ref):
    h = h_ref[...]
    for c in range(o_ref.shape[1] // MXU_COLS):
        cols = slice(c * MXU_COLS, (c + 1) * MXU_COLS)
        z = jnp.dot(h, w_ref[:, cols].astype(BF16), preferred_element_type=F32) + b_ref[:, cols]
        o_ref[:, cols] = (0.5 * jnp.tanh(0.5 * z) + 0.5).astype(o_ref.dtype)


def _proj_gate(h, w_in, b_gate, tm=1024, tn=512):
    s, d = h.shape
    n = 2 * D_MODEL
    return pl.pallas_call(
        _proj_gate_kernel,
        grid=(s // tm, n // tn),
        in_specs=[pl.BlockSpec((tm, d), lambda i, j: (i, 0)),
                  pl.BlockSpec((d, tn), lambda i, j: (0, COL_GATE // tn + j)),
                  pl.BlockSpec((1, tn), lambda i, j: (0, j))],
        out_specs=pl.BlockSpec((tm, tn), lambda i, j: (i, j)),
        out_shape=jax.ShapeDtypeStruct((s, n), BF16),
        compiler_params=_params("parallel", "arbitrary"),
        name="proj_gate",
    )(h, w_in, b_gate.reshape(1, n))


def _attn_a_kernel(q_ref, kc_ref, kp_ref, vc_ref, vp_ref, o_ref, lse_ref, kk_ref, vv_ref,
                   *, sub_len):
    w = WINDOW_STEPS
    t = q_ref.shape[1]
    c = pl.program_id(0)
    kk_ref[0:w] = kp_ref[0]
    kk_ref[w:] = kc_ref[0]
    vv_ref[0:w] = vp_ref[0]
    vv_ref[w:] = vc_ref[0]
    qp = lax.broadcasted_iota(jnp.int32, (w, 2 * w), 0)
    kp = lax.broadcasted_iota(jnp.int32, (w, 2 * w), 1)
    band = (kp >= qp) & (kp <= qp + w)
    band_start = band & (kp >= w)
    if t >= sub_len:
        masks = [band_start if (b * w) % sub_len == 0 else band for b in range(t // w)]
    else:
        is_start = (c % (sub_len // t)) == 0
        masks = [band & (kp >= jnp.where(is_start, w, 0))] + [band] * (t // w - 1)
    for b in range(t // w):
        q = q_ref[0, b * w:(b + 1) * w]
        k = kk_ref[b * w:(b + 2) * w]
        v = vv_ref[b * w:(b + 2) * w]
        s = lax.dot_general(q, k, (((1,), (1,)), ((), ())), preferred_element_type=F32)
        s = jnp.where(masks[b], s, -jnp.inf)
        m = jnp.max(s, axis=-1, keepdims=True)
        p = jnp.exp(s - m)
        l = jnp.sum(p, axis=-1, keepdims=True)
        o = jnp.dot(p.astype(v.dtype), v, preferred_element_type=F32)
        o_ref[b * w:(b + 1) * w] = o / l
        lse_ref[b * w:(b + 1) * w] = jnp.broadcast_to(m + jnp.log(l), (w, HEAD_DIM))


def _attn_a(qkv, t=2048):
    _, heads, dilation, sub_len, hd = qkv.shape
    rows = heads * dilation * sub_len
    flat = qkv.reshape(3, rows, hd)
    w = WINDOW_STEPS
    assert t % sub_len == 0 or sub_len % t == 0
    bpc = t // w
    cur = lambda sec: pl.BlockSpec((1, t, hd), lambda c: (sec, c, 0))
    prev = lambda sec: pl.BlockSpec((1, w, hd), lambda c: (sec, jnp.maximum(c * bpc - 1, 0), 0))
    out_spec = pl.BlockSpec((t, hd), lambda c: (c, 0))
    out, lse = pl.pallas_call(
        functools.partial(_attn_a_kernel, sub_len=sub_len),
        grid=(rows // t,),
        in_specs=[cur(0), cur(1), prev(1), cur(2), prev(2)],
        out_specs=[out_spec, out_spec],
        out_shape=[jax.ShapeDtypeStruct((rows, hd), F32)] * 2,
        scratch_shapes=[pltpu.VMEM((t + w, hd), BF16)] * 2,
        compiler_params=_params("parallel"),
        name=f"attn_a_r{dilation}",
    )(flat, flat, flat, flat, flat)
    shape = (heads, dilation, sub_len, hd)
    return out.reshape(shape), lse.reshape(shape)


def _mix_a_kernel(o0_ref, l0_ref, o1_ref, l1_ref, o2_ref, l2_ref, y_ref, nat_ref):
    tm = y_ref.shape[0]
    refs = ((o0_ref, l0_ref), (o1_ref, l1_ref), (o2_ref, l2_ref))
    r_max = DIL_PAIRS[-1][1]
    n = tm // r_max
    for hh in range(HEADS_PER_GROUP):
        for rho in range(r_max):
            outs, lses = [], []
            for (o_ref, l_ref), (_, r) in zip(refs, DIL_PAIRS):
                step = r_max // r
                rows = pl.ds(rho // r, n, stride=step) if step > 1 else pl.ds(0, n)
                outs.append(o_ref[hh, rho % r, rows, :])
                lses.append(l_ref[hh, rho % r, rows, :])
            m = jnp.maximum(jnp.maximum(lses[0], lses[1]), lses[2])
            ws = [jnp.exp(l - m) for l in lses]
            num = ws[0] * outs[0] + ws[1] * outs[1] + ws[2] * outs[2]
            den = ws[0] + ws[1] + ws[2]
            nat_ref[hh, pl.ds(rho, n, stride=r_max), :] = num / den
        y_ref[:, hh * HEAD_DIM:(hh + 1) * HEAD_DIM] = nat_ref[hh].astype(y_ref.dtype)


def _mix_a(group_outs, seq, tm=512):
    in_specs, args = [], []
    for (o, l), (_, r) in zip(group_outs, DIL_PAIRS):
        spec = pl.BlockSpec((HEADS_PER_GROUP, r, tm // r, HEAD_DIM), lambda i: (0, 0, i, 0))
        in_specs += [spec, spec]
        args += [o, l]
    return pl.pallas_call(
        _mix_a_kernel,
        grid=(seq // tm,),
        in_specs=in_specs,
        out_specs=pl.BlockSpec((tm, A_OUT), lambda i: (i, 0)),
        out_shape=jax.ShapeDtypeStruct((seq, A_OUT), BF16),
        scratch_shapes=[pltpu.VMEM((HEADS_PER_GROUP, tm, HEAD_DIM), F32)],
        compiler_params=_params("parallel"),
        name="mix_a",
    )(*args)


def _attn_b_kernel(lq1_ref, lk1_ref, lq2_ref, lk2_ref, g_ref, q_ref, k_ref, vt_ref, o_ref,
                   m_ref, acc_ref, st_ref, mb_ref, *, lam_init):
    tq = q_ref.shape[1]
    tk = vt_ref.shape[3]
    width = vt_ref.shape[2] - ONES_ROWS
    qi = pl.program_id(1)
    m_ref[...] = jnp.full(m_ref.shape, -jnp.inf, F32)
    acc_ref[...] = jnp.zeros(acc_ref.shape, F32)

    def scores_to(slot, kv):
        start = pl.multiple_of(kv * tk, tk)
        for i in range(2):
            st = lax.dot_general(
                k_ref[i, pl.ds(start, tk), :], q_ref[i], (((1,), (1,)), ((), ())),
                preferred_element_type=F32)
            st_ref[slot, i] = st
            mb_ref[slot, i] = jnp.max(st, axis=0, keepdims=True)

    def softmax_pv(slot, kv, masked):
        vt = vt_ref[0, kv]
        for i in range(2):
            st = st_ref[slot, i]
            if masked:
                kpos = kv * tk + lax.broadcasted_iota(jnp.int32, st.shape, 0)
                qpos = qi * tq + lax.broadcasted_iota(jnp.int32, st.shape, 1)
                st = jnp.where(kpos <= qpos, st, -jnp.inf)
                m_blk = jnp.max(st, axis=0, keepdims=True)
            else:
                m_blk = mb_ref[slot, i]
            m_prev = m_ref[i]
            m_new = jnp.maximum(m_prev, m_blk)
            alpha = jnp.exp2(m_prev - m_new)
            p = jnp.exp2((st - m_new).astype(vt.dtype))
            acc_ref[i] = alpha * acc_ref[i] + jnp.dot(vt, p, preferred_element_type=F32)
            m_ref[i] = m_new

    scores_to(0, 0)

    def pair(j, carry):
        scores_to(1, 2 * j + 1)
        softmax_pv(0, 2 * j, masked=False)
        scores_to(0, 2 * j + 2)
        softmax_pv(1, 2 * j + 1, masked=False)
        return carry

    lax.fori_loop(0, qi // 2, pair, 0)

    @pl.when(qi % 2 == 0)
    def _():
        softmax_pv(0, qi, masked=True)

    @pl.when(qi % 2 == 1)
    def _():
        scores_to(1, qi)
        softmax_pv(0, qi - 1, masked=False)
        softmax_pv(1, qi, masked=True)

    lam = (jnp.exp(jnp.sum(lq1_ref[...] * lk1_ref[...], axis=-1, keepdims=True))
           - jnp.exp(jnp.sum(lq2_ref[...] * lk2_ref[...], axis=-1, keepdims=True))
           + lam_init)
    inv_l = [1.0 / acc_ref[i, width:width + 1, :] for i in range(2)]
    ot = acc_ref[0, :width, :] * inv_l[0] - lam * (acc_ref[1, :width, :] * inv_l[1])
    ms = jnp.mean(ot * ot, axis=0, keepdims=True)
    yt = (ot * lax.rsqrt(ms + EPS)) * g_ref[...]
    o_ref[...] = (yt * (1.0 - lam_init)).T.astype(o_ref.dtype)


def _attn_b(qk, vt, lam_vecs, subln_g, lam_init):
    _, s, hd = qk.shape
    _, n_kv, rows, tk = vt.shape
    width = rows - ONES_ROWS
    tq = tk
    vec = pl.BlockSpec((1, hd), lambda h, i: (0, 0))
    return pl.pallas_call(
        functools.partial(_attn_b_kernel, lam_init=lam_init),
        grid=(B_HEADS, s // tq),
        in_specs=[vec, vec, vec, vec,
                  pl.BlockSpec((width, 1), lambda h, i: (0, 0)),
                  pl.BlockSpec((2, tq, hd), lambda h, i: (h, i, 0)),
                  pl.BlockSpec((2, s, hd), lambda h, i: (B_HEADS + h, 0, 0)),
                  pl.BlockSpec((1, n_kv, rows, tk), lambda h, i: (h, 0, 0, 0))],
        out_specs=pl.BlockSpec((tq, width), lambda h, i: (i, h)),
        out_shape=jax.ShapeDtypeStruct((s, B_HEADS * width), BF16),
        scratch_shapes=[pltpu.VMEM((2, 1, tq), F32),
                        pltpu.VMEM((2, rows, tq), F32),
                        pltpu.VMEM((2, 2, tk, tq), F32),
                        pltpu.VMEM((2, 2, 1, tq), F32)],
        compiler_params=_params("parallel", "arbitrary"),
        name="attn_b",
    )(*[x.reshape(1, hd) for x in lam_vecs], subln_g.reshape(width, 1), qk, qk, vt)


def _merge_kernel(ya_ref, yb_ref, wa_ref, wb_ref, ga_ref, gb_ref, o_ref):
    pa = jnp.dot(ya_ref[...], wa_ref[...].astype(BF16), preferred_element_type=F32)
    pb = jnp.dot(yb_ref[...], wb_ref[...].astype(BF16), preferred_element_type=F32)
    o_ref[...] = (ga_ref[...].astype(F32) * pa + gb_ref[...].astype(F32) * pb).astype(o_ref.dtype)


def _merge(ya, yb, wa, wb, gates, tm=1024, tn=512):
    s = ya.shape[0]
    n = wa.shape[1]
    return pl.pallas_call(
        _merge_kernel,
        grid=(s // tm, n // tn),
        in_specs=[pl.BlockSpec((tm, ya.shape[1]), lambda i, j: (i, 0)),
                  pl.BlockSpec((tm, yb.shape[1]), lambda i, j: (i, 0)),
                  pl.BlockSpec((wa.shape[0], tn), lambda i, j: (0, j)),
                  pl.BlockSpec((wb.shape[0], tn), lambda i, j: (0, j)),
                  pl.BlockSpec((tm, tn), lambda i, j: (i, j)),
                  pl.BlockSpec((tm, tn), lambda i, j: (i, n // tn + j))],
        out_specs=pl.BlockSpec((tm, tn), lambda i, j: (i, j)),
        out_shape=jax.ShapeDtypeStruct((s, n), BF16),
        compiler_params=_params("parallel", "arbitrary"),
        name="merge",
    )(ya, yb, wa, wb, gates, gates)


def _out_proj_kernel(a_ref, b_ref, x_ref, g_ref, x1_ref, xg_ref, ss_ref):
    x1 = x_ref[...] + jnp.dot(a_ref[...], b_ref[...].astype(BF16), preferred_element_type=F32)
    x1_ref[...] = x1
    xg_ref[...] = (x1 * g_ref[...]).astype(xg_ref.dtype)
    ss_ref[...] = jnp.broadcast_to(jnp.sum(x1 * x1, axis=-1, keepdims=True), ss_ref.shape)


def _out_proj(a, b, x, g, tm=1024, tn=512):
    s, kdim = a.shape
    n = b.shape[1]
    lanes = HEAD_DIM
    return pl.pallas_call(
        _out_proj_kernel,
        grid=(s // tm, n // tn),
        in_specs=[pl.BlockSpec((tm, kdim), lambda i, j: (i, 0)),
                  pl.BlockSpec((kdim, tn), lambda i, j: (0, j)),
                  pl.BlockSpec((tm, tn), lambda i, j: (i, j)),
                  pl.BlockSpec((1, tn), lambda i, j: (0, j))],
        out_specs=[pl.BlockSpec((tm, tn), lambda i, j: (i, j)),
                   pl.BlockSpec((tm, tn), lambda i, j: (i, j)),
                   pl.BlockSpec((tm, lanes), lambda i, j: (i, j))],
        out_shape=[jax.ShapeDtypeStruct((s, n), F32),
                   jax.ShapeDtypeStruct((s, n), BF16),
                   jax.ShapeDtypeStruct((s, (n // tn) * lanes), F32)],
        compiler_params=_params("parallel", "arbitrary"),
        name="out_proj",
    )(a, b, x, g.reshape(1, n))


def _ff1_kernel(xg_ref, ss_ref, b_ref, o_ref, r2_ref):
    lanes = HEAD_DIM
    d = xg_ref.shape[1]

    @pl.when(pl.program_id(1) == 0)
    def _():
        total = ss_ref[:, 0:lanes]
        for p in range(1, ss_ref.shape[1] // lanes):
            total = total + ss_ref[:, p * lanes:(p + 1) * lanes]
        r2_ref[...] = 1.0 / (total * (1.0 / d) + EPS)

    u = jnp.maximum(jnp.dot(xg_ref[...], b_ref[...].astype(BF16), preferred_element_type=F32), 0.0)
    for c in range(o_ref.shape[1] // lanes):
        uc = u[:, c * lanes:(c + 1) * lanes]
        o_ref[:, c * lanes:(c + 1) * lanes] = (uc * uc * r2_ref[...]).astype(o_ref.dtype)


def _ff1(xg, ss, b, tm=1024, tn=512):
    s, kdim = xg.shape
    n = b.shape[1]
    return pl.pallas_call(
        _ff1_kernel,
        grid=(s // tm, n // tn),
        in_specs=[pl.BlockSpec((tm, kdim), lambda i, j: (i, 0)),
                  pl.BlockSpec((tm, ss.shape[1]), lambda i, j: (i, 0)),
                  pl.BlockSpec((kdim, tn), lambda i, j: (0, j))],
        out_specs=pl.BlockSpec((tm, tn), lambda i, j: (i, j)),
        out_shape=jax.ShapeDtypeStruct((s, n), BF16),
        scratch_shapes=[pltpu.VMEM((tm, HEAD_DIM), F32)],
        compiler_params=_params("parallel", "arbitrary"),
        name="ff1",
    )(xg, ss, b)


def _ff2_kernel(a_ref, b_ref, x_ref, o_ref):
    @pl.when(pl.program_id(2) == 0)
    def _():
        o_ref[...] = x_ref[...]

    o_ref[...] += jnp.dot(a_ref[...], b_ref[...].astype(BF16), preferred_element_type=F32)


def _ff2(a, b, x, tm=1024, tn=1024, tk=2048):
    s, kdim = a.shape
    n = b.shape[1]
    return pl.pallas_call(
        _ff2_kernel,
        grid=(s // tm, n // tn, kdim // tk),
        in_specs=[pl.BlockSpec((tm, tk), lambda i, j, k: (i, k)),
                  pl.BlockSpec((tk, tn), lambda i, j, k: (k, j)),
                  pl.BlockSpec((tm, tn), lambda i, j, k: (i, j))],
        out_specs=pl.BlockSpec((tm, tn), lambda i, j, k: (i, j)),
        out_shape=jax.ShapeDtypeStruct((s, n), F32),
        compiler_params=_params("parallel", "parallel", "arbitrary"),
        name="ff2",
    )(a, b, x)


def _lambda_init(layer_idx):
    return 0.8 - 0.6 * math.exp(-0.3 * layer_idx)


def kernel(x, norm_mix, w_in, b_gate, w_proj_a, w_proj_b, w_out, lambda_q1, lambda_k1,
           lambda_q2, lambda_k2, subln_g, norm_mlp, w_ff1, w_ff2, norm_final):
    batch, seq, d = x.shape
    depth = w_in.shape[0]
    tables = _rope_tables(seq)
    outs = []
    for b in range(batch):
        xb = x.reshape(seq, d) if batch == 1 else x[b]
        for l in range(depth):
            w_in_l = w_in[l]
            h = _rmsnorm(xb, norm_mix[l], BF16)
            groups = []
            for g in range(N_GROUPS):
                groups.append(_attn_a(_proj_a(h, w_in_l, tables, g)))
            ya = _mix_a(groups, seq)
            qkb = _proj_bqk(h, w_in_l, tables)
            vb = _proj_bv(h, w_in_l)
            lam_init = _lambda_init(l)
            yb = _attn_b(qkb, vb, (lambda_q1[l], lambda_k1[l], lambda_q2[l], lambda_k2[l]),
                         subln_g[l], lam_init)
            gates = _proj_gate(h, w_in_l, b_gate[l])
            merged = _merge(ya, yb, w_proj_a[l], w_proj_b[l], gates)
            xb, xg, ss = _out_proj(merged, w_out[l], xb, norm_mlp[l])
            u = _ff1(xg, ss, w_ff1[l])
            xb = _ff2(u, w_ff2[l], xb)
        outs.append(_rmsnorm(xb, norm_final, x.dtype))
    if batch == 1:
        return outs[0].reshape(1, seq, d)
    return jnp.stack(outs, axis=0)
```

```python
import functools
import math

import jax
import jax.numpy as jnp
from jax import lax
from jax.experimental import pallas as pl
from jax.experimental.pallas import tpu as pltpu

D_MODEL = 4096
HEAD_DIM = 128
DIL_PAIRS = ((128, 1), (512, 4), (2048, 16))
N_GROUPS = len(DIL_PAIRS)
HEADS_PER_GROUP = 4
GROUP_COLS = HEADS_PER_GROUP * HEAD_DIM
A_WIDTH = N_GROUPS * GROUP_COLS
A_OUT = GROUP_COLS
B_HEADS = 8
B_QK = B_HEADS * 2 * HEAD_DIM
B_V = B_HEADS * 2 * HEAD_DIM
D_FF = 4 * D_MODEL
ROPE_THETA = 500000.0
ROT_DIM = HEAD_DIM // 4
ROT_HALF = ROT_DIM // 2
EPS = 1e-6
QK_SCALE = 1.0 / math.sqrt(HEAD_DIM)
LOG2_E = math.log2(math.e)
WINDOW_STEPS = 128
ONES_ROWS = 16
MXU_COLS = 256

COL_QA = 0
COL_KA = A_WIDTH
COL_VA = 2 * A_WIDTH
COL_QB = 3 * A_WIDTH
COL_KB = COL_QB + B_QK
COL_VB = COL_KB + B_QK
COL_GATE = COL_VB + B_V
IN_COLS = COL_GATE + 2 * D_MODEL

VMEM_LIMIT_BYTES = 48 * 1024 * 1024

F32 = jnp.float32
BF16 = jnp.bfloat16


def _params(*semantics):
    return pltpu.CompilerParams(dimension_semantics=semantics,
                                vmem_limit_bytes=VMEM_LIMIT_BYTES)


def _rmsnorm_kernel(x_ref, g_ref, o_ref):
    x = x_ref[...]
    ms = jnp.mean(x * x, axis=-1, keepdims=True)
    o_ref[...] = ((x * lax.rsqrt(ms + EPS)) * g_ref[...]).astype(o_ref.dtype)


def _rmsnorm(x, g, out_dtype, rows=256):
    s, d = x.shape
    return pl.pallas_call(
        _rmsnorm_kernel,
        grid=(s // rows,),
        in_specs=[pl.BlockSpec((rows, d), lambda i: (i, 0)),
                  pl.BlockSpec((1, d), lambda i: (0, 0))],
        out_specs=pl.BlockSpec((rows, d), lambda i: (i, 0)),
        out_shape=jax.ShapeDtypeStruct((s, d), out_dtype),
        compiler_params=_params("parallel"),
        name="rmsnorm",
    )(x, g.reshape(1, d))


def _rope_tables(seq):
    pos = jnp.arange(seq, dtype=F32)
    inv = ROPE_THETA ** (-jnp.arange(0, ROT_DIM, 2, dtype=F32) / ROT_DIM)
    ang = pos[:, None] * inv[None, :]
    cos, sin = jnp.cos(ang), jnp.sin(ang)
    ones = jnp.ones((seq, HEAD_DIM - ROT_DIM), F32)
    zeros = jnp.zeros((seq, HEAD_DIM - ROT_DIM), F32)
    zh = jnp.zeros((seq, ROT_HALF), F32)
    cos_t = jnp.concatenate([cos, cos, ones], axis=1)
    sin_lo = jnp.concatenate([-sin, zh, zeros], axis=1)
    sin_hi = jnp.concatenate([zh, sin, zeros], axis=1)
    rot = jnp.stack([cos_t, sin_lo, sin_hi])
    ident = jnp.stack([jnp.ones_like(cos_t), jnp.zeros_like(cos_t), jnp.zeros_like(cos_t)])
    return jnp.stack([rot, ident])


def _rope_head(y, tab_ref, scale):
    x2 = pltpu.roll(y, HEAD_DIM - ROT_HALF, 1)
    x1 = pltpu.roll(y, ROT_HALF, 1)
    return (y * tab_ref[0, 0] + x2 * tab_ref[0, 1] + x1 * tab_ref[0, 2]) * scale


def _store_product(a_ref, w_ref, acc_ref, *_):
    acc_ref[...] = jnp.dot(a_ref[...], w_ref[...].astype(BF16), preferred_element_type=F32)


def _lagged_kernel(*refs, n_mm, n_ep, n_out, n_tiles, n_j, matmul, epilogue):
    mm_refs = refs[:n_mm]
    ep_refs = refs[n_mm:n_mm + n_ep]
    out_refs = refs[n_mm + n_ep:n_mm + n_ep + n_out]
    scratch = refs[n_mm + n_ep + n_out:]
    t = pl.program_id(0)
    jp = jnp.maximum(t - 1, 0) % n_j

    @pl.when(t == 0)
    def _():
        matmul(*mm_refs, *scratch)

    @pl.when(jnp.logical_and(t > 0, t < n_tiles))
    def _():
        epilogue(jp, *ep_refs, *out_refs, *scratch)
        matmul(*mm_refs, *scratch)

    @pl.when(t == n_tiles)
    def _():
        epilogue(jp, *ep_refs, *out_refs, *scratch)


def _lagged_matmul(name, epilogue, *, n_i, n_j, mm_args, mm_specs, ep_args, ep_specs,
                   out_shapes, out_specs, scratch_shapes, matmul=_store_product):
    n_tiles = n_i * n_j

    def cur(f):
        return lambda t: f(*divmod(jnp.minimum(t, n_tiles - 1), n_j))

    def prev(f):
        return lambda t: f(*divmod(jnp.maximum(t - 1, 0), n_j))

    in_specs = ([pl.BlockSpec(shape, cur(f)) for shape, f in mm_specs]
                + [pl.BlockSpec(shape, prev(f)) for shape, f in ep_specs])
    return pl.pallas_call(
        functools.partial(_lagged_kernel, n_mm=len(mm_args), n_ep=len(ep_args),
                          n_out=len(out_shapes), n_tiles=n_tiles, n_j=n_j,
                          matmul=matmul, epilogue=epilogue),
        grid=(n_tiles + 1,),
        in_specs=in_specs,
        out_specs=[pl.BlockSpec(shape, prev(f)) for shape, f in out_specs],
        out_shape=out_shapes,
        scratch_shapes=scratch_shapes,
        compiler_params=_params("arbitrary"),
        name=name,
    )(*mm_args, *ep_args)


def _proj_a_epilogue(jp, tab_ref, o_ref, acc_ref, y_ref, *, dilation):
    tm = acc_ref.shape[0]
    n = tm // dilation
    scale = jnp.where(jp == 0, QK_SCALE, 1.0).astype(F32)
    for hh in range(HEADS_PER_GROUP):
        y = _rope_head(acc_ref[:, hh * HEAD_DIM:(hh + 1) * HEAD_DIM], tab_ref, scale)
        if dilation == 1:
            o_ref[0, hh, 0] = y.astype(o_ref.dtype)
            continue
        y_ref[hh] = y
        for rho in range(dilation):
            o_ref[0, hh, rho] = y_ref[hh, pl.ds(rho, n, stride=dilation), :].astype(o_ref.dtype)


def _proj_a(h, w_in, tables, group, tm=1024):
    s, d = h.shape
    dilation = DIL_PAIRS[group][1]
    sub_len = s // dilation
    blocks_per_section = A_WIDTH // GROUP_COLS
    return _lagged_matmul(
        f"proj_a_g{group}", functools.partial(_proj_a_epilogue, dilation=dilation),
        n_i=s // tm, n_j=3,
        mm_args=[h, w_in],
        mm_specs=[((tm, d), lambda i, j: (i, 0)),
                  ((d, GROUP_COLS), lambda i, j: (0, j * blocks_per_section + group))],
        ep_args=[tables],
        ep_specs=[((1, 3, tm, HEAD_DIM), lambda i, j: (j // 2, 0, i, 0))],
        out_shapes=[jax.ShapeDtypeStruct((3, HEADS_PER_GROUP, dilation, sub_len, HEAD_DIM), BF16)],
        out_specs=[((1, HEADS_PER_GROUP, dilation, tm // dilation, HEAD_DIM),
                    lambda i, j: (j, 0, 0, i, 0))],
        scratch_shapes=[pltpu.VMEM((tm, GROUP_COLS), F32),
                        pltpu.VMEM((HEADS_PER_GROUP, tm, HEAD_DIM), F32)])[0]


def _proj_bqk_epilogue(jp, tab_ref, o_ref, acc_ref, *, q_tiles):
    scale = jnp.where(jp < q_tiles, QK_SCALE * LOG2_E, 1.0).astype(F32)
    for hh in range(o_ref.shape[0]):
        y = _rope_head(acc_ref[:, hh * HEAD_DIM:(hh + 1) * HEAD_DIM], tab_ref, scale)
        o_ref[hh] = y.astype(o_ref.dtype)


def _proj_bqk(h, w_in, tables, tm=1024, tn=512):
    s, d = h.shape
    heads_per_tile = tn // HEAD_DIM
    n = B_QK + B_QK
    return _lagged_matmul(
        "proj_b_qk", functools.partial(_proj_bqk_epilogue, q_tiles=B_QK // tn),
        n_i=s // tm, n_j=n // tn,
        mm_args=[h, w_in],
        mm_specs=[((tm, d), lambda i, j: (i, 0)),
                  ((d, tn), lambda i, j: (0, COL_QB // tn + j))],
        ep_args=[tables],
        ep_specs=[((1, 3, tm, HEAD_DIM), lambda i, j: (0, 0, i, 0))],
        out_shapes=[jax.ShapeDtypeStruct((n // HEAD_DIM, s, HEAD_DIM), BF16)],
        out_specs=[((heads_per_tile, tm, HEAD_DIM), lambda i, j: (j, i, 0))],
        scratch_shapes=[pltpu.VMEM((tm, tn), F32)])[0]


def _proj_bv_kernel(h_ref, w_ref, o_ref):
    acc = jnp.dot(h_ref[...], w_ref[...].astype(BF16), preferred_element_type=F32)
    heads, blocks, rows, tk = o_ref.shape
    width = rows - ONES_ROWS
    for hv in range(heads):
        vt = acc[:, hv * width:(hv + 1) * width].T
        for c in range(blocks):
            o_ref[hv, c, :width, :] = vt[:, c * tk:(c + 1) * tk].astype(o_ref.dtype)
            o_ref[hv, c, width:, :] = jnp.ones((ONES_ROWS, tk), o_ref.dtype)


def _proj_bv(h, w_in, tm=1024, tn=512, tk=512):
    s, d = h.shape
    width = 2 * HEAD_DIM
    rows = width + ONES_ROWS
    return pl.pallas_call(
        _proj_bv_kernel,
        grid=(s // tm, B_V // tn),
        in_specs=[pl.BlockSpec((tm, d), lambda i, j: (i, 0)),
                  pl.BlockSpec((d, tn), lambda i, j: (0, COL_VB // tn + j))],
        out_specs=pl.BlockSpec((tn // width, tm // tk, rows, tk), lambda i, j: (j, i, 0, 0)),
        out_shape=jax.ShapeDtypeStruct((B_HEADS, s // tk, rows, tk), BF16),
        compiler_params=_params("parallel", "arbitrary"),
        name="proj_b_v",
    )(h, w_in)


def _gate_epilogue(jp, b_ref, o_ref, acc_ref):
    z = acc_ref[...] + b_ref[...]
    o_ref[...] = (0.5 * jnp.tanh(0.5 * z) + 0.5).astype(o_ref.dtype)


def _proj_gate(h, w_in, b_gate, tm=1024, tn=512):
    s, d = h.shape
    n = 2 * D_MODEL
    return _lagged_matmul(
        "proj_gate", _gate_epilogue, n_i=s // tm, n_j=n // tn,
        mm_args=[h, w_in],
        mm_specs=[((tm, d), lambda i, j: (i, 0)),
                  ((d, tn), lambda i, j: (0, COL_GATE // tn + j))],
        ep_args=[b_gate.reshape(1, n)],
        ep_specs=[((1, tn), lambda i, j: (0, j))],
        out_shapes=[jax.ShapeDtypeStruct((s, n), BF16)],
        out_specs=[((tm, tn), lambda i, j: (i, j))],
        scratch_shapes=[pltpu.VMEM((tm, tn), F32)])[0]


def _attn_a_kernel(q_ref, kc_ref, kp_ref, vc_ref, vp_ref, o_ref, lse_ref, kk_ref, vv_ref,
                   *, sub_len):
    w = WINDOW_STEPS
    t = q_ref.shape[1]
    c = pl.program_id(0)
    kk_ref[0:w] = kp_ref[0]
    kk_ref[w:] = kc_ref[0]
    vv_ref[0:w] = vp_ref[0]
    vv_ref[w:] = vc_ref[0]
    qp = lax.broadcasted_iota(jnp.int32, (w, 2 * w), 0)
    kp = lax.broadcasted_iota(jnp.int32, (w, 2 * w), 1)
    band = (kp >= qp) & (kp <= qp + w)
    band_start = band & (kp >= w)
    if t >= sub_len:
        masks = [band_start if (b * w) % sub_len == 0 else band for b in range(t // w)]
    else:
        is_start = (c % (sub_len // t)) == 0
        masks = [band & (kp >= jnp.where(is_start, w, 0))] + [band] * (t // w - 1)
    for b in range(t // w):
        q = q_ref[0, b * w:(b + 1) * w]
        k = kk_ref[b * w:(b + 2) * w]
        v = vv_ref[b * w:(b + 2) * w]
        s = lax.dot_general(q, k, (((1,), (1,)), ((), ())), preferred_element_type=F32)
        s = jnp.where(masks[b], s, -jnp.inf)
        m = jnp.max(s, axis=-1, keepdims=True)
        p = jnp.exp(s - m)
        l = jnp.sum(p, axis=-1, keepdims=True)
        o = jnp.dot(p.astype(v.dtype), v, preferred_element_type=F32)
        o_ref[b * w:(b + 1) * w] = o / l
        lse_ref[b * w:(b + 1) * w] = jnp.broadcast_to(m + jnp.log(l), (w, HEAD_DIM))


def _attn_a(qkv, t=2048):
    _, heads, dilation, sub_len, hd = qkv.shape
    rows = heads * dilation * sub_len
    flat = qkv.reshape(3, rows, hd)
    w = WINDOW_STEPS
    assert t % sub_len == 0 or sub_len % t == 0
    bpc = t // w
    cur = lambda sec: pl.BlockSpec((1, t, hd), lambda c: (sec, c, 0))
    prev = lambda sec: pl.BlockSpec((1, w, hd), lambda c: (sec, jnp.maximum(c * bpc - 1, 0), 0))
    out_spec = pl.BlockSpec((t, hd), lambda c: (c, 0))
    out, lse = pl.pallas_call(
        functools.partial(_attn_a_kernel, sub_len=sub_len),
        grid=(rows // t,),
        in_specs=[cur(0), cur(1), prev(1), cur(2), prev(2)],
        out_specs=[out_spec, out_spec],
        out_shape=[jax.ShapeDtypeStruct((rows, hd), F32)] * 2,
        scratch_shapes=[pltpu.VMEM((t + w, hd), BF16)] * 2,
        compiler_params=_params("parallel"),
        name=f"attn_a_r{dilation}",
    )(flat, flat, flat, flat, flat)
    shape = (heads, dilation, sub_len, hd)
    return out.reshape(shape), lse.reshape(shape)


def _mix_a_kernel(o0_ref, l0_ref, o1_ref, l1_ref, o2_ref, l2_ref, y_ref, nat_ref):
    tm = y_ref.shape[0]
    refs = ((o0_ref, l0_ref), (o1_ref, l1_ref), (o2_ref, l2_ref))
    r_max = DIL_PAIRS[-1][1]
    n = tm // r_max
    for hh in range(HEADS_PER_GROUP):
        for rho in range(r_max):
            outs, lses = [], []
            for (o_ref, l_ref), (_, r) in zip(refs, DIL_PAIRS):
                step = r_max // r
                rows = pl.ds(rho // r, n, stride=step) if step > 1 else pl.ds(0, n)
                outs.append(o_ref[hh, rho % r, rows, :])
                lses.append(l_ref[hh, rho % r, rows, :])
            m = jnp.maximum(jnp.maximum(lses[0], lses[1]), lses[2])
            ws = [jnp.exp(l - m) for l in lses]
            num = ws[0] * outs[0] + ws[1] * outs[1] + ws[2] * outs[2]
            den = ws[0] + ws[1] + ws[2]
            nat_ref[hh, pl.ds(rho, n, stride=r_max), :] = num / den
        y_ref[:, hh * HEAD_DIM:(hh + 1) * HEAD_DIM] = nat_ref[hh].astype(y_ref.dtype)


def _mix_a(group_outs, seq, tm=512):
    in_specs, args = [], []
    for (o, l), (_, r) in zip(group_outs, DIL_PAIRS):
        spec = pl.BlockSpec((HEADS_PER_GROUP, r, tm // r, HEAD_DIM), lambda i: (0, 0, i, 0))
        in_specs += [spec, spec]
        args += [o, l]
    return pl.pallas_call(
        _mix_a_kernel,
        grid=(seq // tm,),
        in_specs=in_specs,
        out_specs=pl.BlockSpec((tm, A_OUT), lambda i: (i, 0)),
        out_shape=jax.ShapeDtypeStruct((seq, A_OUT), BF16),
        scratch_shapes=[pltpu.VMEM((HEADS_PER_GROUP, tm, HEAD_DIM), F32)],
        compiler_params=_params("parallel"),
        name="mix_a",
    )(*args)


def _attn_b_kernel(lq1_ref, lk1_ref, lq2_ref, lk2_ref, g_ref, q_ref, k_ref, vt_ref, o_ref,
                   m_ref, acc_ref, st_ref, mb_ref, *, lam_init):
    tq = q_ref.shape[1]
    tk = vt_ref.shape[3]
    width = vt_ref.shape[2] - ONES_ROWS
    qi = pl.program_id(1)
    m_ref[...] = jnp.full(m_ref.shape, -jnp.inf, F32)
    acc_ref[...] = jnp.zeros(acc_ref.shape, F32)

    def scores_to(slot, kv):
        start = pl.multiple_of(kv * tk, tk)
        for i in range(2):
            st = lax.dot_general(
                k_ref[i, pl.ds(start, tk), :], q_ref[i], (((1,), (1,)), ((), ())),
                preferred_element_type=F32)
            st_ref[slot, i] = st
            mb_ref[slot, i] = jnp.max(st, axis=0, keepdims=True)

    def softmax_pv(slot, kv, masked):
        vt = vt_ref[0, kv]
        for i in range(2):
            st = st_ref[slot, i]
            if masked:
                kpos = kv * tk + lax.broadcasted_iota(jnp.int32, st.shape, 0)
                qpos = qi * tq + lax.broadcasted_iota(jnp.int32, st.shape, 1)
                st = jnp.where(kpos <= qpos, st, -jnp.inf)
                m_blk = jnp.max(st, axis=0, keepdims=True)
            else:
                m_blk = mb_ref[slot, i]
            m_prev = m_ref[i]
            m_new = jnp.maximum(m_prev, m_blk)
            alpha = jnp.exp2(m_prev - m_new)
            p = jnp.exp2((st - m_new).astype(vt.dtype))
            acc_ref[i] = alpha * acc_ref[i] + jnp.dot(vt, p, preferred_element_type=F32)
            m_ref[i] = m_new

    scores_to(0, 0)

    def pair(j, carry):
        scores_to(1, 2 * j + 1)
        softmax_pv(0, 2 * j, masked=False)
        scores_to(0, 2 * j + 2)
        softmax_pv(1, 2 * j + 1, masked=False)
        return carry

    lax.fori_loop(0, qi // 2, pair, 0)

    @pl.when(qi % 2 == 0)
    def _():
        softmax_pv(0, qi, masked=True)

    @pl.when(qi % 2 == 1)
    def _():
        scores_to(1, qi)
        softmax_pv(0, qi - 1, masked=False)
        softmax_pv(1, qi, masked=True)

    lam = (jnp.exp(jnp.sum(lq1_ref[...] * lk1_ref[...], axis=-1, keepdims=True))
           - jnp.exp(jnp.sum(lq2_ref[...] * lk2_ref[...], axis=-1, keepdims=True))
           + lam_init)
    inv_l = [1.0 / acc_ref[i, width:width + 1, :] for i in range(2)]
    ot = acc_ref[0, :width, :] * inv_l[0] - lam * (acc_ref[1, :width, :] * inv_l[1])
    ms = jnp.mean(ot * ot, axis=0, keepdims=True)
    yt = (ot * lax.rsqrt(ms + EPS)) * g_ref[...]
    o_ref[...] = (yt * (1.0 - lam_init)).T.astype(o_ref.dtype)


def _attn_b(qk, vt, lam_vecs, subln_g, lam_init):
    _, s, hd = qk.shape
    _, n_kv, rows, tk = vt.shape
    width = rows - ONES_ROWS
    tq = tk
    vec = pl.BlockSpec((1, hd), lambda h, i: (0, 0))
    return pl.pallas_call(
        functools.partial(_attn_b_kernel, lam_init=lam_init),
        grid=(B_HEADS, s // tq),
        in_specs=[vec, vec, vec, vec,
                  pl.BlockSpec((width, 1), lambda h, i: (0, 0)),
                  pl.BlockSpec((2, tq, hd), lambda h, i: (h, i, 0)),
                  pl.BlockSpec((2, s, hd), lambda h, i: (B_HEADS + h, 0, 0)),
                  pl.BlockSpec((1, n_kv, rows, tk), lambda h, i: (h, 0, 0, 0))],
        out_specs=pl.BlockSpec((tq, width), lambda h, i: (i, h)),
        out_shape=jax.ShapeDtypeStruct((s, B_HEADS * width), BF16),
        scratch_shapes=[pltpu.VMEM((2, 1, tq), F32),
                        pltpu.VMEM((2, rows, tq), F32),
                        pltpu.VMEM((2, 2, tk, tq), F32),
                        pltpu.VMEM((2, 2, 1, tq), F32)],
        compiler_params=_params("parallel", "arbitrary"),
        name="attn_b",
    )(*[x.reshape(1, hd) for x in lam_vecs], subln_g.reshape(width, 1), qk, qk, vt)


def _merge_kernel(ya_ref, yb_ref, wa_ref, wb_ref, ga_ref, gb_ref, o_ref):
    pa = jnp.dot(ya_ref[...], wa_ref[...].astype(BF16), preferred_element_type=F32)
    pb = jnp.dot(yb_ref[...], wb_ref[...].astype(BF16), preferred_element_type=F32)
    o_ref[...] = (ga_ref[...].astype(F32) * pa + gb_ref[...].astype(F32) * pb).astype(o_ref.dtype)


def _merge(ya, yb, wa, wb, gates, tm=1024, tn=512):
    s = ya.shape[0]
    n = wa.shape[1]
    return pl.pallas_call(
        _merge_kernel,
        grid=(s // tm, n // tn),
        in_specs=[pl.BlockSpec((tm, ya.shape[1]), lambda i, j: (i, 0)),
                  pl.BlockSpec((tm, yb.shape[1]), lambda i, j: (i, 0)),
                  pl.BlockSpec((wa.shape[0], tn), lambda i, j: (0, j)),
                  pl.BlockSpec((wb.shape[0], tn), lambda i, j: (0, j)),
                  pl.BlockSpec((tm, tn), lambda i, j: (i, j)),
                  pl.BlockSpec((tm, tn), lambda i, j: (i, n // tn + j))],
        out_specs=pl.BlockSpec((tm, tn), lambda i, j: (i, j)),
        out_shape=jax.ShapeDtypeStruct((s, n), BF16),
        compiler_params=_params("parallel", "arbitrary"),
        name="merge",
    )(ya, yb, wa, wb, gates, gates)


def _out_proj_kernel(a_ref, b_ref, x_ref, g_ref, x1_ref, xg_ref, ss_ref):
    x1 = x_ref[...] + jnp.dot(a_ref[...], b_ref[...].astype(BF16), preferred_element_type=F32)
    x1_ref[...] = x1
    xg_ref[...] = (x1 * g_ref[...]).astype(xg_ref.dtype)
    ss_ref[...] = jnp.broadcast_to(jnp.sum(x1 * x1, axis=-1, keepdims=True), ss_ref.shape)


def _out_proj(a, b, x, g, tm=1024, tn=512):
    s, kdim = a.shape
    n = b.shape[1]
    lanes = HEAD_DIM
    return pl.pallas_call(
        _out_proj_kernel,
        grid=(s // tm, n // tn),
        in_specs=[pl.BlockSpec((tm, kdim), lambda i, j: (i, 0)),
                  pl.BlockSpec((kdim, tn), lambda i, j: (0, j)),
                  pl.BlockSpec((tm, tn), lambda i, j: (i, j)),
                  pl.BlockSpec((1, tn), lambda i, j: (0, j))],
        out_specs=[pl.BlockSpec((tm, tn), lambda i, j: (i, j)),
                   pl.BlockSpec((tm, tn), lambda i, j: (i, j)),
                   pl.BlockSpec((tm, lanes), lambda i, j: (i, j))],
        out_shape=[jax.ShapeDtypeStruct((s, n), F32),
                   jax.ShapeDtypeStruct((s, n), BF16),
                   jax.ShapeDtypeStruct((s, (n // tn) * lanes), F32)],
        compiler_params=_params("parallel", "arbitrary"),
        name="out_proj",
    )(a, b, x, g.reshape(1, n))


def _ff1_kernel(xg_ref, ss_ref, b_ref, o_ref, r2_ref):
    lanes = HEAD_DIM
    d = xg_ref.shape[1]

    @pl.when(pl.program_id(1) == 0)
    def _():
        total = ss_ref[:, 0:lanes]
        for p in range(1, ss_ref.shape[1] // lanes):
            total = total + ss_ref[:, p * lanes:(p + 1) * lanes]
        r2_ref[...] = 1.0 / (total * (1.0 / d) + EPS)

    u = jnp.maximum(jnp.dot(xg_ref[...], b_ref[...].astype(BF16), preferred_element_type=F32), 0.0)
    for c in range(o_ref.shape[1] // lanes):
        uc = u[:, c * lanes:(c + 1) * lanes]
        o_ref[:, c * lanes:(c + 1) * lanes] = (uc * uc * r2_ref[...]).astype(o_ref.dtype)


def _ff1(xg, ss, b, tm=1024, tn=512):
    s, kdim = xg.shape
    n = b.shape[1]
    return pl.pallas_call(
        _ff1_kernel,
        grid=(s // tm, n // tn),
        in_specs=[pl.BlockSpec((tm, kdim), lambda i, j: (i, 0)),
                  pl.BlockSpec((tm, ss.shape[1]), lambda i, j: (i, 0)),
                  pl.BlockSpec((kdim, tn), lambda i, j: (0, j))],
        out_specs=pl.BlockSpec((tm, tn), lambda i, j: (i, j)),
        out_shape=jax.ShapeDtypeStruct((s, n), BF16),
        scratch_shapes=[pltpu.VMEM((tm, HEAD_DIM), F32)],
        compiler_params=_params("parallel", "arbitrary"),
        name="ff1",
    )(xg, ss, b)


def _ff2_kernel(a_ref, b_ref, x_ref, o_ref):
    @pl.when(pl.program_id(2) == 0)
    def _():
        o_ref[...] = x_ref[...]

    o_ref[...] += jnp.dot(a_ref[...], b_ref[...].astype(BF16), preferred_element_type=F32)


def _ff2(a, b, x, tm=1024, tn=1024, tk=2048):
    s, kdim = a.shape
    n = b.shape[1]
    return pl.pallas_call(
        _ff2_kernel,
        grid=(s // tm, n // tn, kdim // tk),
        in_specs=[pl.BlockSpec((tm, tk), lambda i, j, k: (i, k)),
                  pl.BlockSpec((tk, tn), lambda i, j, k: (k, j)),
                  pl.BlockSpec((tm, tn), lambda i, j, k: (i, j))],
        out_specs=pl.BlockSpec((tm, tn), lambda i, j, k: (i, j)),
        out_shape=jax.ShapeDtypeStruct((s, n), F32),
        compiler_params=_params("parallel", "parallel", "arbitrary"),
        name="ff2",
    )(a, b, x)


def _lambda_init(layer_idx):
    return 0.8 - 0.6 * math.exp(-0.3 * layer_idx)


def kernel(x, norm_mix, w_in, b_gate, w_proj_a, w_proj_b, w_out, lambda_q1, lambda_k1,
           lambda_q2, lambda_k2, subln_g, norm_mlp, w_ff1, w_ff2, norm_final):
    batch, seq, d = x.shape
    depth = w_in.shape[0]
    tables = _rope_tables(seq)
    outs = []
    for b in range(batch):
        xb = x.reshape(seq, d) if batch == 1 else x[b]
        for l in range(depth):
            w_in_l = w_in[l]
            h = _rmsnorm(xb, norm_mix[l], BF16)
            groups = []
            for g in range(N_GROUPS):
                groups.append(_attn_a(_proj_a(h, w_in_l, tables, g)))
            ya = _mix_a(groups, seq)
            qkb = _proj_bqk(h, w_in_l, tables)
            vb = _proj_bv(h, w_in_l)
            lam_init = _lambda_init(l)
            yb = _attn_b(qkb, vb, (lambda_q1[l], lambda_k1[l], lambda_q2[l], lambda_k2[l]),
                         subln_g[l], lam_init)
            gates = _proj_gate(h, w_in_l, b_gate[l])
            merged = _merge(ya, yb, w_proj_a[l], w_proj_b[l], gates)
            xb, xg, ss = _out_proj(merged, w_out[l], xb, norm_mlp[l])
            u = _ff1(xg, ss, w_ff1[l])
            xb = _ff2(u, w_ff2[l], xb)
        outs.append(_rmsnorm(xb, norm_final, x.dtype))
    if batch == 1:
        return outs[0].reshape(1, seq, d)
    return jnp.stack(outs, axis=0)
```

```python
import functools
import math

import jax
import jax.numpy as jnp
from jax import lax
from jax.experimental import pallas as pl
from jax.experimental.pallas import tpu as pltpu

D_MODEL = 4096
HEAD_DIM = 128
DIL_PAIRS = ((128, 1), (512, 4), (2048, 16))
N_GROUPS = len(DIL_PAIRS)
HEADS_PER_GROUP = 4
GROUP_COLS = HEADS_PER_GROUP * HEAD_DIM
A_WIDTH = N_GROUPS * GROUP_COLS
A_OUT = GROUP_COLS
B_HEADS = 8
B_QK = B_HEADS * 2 * HEAD_DIM
B_V = B_HEADS * 2 * HEAD_DIM
D_FF = 4 * D_MODEL
ROPE_THETA = 500000.0
ROT_DIM = HEAD_DIM // 4
ROT_HALF = ROT_DIM // 2
EPS = 1e-6
QK_SCALE = 1.0 / math.sqrt(HEAD_DIM)
LOG2_E = math.log2(math.e)
WINDOW_STEPS = 128
ONES_ROWS = 16
MXU_COLS = 256
BF16_SUBLANES = 16

COL_QA = 0
COL_KA = A_WIDTH
COL_VA = 2 * A_WIDTH
COL_QB = 3 * A_WIDTH
COL_KB = COL_QB + B_QK
COL_VB = COL_KB + B_QK
COL_GATE = COL_VB + B_V
IN_COLS = COL_GATE + 2 * D_MODEL

VMEM_LIMIT_BYTES = 48 * 1024 * 1024

F32 = jnp.float32
BF16 = jnp.bfloat16


def _params(*semantics):
    return pltpu.CompilerParams(dimension_semantics=semantics,
                                vmem_limit_bytes=VMEM_LIMIT_BYTES)


def _rmsnorm_kernel(x_ref, g_ref, o_ref):
    x = x_ref[...]
    ms = jnp.mean(x * x, axis=-1, keepdims=True)
    o_ref[...] = ((x * lax.rsqrt(ms + EPS)) * g_ref[...]).astype(o_ref.dtype)


def _rmsnorm(x, g, out_dtype, rows=256):
    s, d = x.shape
    return pl.pallas_call(
        _rmsnorm_kernel,
        grid=(s // rows,),
        in_specs=[pl.BlockSpec((rows, d), lambda i: (i, 0)),
                  pl.BlockSpec((1, d), lambda i: (0, 0))],
        out_specs=pl.BlockSpec((rows, d), lambda i: (i, 0)),
        out_shape=jax.ShapeDtypeStruct((s, d), out_dtype),
        compiler_params=_params("parallel"),
        name="rmsnorm",
    )(x, g.reshape(1, d))


def _rope_tables(seq):
    pos = jnp.arange(seq, dtype=F32)
    inv = ROPE_THETA ** (-jnp.arange(0, ROT_DIM, 2, dtype=F32) / ROT_DIM)
    ang = pos[:, None] * inv[None, :]
    cos, sin = jnp.cos(ang), jnp.sin(ang)
    ones = jnp.ones((seq, HEAD_DIM - ROT_DIM), F32)
    zeros = jnp.zeros((seq, HEAD_DIM - ROT_DIM), F32)
    zh = jnp.zeros((seq, ROT_HALF), F32)
    cos_t = jnp.concatenate([cos, cos, ones], axis=1)
    sin_lo = jnp.concatenate([-sin, zh, zeros], axis=1)
    sin_hi = jnp.concatenate([zh, sin, zeros], axis=1)
    rot = jnp.stack([cos_t, sin_lo, sin_hi])
    ident = jnp.stack([jnp.ones_like(cos_t), jnp.zeros_like(cos_t), jnp.zeros_like(cos_t)])
    return jnp.stack([rot, ident])


def _rope_head(y, tab_ref, scale):
    x2 = pltpu.roll(y, HEAD_DIM - ROT_HALF, 1)
    x1 = pltpu.roll(y, ROT_HALF, 1)
    return (y * tab_ref[0, 0] + x2 * tab_ref[0, 1] + x1 * tab_ref[0, 2]) * scale


def _store_product(a_ref, w_ref, acc_ref, *_):
    acc_ref[...] = jnp.dot(a_ref[...], w_ref[...].astype(BF16), preferred_element_type=F32)


def _lagged_kernel(*refs, n_mm, n_ep, n_out, n_tiles, n_j, matmul, epilogue):
    mm_refs = refs[:n_mm]
    ep_refs = refs[n_mm:n_mm + n_ep]
    out_refs = refs[n_mm + n_ep:n_mm + n_ep + n_out]
    scratch = refs[n_mm + n_ep + n_out:]
    t = pl.program_id(0)
    jp = jnp.maximum(t - 1, 0) % n_j

    @pl.when(t == 0)
    def _():
        matmul(*mm_refs, *scratch)

    @pl.when(jnp.logical_and(t > 0, t < n_tiles))
    def _():
        epilogue(jp, *ep_refs, *out_refs, *scratch)
        matmul(*mm_refs, *scratch)

    @pl.when(t == n_tiles)
    def _():
        epilogue(jp, *ep_refs, *out_refs, *scratch)


def _lagged_matmul(name, epilogue, *, n_i, n_j, mm_args, mm_specs, ep_args, ep_specs,
                   out_shapes, out_specs, scratch_shapes, matmul=_store_product):
    n_tiles = n_i * n_j

    def cur(f):
        return lambda t: f(*divmod(jnp.minimum(t, n_tiles - 1), n_j))

    def prev(f):
        return lambda t: f(*divmod(jnp.maximum(t - 1, 0), n_j))

    in_specs = ([pl.BlockSpec(shape, cur(f)) for shape, f in mm_specs]
                + [pl.BlockSpec(shape, prev(f)) for shape, f in ep_specs])
    return pl.pallas_call(
        functools.partial(_lagged_kernel, n_mm=len(mm_args), n_ep=len(ep_args),
                          n_out=len(out_shapes), n_tiles=n_tiles, n_j=n_j,
                          matmul=matmul, epilogue=epilogue),
        grid=(n_tiles + 1,),
        in_specs=in_specs,
        out_specs=[pl.BlockSpec(shape, prev(f)) for shape, f in out_specs],
        out_shape=out_shapes,
        scratch_shapes=scratch_shapes,
        compiler_params=_params("arbitrary"),
        name=name,
    )(*mm_args, *ep_args)


def _proj_a_epilogue(jp, tab_ref, o_ref, acc_ref, y_ref, *, dilation):
    tm = acc_ref.shape[0]
    n = tm // dilation
    scale = jnp.where(jp == 0, QK_SCALE, 1.0).astype(F32)
    for hh in range(HEADS_PER_GROUP):
        y = _rope_head(acc_ref[:, hh * HEAD_DIM:(hh + 1) * HEAD_DIM], tab_ref, scale)
        if dilation == 1:
            o_ref[0, hh, 0] = y.astype(o_ref.dtype)
            continue
        y_ref[hh] = y
        for rho in range(dilation):
            o_ref[0, hh, rho] = y_ref[hh, pl.ds(rho, n, stride=dilation), :].astype(o_ref.dtype)


def _proj_a(h, w_in, tables, group, tm=1024):
    s, d = h.shape
    dilation = DIL_PAIRS[group][1]
    sub_len = s // dilation
    blocks_per_section = A_WIDTH // GROUP_COLS
    return _lagged_matmul(
        f"proj_a_g{group}", functools.partial(_proj_a_epilogue, dilation=dilation),
        n_i=s // tm, n_j=3,
        mm_args=[h, w_in],
        mm_specs=[((tm, d), lambda i, j: (i, 0)),
                  ((d, GROUP_COLS), lambda i, j: (0, j * blocks_per_section + group))],
        ep_args=[tables],
        ep_specs=[((1, 3, tm, HEAD_DIM), lambda i, j: (j // 2, 0, i, 0))],
        out_shapes=[jax.ShapeDtypeStruct((3, HEADS_PER_GROUP, dilation, sub_len, HEAD_DIM), BF16)],
        out_specs=[((1, HEADS_PER_GROUP, dilation, tm // dilation, HEAD_DIM),
                    lambda i, j: (j, 0, 0, i, 0))],
        scratch_shapes=[pltpu.VMEM((tm, GROUP_COLS), F32),
                        pltpu.VMEM((HEADS_PER_GROUP, tm, HEAD_DIM), F32)])[0]


def _proj_bqk_epilogue(jp, tab_ref, o_ref, acc_ref, *, q_tiles):
    scale = jnp.where(jp < q_tiles, QK_SCALE * LOG2_E, 1.0).astype(F32)
    for hh in range(o_ref.shape[0]):
        y = _rope_head(acc_ref[:, hh * HEAD_DIM:(hh + 1) * HEAD_DIM], tab_ref, scale)
        o_ref[hh] = y.astype(o_ref.dtype)


def _proj_bqk(h, w_in, tables, tm=1024, tn=512):
    s, d = h.shape
    heads_per_tile = tn // HEAD_DIM
    n = B_QK + B_QK
    return _lagged_matmul(
        "proj_b_qk", functools.partial(_proj_bqk_epilogue, q_tiles=B_QK // tn),
        n_i=s // tm, n_j=n // tn,
        mm_args=[h, w_in],
        mm_specs=[((tm, d), lambda i, j: (i, 0)),
                  ((d, tn), lambda i, j: (0, COL_QB // tn + j))],
        ep_args=[tables],
        ep_specs=[((1, 3, tm, HEAD_DIM), lambda i, j: (0, 0, i, 0))],
        out_shapes=[jax.ShapeDtypeStruct((n // HEAD_DIM, s, HEAD_DIM), BF16)],
        out_specs=[((heads_per_tile, tm, HEAD_DIM), lambda i, j: (j, i, 0))],
        scratch_shapes=[pltpu.VMEM((tm, tn), F32)])[0]


def _proj_bv_kernel(h_ref, w_ref, o_ref):
    acc = jnp.dot(h_ref[...], w_ref[...].astype(BF16), preferred_element_type=F32)
    heads, blocks, rows, tk = o_ref.shape
    width = rows - ONES_ROWS
    for hv in range(heads):
        vt = acc[:, hv * width:(hv + 1) * width].T
        for c in range(blocks):
            o_ref[hv, c, :width, :] = vt[:, c * tk:(c + 1) * tk].astype(o_ref.dtype)
            o_ref[hv, c, width:, :] = jnp.ones((ONES_ROWS, tk), o_ref.dtype)


def _proj_bv(h, w_in, tm=1024, tn=512, tk=512):
    s, d = h.shape
    width = 2 * HEAD_DIM
    rows = width + ONES_ROWS
    return pl.pallas_call(
        _proj_bv_kernel,
        grid=(s // tm, B_V // tn),
        in_specs=[pl.BlockSpec((tm, d), lambda i, j: (i, 0)),
                  pl.BlockSpec((d, tn), lambda i, j: (0, COL_VB // tn + j))],
        out_specs=pl.BlockSpec((tn // width, tm // tk, rows, tk), lambda i, j: (j, i, 0, 0)),
        out_shape=jax.ShapeDtypeStruct((B_HEADS, s // tk, rows, tk), BF16),
        compiler_params=_params("parallel", "arbitrary"),
        name="proj_b_v",
    )(h, w_in)


def _gate_epilogue(jp, b_ref, o_ref, acc_ref):
    z = acc_ref[...] + b_ref[...]
    o_ref[...] = (0.5 * jnp.tanh(0.5 * z) + 0.5).astype(o_ref.dtype)


def _proj_gate(h, w_in, b_gate, tm=1024, tn=512):
    s, d = h.shape
    n = 2 * D_MODEL
    return _lagged_matmul(
        "proj_gate", _gate_epilogue, n_i=s // tm, n_j=n // tn,
        mm_args=[h, w_in],
        mm_specs=[((tm, d), lambda i, j: (i, 0)),
                  ((d, tn), lambda i, j: (0, COL_GATE // tn + j))],
        ep_args=[b_gate.reshape(1, n)],
        ep_specs=[((1, tn), lambda i, j: (0, j))],
        out_shapes=[jax.ShapeDtypeStruct((s, n), BF16)],
        out_specs=[((tm, tn), lambda i, j: (i, j))],
        scratch_shapes=[pltpu.VMEM((tm, tn), F32)])[0]


def _attn_a_kernel(q_ref, kc_ref, kp_ref, vc_ref, vp_ref, o_ref, lse_ref, kk_ref, vv_ref,
                   *, sub_len):
    w = WINDOW_STEPS
    t = q_ref.shape[1]
    c = pl.program_id(0)
    kk_ref[0:w] = kp_ref[0]
    kk_ref[w:] = kc_ref[0]
    vv_ref[0:w] = vp_ref[0]
    vv_ref[w:] = vc_ref[0]
    qp = lax.broadcasted_iota(jnp.int32, (w, 2 * w), 0)
    kp = lax.broadcasted_iota(jnp.int32, (w, 2 * w), 1)
    band = (kp >= qp) & (kp <= qp + w)
    band_start = band & (kp >= w)
    if t >= sub_len:
        masks = [band_start if (b * w) % sub_len == 0 else band for b in range(t // w)]
    else:
        is_start = (c % (sub_len // t)) == 0
        masks = [band & (kp >= jnp.where(is_start, w, 0))] + [band] * (t // w - 1)
    for b in range(t // w):
        q = q_ref[0, b * w:(b + 1) * w]
        k = kk_ref[b * w:(b + 2) * w]
        v = vv_ref[b * w:(b + 2) * w]
        s = lax.dot_general(q, k, (((1,), (1,)), ((), ())), preferred_element_type=F32)
        s = jnp.where(masks[b], s, -jnp.inf)
        m = jnp.max(s, axis=-1, keepdims=True)
        p = jnp.exp(s - m)
        l = jnp.sum(p, axis=-1, keepdims=True)
        o = jnp.dot(p.astype(v.dtype), v, preferred_element_type=F32)
        o_ref[b * w:(b + 1) * w] = o / l
        lse_ref[b * w:(b + 1) * w] = jnp.broadcast_to(m + jnp.log(l), (w, HEAD_DIM))


def _attn_a(qkv, t=2048):
    _, heads, dilation, sub_len, hd = qkv.shape
    rows = heads * dilation * sub_len
    flat = qkv.reshape(3, rows, hd)
    w = WINDOW_STEPS
    assert t % sub_len == 0 or sub_len % t == 0
    bpc = t // w
    cur = lambda sec: pl.BlockSpec((1, t, hd), lambda c: (sec, c, 0))
    prev = lambda sec: pl.BlockSpec((1, w, hd), lambda c: (sec, jnp.maximum(c * bpc - 1, 0), 0))
    out_spec = pl.BlockSpec((t, hd), lambda c: (c, 0))
    out, lse = pl.pallas_call(
        functools.partial(_attn_a_kernel, sub_len=sub_len),
        grid=(rows // t,),
        in_specs=[cur(0), cur(1), prev(1), cur(2), prev(2)],
        out_specs=[out_spec, out_spec],
        out_shape=[jax.ShapeDtypeStruct((rows, hd), F32)] * 2,
        scratch_shapes=[pltpu.VMEM((t + w, hd), BF16)] * 2,
        compiler_params=_params("parallel"),
        name=f"attn_a_r{dilation}",
    )(flat, flat, flat, flat, flat)
    shape = (heads, dilation, sub_len, hd)
    return out.reshape(shape), lse.reshape(shape)


def _mix_a_kernel(o0_ref, l0_ref, o1_ref, l1_ref, o2_ref, l2_ref, y_ref, nat_ref):
    tm = y_ref.shape[0]
    refs = ((o0_ref, l0_ref), (o1_ref, l1_ref), (o2_ref, l2_ref))
    r_max = DIL_PAIRS[-1][1]
    n = tm // r_max
    for hh in range(HEADS_PER_GROUP):
        for rho in range(r_max):
            outs, lses = [], []
            for (o_ref, l_ref), (_, r) in zip(refs, DIL_PAIRS):
                step = r_max // r
                rows = pl.ds(rho // r, n, stride=step) if step > 1 else pl.ds(0, n)
                outs.append(o_ref[hh, rho % r, rows, :])
                lses.append(l_ref[hh, rho % r, rows, :])
            m = jnp.maximum(jnp.maximum(lses[0], lses[1]), lses[2])
            ws = [jnp.exp(l - m) for l in lses]
            num = ws[0] * outs[0] + ws[1] * outs[1] + ws[2] * outs[2]
            den = ws[0] + ws[1] + ws[2]
            nat_ref[hh, pl.ds(rho, n, stride=r_max), :] = num / den
        y_ref[:, hh * HEAD_DIM:(hh + 1) * HEAD_DIM] = nat_ref[hh].astype(y_ref.dtype)


def _mix_a(group_outs, seq, tm=512):
    in_specs, args = [], []
    for (o, l), (_, r) in zip(group_outs, DIL_PAIRS):
        spec = pl.BlockSpec((HEADS_PER_GROUP, r, tm // r, HEAD_DIM), lambda i: (0, 0, i, 0))
        in_specs += [spec, spec]
        args += [o, l]
    return pl.pallas_call(
        _mix_a_kernel,
        grid=(seq // tm,),
        in_specs=in_specs,
        out_specs=pl.BlockSpec((tm, A_OUT), lambda i: (i, 0)),
        out_shape=jax.ShapeDtypeStruct((seq, A_OUT), BF16),
        scratch_shapes=[pltpu.VMEM((HEADS_PER_GROUP, tm, HEAD_DIM), F32)],
        compiler_params=_params("parallel"),
        name="mix_a",
    )(*args)


def _attn_b_kernel(lq1_ref, lk1_ref, lq2_ref, lk2_ref, g_ref, q_ref, k_ref, vt_ref, *rest,
                   lam_init, n_cast):
    w_refs = rest[:n_cast]
    o_ref = rest[n_cast]
    wo_refs = rest[n_cast + 1:2 * n_cast + 1]
    m_ref, acc_ref, st_ref, mb_ref = rest[2 * n_cast + 1:]
    for w_ref, wo_ref in zip(w_refs, wo_refs):
        wo_ref[...] = w_ref[...].astype(wo_ref.dtype)
    tq = q_ref.shape[1]
    tk = vt_ref.shape[3]
    width = vt_ref.shape[2] - ONES_ROWS
    qi = pl.program_id(1)
    m_ref[...] = jnp.full(m_ref.shape, -jnp.inf, F32)
    acc_ref[...] = jnp.zeros(acc_ref.shape, F32)

    def scores_to(slot, kv):
        start = pl.multiple_of(kv * tk, tk)
        for i in range(2):
            st = lax.dot_general(
                k_ref[i, pl.ds(start, tk), :], q_ref[i], (((1,), (1,)), ((), ())),
                preferred_element_type=F32)
            st_ref[slot, i] = st
            mb_ref[slot, i] = jnp.max(st, axis=0, keepdims=True)

    def softmax_pv(slot, kv, masked):
        vt = vt_ref[0, kv]
        for i in range(2):
            st = st_ref[slot, i]
            if masked:
                kpos = kv * tk + lax.broadcasted_iota(jnp.int32, st.shape, 0)
                qpos = qi * tq + lax.broadcasted_iota(jnp.int32, st.shape, 1)
                st = jnp.where(kpos <= qpos, st, -jnp.inf)
                m_blk = jnp.max(st, axis=0, keepdims=True)
            else:
                m_blk = mb_ref[slot, i]
            m_prev = m_ref[i]
            m_new = jnp.maximum(m_prev, m_blk)
            alpha = jnp.exp2(m_prev - m_new)
            p = jnp.exp2((st - m_new).astype(vt.dtype))
            acc_ref[i] = alpha * acc_ref[i] + jnp.dot(vt, p, preferred_element_type=F32)
            m_ref[i] = m_new

    scores_to(0, 0)

    def pair(j, carry):
        scores_to(1, 2 * j + 1)
        softmax_pv(0, 2 * j, masked=False)
        scores_to(0, 2 * j + 2)
        softmax_pv(1, 2 * j + 1, masked=False)
        return carry

    lax.fori_loop(0, qi // 2, pair, 0)

    @pl.when(qi % 2 == 0)
    def _():
        softmax_pv(0, qi, masked=True)

    @pl.when(qi % 2 == 1)
    def _():
        scores_to(1, qi)
        softmax_pv(0, qi - 1, masked=False)
        softmax_pv(1, qi, masked=True)

    lam = (jnp.exp(jnp.sum(lq1_ref[...] * lk1_ref[...], axis=-1, keepdims=True))
           - jnp.exp(jnp.sum(lq2_ref[...] * lk2_ref[...], axis=-1, keepdims=True))
           + lam_init)
    inv_l = [1.0 / acc_ref[i, width:width + 1, :] for i in range(2)]
    ot = acc_ref[0, :width, :] * inv_l[0] - lam * (acc_ref[1, :width, :] * inv_l[1])
    ms = jnp.mean(ot * ot, axis=0, keepdims=True)
    yt = (ot * lax.rsqrt(ms + EPS)) * g_ref[...]
    o_ref[...] = (yt * (1.0 - lam_init)).T.astype(o_ref.dtype)


def _attn_b(qk, vt, lam_vecs, subln_g, lam_init, cast_weights):
    _, s, hd = qk.shape
    _, n_kv, rows, tk = vt.shape
    width = rows - ONES_ROWS
    tq = tk
    n_q = s // tq
    n_steps = B_HEADS * n_q
    vec = pl.BlockSpec((1, hd), lambda h, i: (0, 0))
    w_specs = [pl.BlockSpec((w.shape[0] // n_steps, w.shape[1]), lambda h, i: (h * n_q + i, 0))
               for w in cast_weights]
    for w in cast_weights:
        assert w.shape[0] % (n_steps * BF16_SUBLANES) == 0, w.shape
    return pl.pallas_call(
        functools.partial(_attn_b_kernel, lam_init=lam_init, n_cast=len(cast_weights)),
        grid=(B_HEADS, n_q),
        in_specs=[vec, vec, vec, vec,
                  pl.BlockSpec((width, 1), lambda h, i: (0, 0)),
                  pl.BlockSpec((2, tq, hd), lambda h, i: (h, i, 0)),
                  pl.BlockSpec((2, s, hd), lambda h, i: (B_HEADS + h, 0, 0)),
                  pl.BlockSpec((1, n_kv, rows, tk), lambda h, i: (h, 0, 0, 0))] + w_specs,
        out_specs=[pl.BlockSpec((tq, width), lambda h, i: (i, h))] + w_specs,
        out_shape=[jax.ShapeDtypeStruct((s, B_HEADS * width), BF16)]
                  + [jax.ShapeDtypeStruct(w.shape, BF16) for w in cast_weights],
        scratch_shapes=[pltpu.VMEM((2, 1, tq), F32),
                        pltpu.VMEM((2, rows, tq), F32),
                        pltpu.VMEM((2, 2, tk, tq), F32),
                        pltpu.VMEM((2, 2, 1, tq), F32)],
        compiler_params=_params("arbitrary", "arbitrary"),
        name="attn_b",
    )(*[x.reshape(1, hd) for x in lam_vecs], subln_g.reshape(width, 1), qk, qk, vt,
      *cast_weights)


def _merge_kernel(ya_ref, yb_ref, wa_ref, wb_ref, ga_ref, gb_ref, o_ref):
    pa = jnp.dot(ya_ref[...], wa_ref[...].astype(BF16), preferred_element_type=F32)
    pb = jnp.dot(yb_ref[...], wb_ref[...], preferred_element_type=F32)
    o_ref[...] = (ga_ref[...].astype(F32) * pa + gb_ref[...].astype(F32) * pb).astype(o_ref.dtype)


def _merge(ya, yb, wa, wb, gates, tm=1024, tn=1024):
    s = ya.shape[0]
    n = wa.shape[1]
    return pl.pallas_call(
        _merge_kernel,
        grid=(s // tm, n // tn),
        in_specs=[pl.BlockSpec((tm, ya.shape[1]), lambda i, j: (i, 0)),
                  pl.BlockSpec((tm, yb.shape[1]), lambda i, j: (i, 0)),
                  pl.BlockSpec((wa.shape[0], tn), lambda i, j: (0, j)),
                  pl.BlockSpec((wb.shape[0], tn), lambda i, j: (0, j)),
                  pl.BlockSpec((tm, tn), lambda i, j: (i, j)),
                  pl.BlockSpec((tm, tn), lambda i, j: (i, n // tn + j))],
        out_specs=pl.BlockSpec((tm, tn), lambda i, j: (i, j)),
        out_shape=jax.ShapeDtypeStruct((s, n), BF16),
        compiler_params=_params("parallel", "arbitrary"),
        name="merge",
    )(ya, yb, wa, wb, gates, gates)


def _out_proj_kernel(a_ref, b_ref, x_ref, g_ref, x1_ref, xg_ref, r2_ref, *, n_cols):
    j = pl.program_id(1)
    x1 = x_ref[...] + jnp.dot(a_ref[...], b_ref[...], preferred_element_type=F32)
    x1_ref[...] = x1
    xg_ref[...] = (x1 * g_ref[...]).astype(xg_ref.dtype)
    part = jnp.broadcast_to(jnp.sum(x1 * x1, axis=-1, keepdims=True), r2_ref.shape)

    @pl.when(j == 0)
    def _():
        r2_ref[...] = part

    @pl.when(j > 0)
    def _():
        r2_ref[...] += part

    @pl.when(j == n_cols // x_ref.shape[1] - 1)
    def _():
        r2_ref[...] = 1.0 / (r2_ref[...] * (1.0 / n_cols) + EPS)


def _out_proj(a, b, x, g, tm=1024, tn=512):
    s, kdim = a.shape
    n = b.shape[1]
    lanes = HEAD_DIM
    return pl.pallas_call(
        functools.partial(_out_proj_kernel, n_cols=n),
        grid=(s // tm, n // tn),
        in_specs=[pl.BlockSpec((tm, kdim), lambda i, j: (i, 0)),
                  pl.BlockSpec((kdim, tn), lambda i, j: (0, j)),
                  pl.BlockSpec((tm, tn), lambda i, j: (i, j)),
                  pl.BlockSpec((1, tn), lambda i, j: (0, j))],
        out_specs=[pl.BlockSpec((tm, tn), lambda i, j: (i, j)),
                   pl.BlockSpec((tm, tn), lambda i, j: (i, j)),
                   pl.BlockSpec((tm, lanes), lambda i, j: (i, 0))],
        out_shape=[jax.ShapeDtypeStruct((s, n), F32),
                   jax.ShapeDtypeStruct((s, n), BF16),
                   jax.ShapeDtypeStruct((s, lanes), F32)],
        compiler_params=_params("parallel", "arbitrary"),
        name="out_proj",
    )(a, b, x, g.reshape(1, n))


def _ff1_kernel(xg_ref, r2_ref, b_ref, o_ref):
    lanes = r2_ref.shape[1]
    u = jnp.maximum(jnp.dot(xg_ref[...], b_ref[...], preferred_element_type=F32), 0.0)
    for c in range(o_ref.shape[1] // lanes):
        uc = u[:, c * lanes:(c + 1) * lanes]
        o_ref[:, c * lanes:(c + 1) * lanes] = (uc * uc * r2_ref[...]).astype(o_ref.dtype)


def _ff1(xg, r2, b, tm=1024, tn=1024):
    s, kdim = xg.shape
    n = b.shape[1]
    return pl.pallas_call(
        _ff1_kernel,
        grid=(s // tm, n // tn),
        in_specs=[pl.BlockSpec((tm, kdim), lambda i, j: (i, 0)),
                  pl.BlockSpec((tm, r2.shape[1]), lambda i, j: (i, 0)),
                  pl.BlockSpec((kdim, tn), lambda i, j: (0, j))],
        out_specs=pl.BlockSpec((tm, tn), lambda i, j: (i, j)),
        out_shape=jax.ShapeDtypeStruct((s, n), BF16),
        compiler_params=_params("parallel", "arbitrary"),
        name="ff1",
    )(xg, r2, b)


def _ff2_kernel(a_ref, b_ref, x_ref, o_ref):
    @pl.when(pl.program_id(2) == 0)
    def _():
        o_ref[...] = x_ref[...]

    o_ref[...] += jnp.dot(a_ref[...], b_ref[...], preferred_element_type=F32)


def _ff2(a, b, x, tm=1024, tn=1024, tk=2048):
    s, kdim = a.shape
    n = b.shape[1]
    return pl.pallas_call(
        _ff2_kernel,
        grid=(s // tm, n // tn, kdim // tk),
        in_specs=[pl.BlockSpec((tm, tk), lambda i, j, k: (i, k)),
                  pl.BlockSpec((tk, tn), lambda i, j, k: (k, j)),
                  pl.BlockSpec((tm, tn), lambda i, j, k: (i, j))],
        out_specs=pl.BlockSpec((tm, tn), lambda i, j, k: (i, j)),
        out_shape=jax.ShapeDtypeStruct((s, n), F32),
        compiler_params=_params("parallel", "parallel", "arbitrary"),
        name="ff2",
    )(a, b, x)


def _lambda_init(layer_idx):
    return 0.8 - 0.6 * math.exp(-0.3 * layer_idx)


def kernel(x, norm_mix, w_in, b_gate, w_proj_a, w_proj_b, w_out, lambda_q1, lambda_k1,
           lambda_q2, lambda_k2, subln_g, norm_mlp, w_ff1, w_ff2, norm_final):
    batch, seq, d = x.shape
    depth = w_in.shape[0]
    tables = _rope_tables(seq)
    outs = []
    for b in range(batch):
        xb = x.reshape(seq, d) if batch == 1 else x[b]
        for l in range(depth):
            w_in_l = w_in[l]
            h = _rmsnorm(xb, norm_mix[l], BF16)
            groups = []
            for g in range(N_GROUPS):
                groups.append(_attn_a(_proj_a(h, w_in_l, tables, g)))
            ya = _mix_a(groups, seq)
            qkb = _proj_bqk(h, w_in_l, tables)
            vb = _proj_bv(h, w_in_l)
            lam_init = _lambda_init(l)
            yb, wb_bf, wo_bf, w1_bf, w2_bf = _attn_b(
                qkb, vb, (lambda_q1[l], lambda_k1[l], lambda_q2[l], lambda_k2[l]),
                subln_g[l], lam_init, [w_proj_b[l], w_out[l], w_ff1[l], w_ff2[l]])
            gates = _proj_gate(h, w_in_l, b_gate[l])
            merged = _merge(ya, yb, w_proj_a[l], wb_bf, gates)
            xb, xg, r2 = _out_proj(merged, wo_bf, xb, norm_mlp[l])
            u = _ff1(xg, r2, w1_bf)
            xb = _ff2(u, w2_bf, xb)
        outs.append(_rmsnorm(xb, norm_final, x.dtype))
    if batch == 1:
        return outs[0].reshape(1, seq, d)
    return jnp.stack(outs, axis=0)
```

```python
import functools
import math

import jax
import jax.numpy as jnp
from jax import lax
from jax.experimental import pallas as pl
from jax.experimental.pallas import tpu as pltpu

D_MODEL = 4096
HEAD_DIM = 128
DIL_PAIRS = ((128, 1), (512, 4), (2048, 16))
N_GROUPS = len(DIL_PAIRS)
HEADS_PER_GROUP = 4
GROUP_COLS = HEADS_PER_GROUP * HEAD_DIM
A_WIDTH = N_GROUPS * GROUP_COLS
A_OUT = GROUP_COLS
B_HEADS = 8
B_QK = B_HEADS * 2 * HEAD_DIM
B_V = B_HEADS * 2 * HEAD_DIM
D_FF = 4 * D_MODEL
ROPE_THETA = 500000.0
ROT_DIM = HEAD_DIM // 4
ROT_HALF = ROT_DIM // 2
EPS = 1e-6
QK_SCALE = 1.0 / math.sqrt(HEAD_DIM)
LOG2_E = math.log2(math.e)
WINDOW_STEPS = 128
ONES_ROWS = 16
MXU_COLS = 256
BF16_SUBLANES = 16

COL_QA = 0
COL_KA = A_WIDTH
COL_VA = 2 * A_WIDTH
COL_QB = 3 * A_WIDTH
COL_KB = COL_QB + B_QK
COL_VB = COL_KB + B_QK
COL_GATE = COL_VB + B_V
IN_COLS = COL_GATE + 2 * D_MODEL

VMEM_LIMIT_BYTES = 56 * 1024 * 1024

F32 = jnp.float32
BF16 = jnp.bfloat16


def _params(*semantics):
    return pltpu.CompilerParams(dimension_semantics=semantics,
                                vmem_limit_bytes=VMEM_LIMIT_BYTES)


def _rmsnorm_kernel(x_ref, g_ref, o_ref):
    x = x_ref[...]
    ms = jnp.mean(x * x, axis=-1, keepdims=True)
    o_ref[...] = ((x * lax.rsqrt(ms + EPS)) * g_ref[...]).astype(o_ref.dtype)


def _rmsnorm(x, g, out_dtype, rows=256):
    s, d = x.shape
    return pl.pallas_call(
        _rmsnorm_kernel,
        grid=(s // rows,),
        in_specs=[pl.BlockSpec((rows, d), lambda i: (i, 0)),
                  pl.BlockSpec((1, d), lambda i: (0, 0))],
        out_specs=pl.BlockSpec((rows, d), lambda i: (i, 0)),
        out_shape=jax.ShapeDtypeStruct((s, d), out_dtype),
        compiler_params=_params("parallel"),
        name="rmsnorm",
    )(x, g.reshape(1, d))


def _rope_tables(seq):
    pos = jnp.arange(seq, dtype=F32)
    inv = ROPE_THETA ** (-jnp.arange(0, ROT_DIM, 2, dtype=F32) / ROT_DIM)
    ang = pos[:, None] * inv[None, :]
    cos, sin = jnp.cos(ang), jnp.sin(ang)
    ones = jnp.ones((seq, HEAD_DIM - ROT_DIM), F32)
    zeros = jnp.zeros((seq, HEAD_DIM - ROT_DIM), F32)
    zh = jnp.zeros((seq, ROT_HALF), F32)
    cos_t = jnp.concatenate([cos, cos, ones], axis=1)
    sin_lo = jnp.concatenate([-sin, zh, zeros], axis=1)
    sin_hi = jnp.concatenate([zh, sin, zeros], axis=1)
    rot = jnp.stack([cos_t, sin_lo, sin_hi])
    ident = jnp.stack([jnp.ones_like(cos_t), jnp.zeros_like(cos_t), jnp.zeros_like(cos_t)])
    return jnp.stack([rot, ident])


def _rope_head(y, tab_ref, scale):
    x2 = pltpu.roll(y, HEAD_DIM - ROT_HALF, 1)
    x1 = pltpu.roll(y, ROT_HALF, 1)
    return (y * tab_ref[0, 0] + x2 * tab_ref[0, 1] + x1 * tab_ref[0, 2]) * scale


def _store_product(a_ref, w_ref, acc_ref, *_):
    acc_ref[...] = jnp.dot(a_ref[...], w_ref[...].astype(BF16), preferred_element_type=F32)


def _lagged_kernel(*refs, n_mm, n_ep, n_out, n_tiles, n_j, matmul, epilogue):
    mm_refs = refs[:n_mm]
    ep_refs = refs[n_mm:n_mm + n_ep]
    out_refs = refs[n_mm + n_ep:n_mm + n_ep + n_out]
    scratch = refs[n_mm + n_ep + n_out:]
    t = pl.program_id(0)
    jp = jnp.maximum(t - 1, 0) % n_j

    @pl.when(t == 0)
    def _():
        matmul(*mm_refs, *scratch)

    @pl.when(jnp.logical_and(t > 0, t < n_tiles))
    def _():
        epilogue(jp, *ep_refs, *out_refs, *scratch)
        matmul(*mm_refs, *scratch)

    @pl.when(t == n_tiles)
    def _():
        epilogue(jp, *ep_refs, *out_refs, *scratch)


def _lagged_matmul(name, epilogue, *, n_i, n_j, mm_args, mm_specs, ep_args, ep_specs,
                   out_shapes, out_specs, scratch_shapes, matmul=_store_product):
    n_tiles = n_i * n_j

    def cur(f):
        return lambda t: f(*divmod(jnp.minimum(t, n_tiles - 1), n_j))

    def prev(f):
        return lambda t: f(*divmod(jnp.maximum(t - 1, 0), n_j))

    in_specs = ([pl.BlockSpec(shape, cur(f)) for shape, f in mm_specs]
                + [pl.BlockSpec(shape, prev(f)) for shape, f in ep_specs])
    return pl.pallas_call(
        functools.partial(_lagged_kernel, n_mm=len(mm_args), n_ep=len(ep_args),
                          n_out=len(out_shapes), n_tiles=n_tiles, n_j=n_j,
                          matmul=matmul, epilogue=epilogue),
        grid=(n_tiles + 1,),
        in_specs=in_specs,
        out_specs=[pl.BlockSpec(shape, prev(f)) for shape, f in out_specs],
        out_shape=out_shapes,
        scratch_shapes=scratch_shapes,
        compiler_params=_params("arbitrary"),
        name=name,
    )(*mm_args, *ep_args)


def _proj_a_epilogue(jp, tab_ref, o_ref, acc_ref, y_ref, *, dilation):
    tm = acc_ref.shape[0]
    n = tm // dilation
    scale = jnp.where(jp == 0, QK_SCALE, 1.0).astype(F32)
    for hh in range(HEADS_PER_GROUP):
        y = _rope_head(acc_ref[:, hh * HEAD_DIM:(hh + 1) * HEAD_DIM], tab_ref, scale)
        if dilation == 1:
            o_ref[0, hh, 0] = y.astype(o_ref.dtype)
            continue
        y_ref[hh] = y
        for rho in range(dilation):
            o_ref[0, hh, rho] = y_ref[hh, pl.ds(rho, n, stride=dilation), :].astype(o_ref.dtype)


def _proj_a(h, w_in, tables, group, tm=1024):
    s, d = h.shape
    dilation = DIL_PAIRS[group][1]
    sub_len = s // dilation
    blocks_per_section = A_WIDTH // GROUP_COLS
    return _lagged_matmul(
        f"proj_a_g{group}", functools.partial(_proj_a_epilogue, dilation=dilation),
        n_i=s // tm, n_j=3,
        mm_args=[h, w_in],
        mm_specs=[((tm, d), lambda i, j: (i, 0)),
                  ((d, GROUP_COLS), lambda i, j: (0, j * blocks_per_section + group))],
        ep_args=[tables],
        ep_specs=[((1, 3, tm, HEAD_DIM), lambda i, j: (j // 2, 0, i, 0))],
        out_shapes=[jax.ShapeDtypeStruct((3, HEADS_PER_GROUP, dilation, sub_len, HEAD_DIM), BF16)],
        out_specs=[((1, HEADS_PER_GROUP, dilation, tm // dilation, HEAD_DIM),
                    lambda i, j: (j, 0, 0, i, 0))],
        scratch_shapes=[pltpu.VMEM((tm, GROUP_COLS), F32),
                        pltpu.VMEM((HEADS_PER_GROUP, tm, HEAD_DIM), F32)])[0]


def _proj_bqk_epilogue(jp, tab_ref, o_ref, acc_ref, *, q_tiles):
    scale = jnp.where(jp < q_tiles, QK_SCALE * LOG2_E, 1.0).astype(F32)
    for hh in range(o_ref.shape[0]):
        y = _rope_head(acc_ref[:, hh * HEAD_DIM:(hh + 1) * HEAD_DIM], tab_ref, scale)
        o_ref[hh] = y.astype(o_ref.dtype)


def _proj_bqk(h, w_in, tables, tm=1024, tn=512):
    s, d = h.shape
    heads_per_tile = tn // HEAD_DIM
    n = B_QK + B_QK
    return _lagged_matmul(
        "proj_b_qk", functools.partial(_proj_bqk_epilogue, q_tiles=B_QK // tn),
        n_i=s // tm, n_j=n // tn,
        mm_args=[h, w_in],
        mm_specs=[((tm, d), lambda i, j: (i, 0)),
                  ((d, tn), lambda i, j: (0, COL_QB // tn + j))],
        ep_args=[tables],
        ep_specs=[((1, 3, tm, HEAD_DIM), lambda i, j: (0, 0, i, 0))],
        out_shapes=[jax.ShapeDtypeStruct((n // HEAD_DIM, s, HEAD_DIM), BF16)],
        out_specs=[((heads_per_tile, tm, HEAD_DIM), lambda i, j: (j, i, 0))],
        scratch_shapes=[pltpu.VMEM((tm, tn), F32)])[0]


def _proj_bv_kernel(h_ref, w_ref, o_ref):
    acc = jnp.dot(h_ref[...], w_ref[...].astype(BF16), preferred_element_type=F32)
    heads, blocks, rows, tk = o_ref.shape
    width = rows - ONES_ROWS
    for hv in range(heads):
        vt = acc[:, hv * width:(hv + 1) * width].T
        for c in range(blocks):
            o_ref[hv, c, :width, :] = vt[:, c * tk:(c + 1) * tk].astype(o_ref.dtype)
            o_ref[hv, c, width:, :] = jnp.ones((ONES_ROWS, tk), o_ref.dtype)


def _proj_bv(h, w_in, tm=1024, tn=512, tk=512):
    s, d = h.shape
    width = 2 * HEAD_DIM
    rows = width + ONES_ROWS
    return pl.pallas_call(
        _proj_bv_kernel,
        grid=(s // tm, B_V // tn),
        in_specs=[pl.BlockSpec((tm, d), lambda i, j: (i, 0)),
                  pl.BlockSpec((d, tn), lambda i, j: (0, COL_VB // tn + j))],
        out_specs=pl.BlockSpec((tn // width, tm // tk, rows, tk), lambda i, j: (j, i, 0, 0)),
        out_shape=jax.ShapeDtypeStruct((B_HEADS, s // tk, rows, tk), BF16),
        compiler_params=_params("parallel", "arbitrary"),
        name="proj_b_v",
    )(h, w_in)


def _gate_epilogue(jp, b_ref, o_ref, acc_ref):
    z = acc_ref[...] + b_ref[...]
    o_ref[...] = (0.5 * jnp.tanh(0.5 * z) + 0.5).astype(o_ref.dtype)


def _proj_gate(h, w_in, b_gate, tm=1024, tn=512):
    s, d = h.shape
    n = 2 * D_MODEL
    return _lagged_matmul(
        "proj_gate", _gate_epilogue, n_i=s // tm, n_j=n // tn,
        mm_args=[h, w_in],
        mm_specs=[((tm, d), lambda i, j: (i, 0)),
                  ((d, tn), lambda i, j: (0, COL_GATE // tn + j))],
        ep_args=[b_gate.reshape(1, n)],
        ep_specs=[((1, tn), lambda i, j: (0, j))],
        out_shapes=[jax.ShapeDtypeStruct((s, n), BF16)],
        out_specs=[((tm, tn), lambda i, j: (i, j))],
        scratch_shapes=[pltpu.VMEM((tm, tn), F32)])[0]


def _attn_a_kernel(q_ref, kc_ref, kp_ref, vc_ref, vp_ref, o_ref, lse_ref, kk_ref, vv_ref,
                   *, sub_len):
    w = WINDOW_STEPS
    t = q_ref.shape[1]
    c = pl.program_id(0)
    kk_ref[0:w] = kp_ref[0]
    kk_ref[w:] = kc_ref[0]
    vv_ref[0:w] = vp_ref[0]
    vv_ref[w:] = vc_ref[0]
    qp = lax.broadcasted_iota(jnp.int32, (w, 2 * w), 0)
    kp = lax.broadcasted_iota(jnp.int32, (w, 2 * w), 1)
    band = (kp >= qp) & (kp <= qp + w)
    band_start = band & (kp >= w)
    if t >= sub_len:
        masks = [band_start if (b * w) % sub_len == 0 else band for b in range(t // w)]
    else:
        is_start = (c % (sub_len // t)) == 0
        masks = [band & (kp >= jnp.where(is_start, w, 0))] + [band] * (t // w - 1)
    for b in range(t // w):
        q = q_ref[0, b * w:(b + 1) * w]
        k = kk_ref[b * w:(b + 2) * w]
        v = vv_ref[b * w:(b + 2) * w]
        s = lax.dot_general(q, k, (((1,), (1,)), ((), ())), preferred_element_type=F32)
        s = jnp.where(masks[b], s, -jnp.inf)
        m = jnp.max(s, axis=-1, keepdims=True)
        p = jnp.exp(s - m)
        l = jnp.sum(p, axis=-1, keepdims=True)
        o = jnp.dot(p.astype(v.dtype), v, preferred_element_type=F32)
        o_ref[b * w:(b + 1) * w] = o / l
        lse_ref[b * w:(b + 1) * w] = jnp.broadcast_to(m + jnp.log(l), (w, HEAD_DIM))


def _attn_a(qkv, t=2048):
    _, heads, dilation, sub_len, hd = qkv.shape
    rows = heads * dilation * sub_len
    flat = qkv.reshape(3, rows, hd)
    w = WINDOW_STEPS
    assert t % sub_len == 0 or sub_len % t == 0
    bpc = t // w
    cur = lambda sec: pl.BlockSpec((1, t, hd), lambda c: (sec, c, 0))
    prev = lambda sec: pl.BlockSpec((1, w, hd), lambda c: (sec, jnp.maximum(c * bpc - 1, 0), 0))
    out_spec = pl.BlockSpec((t, hd), lambda c: (c, 0))
    out, lse = pl.pallas_call(
        functools.partial(_attn_a_kernel, sub_len=sub_len),
        grid=(rows // t,),
        in_specs=[cur(0), cur(1), prev(1), cur(2), prev(2)],
        out_specs=[out_spec, out_spec],
        out_shape=[jax.ShapeDtypeStruct((rows, hd), F32)] * 2,
        scratch_shapes=[pltpu.VMEM((t + w, hd), BF16)] * 2,
        compiler_params=_params("parallel"),
        name=f"attn_a_r{dilation}",
    )(flat, flat, flat, flat, flat)
    shape = (heads, dilation, sub_len, hd)
    return out.reshape(shape), lse.reshape(shape)


def _mix_a_kernel(o0_ref, l0_ref, o1_ref, l1_ref, o2_ref, l2_ref, y_ref, nat_ref):
    tm = y_ref.shape[0]
    refs = ((o0_ref, l0_ref), (o1_ref, l1_ref), (o2_ref, l2_ref))
    r_max = DIL_PAIRS[-1][1]
    n = tm // r_max
    for hh in range(HEADS_PER_GROUP):
        for rho in range(r_max):
            outs, lses = [], []
            for (o_ref, l_ref), (_, r) in zip(refs, DIL_PAIRS):
                step = r_max // r
                rows = pl.ds(rho // r, n, stride=step) if step > 1 else pl.ds(0, n)
                outs.append(o_ref[hh, rho % r, rows, :])
                lses.append(l_ref[hh, rho % r, rows, :])
            m = jnp.maximum(jnp.maximum(lses[0], lses[1]), lses[2])
            ws = [jnp.exp(l - m) for l in lses]
            num = ws[0] * outs[0] + ws[1] * outs[1] + ws[2] * outs[2]
            den = ws[0] + ws[1] + ws[2]
            nat_ref[hh, pl.ds(rho, n, stride=r_max), :] = num / den
        y_ref[:, hh * HEAD_DIM:(hh + 1) * HEAD_DIM] = nat_ref[hh].astype(y_ref.dtype)


def _mix_a(group_outs, seq, tm=512):
    in_specs, args = [], []
    for (o, l), (_, r) in zip(group_outs, DIL_PAIRS):
        spec = pl.BlockSpec((HEADS_PER_GROUP, r, tm // r, HEAD_DIM), lambda i: (0, 0, i, 0))
        in_specs += [spec, spec]
        args += [o, l]
    return pl.pallas_call(
        _mix_a_kernel,
        grid=(seq // tm,),
        in_specs=in_specs,
        out_specs=pl.BlockSpec((tm, A_OUT), lambda i: (i, 0)),
        out_shape=jax.ShapeDtypeStruct((seq, A_OUT), BF16),
        scratch_shapes=[pltpu.VMEM((HEADS_PER_GROUP, tm, HEAD_DIM), F32)],
        compiler_params=_params("parallel"),
        name="mix_a",
    )(*args)


def _attn_b_kernel(lq1_ref, lk1_ref, lq2_ref, lk2_ref, g_ref, q_ref, k_ref, vt_ref, *rest,
                   lam_init, n_cast):
    w_refs = rest[:n_cast]
    o_ref = rest[n_cast]
    wo_refs = rest[n_cast + 1:2 * n_cast + 1]
    m_ref, acc_ref, st_ref, mb_ref = rest[2 * n_cast + 1:]
    for w_ref, wo_ref in zip(w_refs, wo_refs):
        wo_ref[...] = w_ref[...].astype(wo_ref.dtype)
    tq = q_ref.shape[1]
    tk = vt_ref.shape[3]
    width = vt_ref.shape[2] - ONES_ROWS
    qi = pl.program_id(1)
    m_ref[...] = jnp.full(m_ref.shape, -jnp.inf, F32)
    acc_ref[...] = jnp.zeros(acc_ref.shape, F32)

    def scores_to(slot, kv):
        start = pl.multiple_of(kv * tk, tk)
        for i in range(2):
            st = lax.dot_general(
                k_ref[i, pl.ds(start, tk), :], q_ref[i], (((1,), (1,)), ((), ())),
                preferred_element_type=F32)
            st_ref[slot, i] = st
            mb_ref[slot, i] = jnp.max(st, axis=0, keepdims=True)

    def softmax_pv(slot, kv, masked):
        vt = vt_ref[0, kv]
        for i in range(2):
            st = st_ref[slot, i]
            if masked:
                kpos = kv * tk + lax.broadcasted_iota(jnp.int32, st.shape, 0)
                qpos = qi * tq + lax.broadcasted_iota(jnp.int32, st.shape, 1)
                st = jnp.where(kpos <= qpos, st, -jnp.inf)
                m_blk = jnp.max(st, axis=0, keepdims=True)
            else:
                m_blk = mb_ref[slot, i]
            m_prev = m_ref[i]
            m_new = jnp.maximum(m_prev, m_blk)
            alpha = jnp.exp2(m_prev - m_new)
            p = jnp.exp2((st - m_new).astype(vt.dtype))
            acc_ref[i] = alpha * acc_ref[i] + jnp.dot(vt, p, preferred_element_type=F32)
            m_ref[i] = m_new

    scores_to(0, 0)

    def pair(j, carry):
        scores_to(1, 2 * j + 1)
        softmax_pv(0, 2 * j, masked=False)
        scores_to(0, 2 * j + 2)
        softmax_pv(1, 2 * j + 1, masked=False)
        return carry

    lax.fori_loop(0, qi // 2, pair, 0)

    @pl.when(qi % 2 == 0)
    def _():
        softmax_pv(0, qi, masked=True)

    @pl.when(qi % 2 == 1)
    def _():
        scores_to(1, qi)
        softmax_pv(0, qi - 1, masked=False)
        softmax_pv(1, qi, masked=True)

    lam = (jnp.exp(jnp.sum(lq1_ref[...] * lk1_ref[...], axis=-1, keepdims=True))
           - jnp.exp(jnp.sum(lq2_ref[...] * lk2_ref[...], axis=-1, keepdims=True))
           + lam_init)
    inv_l = [1.0 / acc_ref[i, width:width + 1, :] for i in range(2)]
    ot = acc_ref[0, :width, :] * inv_l[0] - lam * (acc_ref[1, :width, :] * inv_l[1])
    ms = jnp.mean(ot * ot, axis=0, keepdims=True)
    yt = (ot * lax.rsqrt(ms + EPS)) * g_ref[...]
    o_ref[...] = (yt * (1.0 - lam_init)).T.astype(o_ref.dtype)


def _attn_b(qk, vt, lam_vecs, subln_g, lam_init, cast_weights):
    _, s, hd = qk.shape
    _, n_kv, rows, tk = vt.shape
    width = rows - ONES_ROWS
    tq = tk
    n_q = s // tq
    n_steps = B_HEADS * n_q
    vec = pl.BlockSpec((1, hd), lambda h, i: (0, 0))
    w_specs = [pl.BlockSpec((w.shape[0] // n_steps, w.shape[1]), lambda h, i: (h * n_q + i, 0))
               for w in cast_weights]
    for w in cast_weights:
        assert w.shape[0] % (n_steps * BF16_SUBLANES) == 0, w.shape
    return pl.pallas_call(
        functools.partial(_attn_b_kernel, lam_init=lam_init, n_cast=len(cast_weights)),
        grid=(B_HEADS, n_q),
        in_specs=[vec, vec, vec, vec,
                  pl.BlockSpec((width, 1), lambda h, i: (0, 0)),
                  pl.BlockSpec((2, tq, hd), lambda h, i: (h, i, 0)),
                  pl.BlockSpec((2, s, hd), lambda h, i: (B_HEADS + h, 0, 0)),
                  pl.BlockSpec((1, n_kv, rows, tk), lambda h, i: (h, 0, 0, 0))] + w_specs,
        out_specs=[pl.BlockSpec((tq, width), lambda h, i: (i, h))] + w_specs,
        out_shape=[jax.ShapeDtypeStruct((s, B_HEADS * width), BF16)]
                  + [jax.ShapeDtypeStruct(w.shape, BF16) for w in cast_weights],
        scratch_shapes=[pltpu.VMEM((2, 1, tq), F32),
                        pltpu.VMEM((2, rows, tq), F32),
                        pltpu.VMEM((2, 2, tk, tq), F32),
                        pltpu.VMEM((2, 2, 1, tq), F32)],
        compiler_params=_params("arbitrary", "arbitrary"),
        name="attn_b",
    )(*[x.reshape(1, hd) for x in lam_vecs], subln_g.reshape(width, 1), qk, qk, vt,
      *cast_weights)


def _merge_kernel(ya_ref, yb_ref, wa_ref, wb_ref, ga_ref, gb_ref, o_ref):
    pa = jnp.dot(ya_ref[...], wa_ref[...].astype(BF16), preferred_element_type=F32)
    pb = jnp.dot(yb_ref[...], wb_ref[...], preferred_element_type=F32)
    o_ref[...] = (ga_ref[...].astype(F32) * pa + gb_ref[...].astype(F32) * pb).astype(o_ref.dtype)


def _merge(ya, yb, wa, wb, gates, tm=1024, tn=1024):
    s = ya.shape[0]
    n = wa.shape[1]
    return pl.pallas_call(
        _merge_kernel,
        grid=(s // tm, n // tn),
        in_specs=[pl.BlockSpec((tm, ya.shape[1]), lambda i, j: (i, 0)),
                  pl.BlockSpec((tm, yb.shape[1]), lambda i, j: (i, 0)),
                  pl.BlockSpec((wa.shape[0], tn), lambda i, j: (0, j)),
                  pl.BlockSpec((wb.shape[0], tn), lambda i, j: (0, j)),
                  pl.BlockSpec((tm, tn), lambda i, j: (i, j)),
                  pl.BlockSpec((tm, tn), lambda i, j: (i, n // tn + j))],
        out_specs=pl.BlockSpec((tm, tn), lambda i, j: (i, j)),
        out_shape=jax.ShapeDtypeStruct((s, n), BF16),
        compiler_params=_params("parallel", "arbitrary"),
        name="merge",
    )(ya, yb, wa, wb, gates, gates)


def _out_proj_kernel(a_ref, b_ref, x_ref, g_ref, x1_ref, xg_ref, r2_ref, *, n_cols):
    j = pl.program_id(1)
    x1 = x_ref[...] + jnp.dot(a_ref[...], b_ref[...], preferred_element_type=F32)
    x1_ref[...] = x1
    xg_ref[...] = (x1 * g_ref[...]).astype(xg_ref.dtype)
    part = jnp.broadcast_to(jnp.sum(x1 * x1, axis=-1, keepdims=True), r2_ref.shape)

    @pl.when(j == 0)
    def _():
        r2_ref[...] = part

    @pl.when(j > 0)
    def _():
        r2_ref[...] += part

    @pl.when(j == n_cols // x_ref.shape[1] - 1)
    def _():
        r2_ref[...] = 1.0 / (r2_ref[...] * (1.0 / n_cols) + EPS)


def _out_proj(a, b, x, g, tm=1024, tn=512):
    s, kdim = a.shape
    n = b.shape[1]
    lanes = HEAD_DIM
    return pl.pallas_call(
        functools.partial(_out_proj_kernel, n_cols=n),
        grid=(s // tm, n // tn),
        in_specs=[pl.BlockSpec((tm, kdim), lambda i, j: (i, 0)),
                  pl.BlockSpec((kdim, tn), lambda i, j: (0, j)),
                  pl.BlockSpec((tm, tn), lambda i, j: (i, j)),
                  pl.BlockSpec((1, tn), lambda i, j: (0, j))],
        out_specs=[pl.BlockSpec((tm, tn), lambda i, j: (i, j)),
                   pl.BlockSpec((tm, tn), lambda i, j: (i, j)),
                   pl.BlockSpec((tm, lanes), lambda i, j: (i, 0))],
        out_shape=[jax.ShapeDtypeStruct((s, n), F32),
                   jax.ShapeDtypeStruct((s, n), BF16),
                   jax.ShapeDtypeStruct((s, lanes), F32)],
        compiler_params=_params("parallel", "arbitrary"),
        name="out_proj",
    )(a, b, x, g.reshape(1, n))


def _ff1_kernel(xg_ref, r2_ref, b_ref, o_ref):
    lanes = r2_ref.shape[1]
    u = jnp.maximum(jnp.dot(xg_ref[...], b_ref[...], preferred_element_type=F32), 0.0)
    for c in range(o_ref.shape[1] // lanes):
        uc = u[:, c * lanes:(c + 1) * lanes]
        o_ref[:, c * lanes:(c + 1) * lanes] = (uc * uc * r2_ref[...]).astype(o_ref.dtype)


def _ff1(xg, r2, b, tm=1024, tn=1024):
    s, kdim = xg.shape
    n = b.shape[1]
    return pl.pallas_call(
        _ff1_kernel,
        grid=(s // tm, n // tn),
        in_specs=[pl.BlockSpec((tm, kdim), lambda i, j: (i, 0)),
                  pl.BlockSpec((tm, r2.shape[1]), lambda i, j: (i, 0)),
                  pl.BlockSpec((kdim, tn), lambda i, j: (0, j))],
        out_specs=pl.BlockSpec((tm, tn), lambda i, j: (i, j)),
        out_shape=jax.ShapeDtypeStruct((s, n), BF16),
        compiler_params=_params("parallel", "arbitrary"),
        name="ff1",
    )(xg, r2, b)


def _ff2_kernel(a_ref, b_ref, x_ref, o_ref):
    @pl.when(pl.program_id(2) == 0)
    def _():
        o_ref[...] = x_ref[...]

    o_ref[...] += jnp.dot(a_ref[...], b_ref[...], preferred_element_type=F32)


def _ff2(a, b, x, tm=1024, tn=1024, tk=4096):
    s, kdim = a.shape
    n = b.shape[1]
    return pl.pallas_call(
        _ff2_kernel,
        grid=(s // tm, n // tn, kdim // tk),
        in_specs=[pl.BlockSpec((tm, tk), lambda i, j, k: (i, k)),
                  pl.BlockSpec((tk, tn), lambda i, j, k: (k, j)),
                  pl.BlockSpec((tm, tn), lambda i, j, k: (i, j))],
        out_specs=pl.BlockSpec((tm, tn), lambda i, j, k: (i, j)),
        out_shape=jax.ShapeDtypeStruct((s, n), F32),
        compiler_params=_params("parallel", "parallel", "arbitrary"),
        name="ff2",
    )(a, b, x)


def _lambda_init(layer_idx):
    return 0.8 - 0.6 * math.exp(-0.3 * layer_idx)


def kernel(x, norm_mix, w_in, b_gate, w_proj_a, w_proj_b, w_out, lambda_q1, lambda_k1,
           lambda_q2, lambda_k2, subln_g, norm_mlp, w_ff1, w_ff2, norm_final):
    batch, seq, d = x.shape
    depth = w_in.shape[0]
    tables = _rope_tables(seq)
    outs = []
    for b in range(batch):
        xb = x.reshape(seq, d) if batch == 1 else x[b]
        for l in range(depth):
            w_in_l = w_in[l]
            h = _rmsnorm(xb, norm_mix[l], BF16)
            groups = []
            for g in range(N_GROUPS):
                groups.append(_attn_a(_proj_a(h, w_in_l, tables, g)))
            ya = _mix_a(groups, seq)
            qkb = _proj_bqk(h, w_in_l, tables)
            vb = _proj_bv(h, w_in_l)
            lam_init = _lambda_init(l)
            yb, wb_bf, wo_bf, w1_bf, w2_bf = _attn_b(
                qkb, vb, (lambda_q1[l], lambda_k1[l], lambda_q2[l], lambda_k2[l]),
                subln_g[l], lam_init, [w_proj_b[l], w_out[l], w_ff1[l], w_ff2[l]])
            gates = _proj_gate(h, w_in_l, b_gate[l])
            merged = _merge(ya, yb, w_proj_a[l], wb_bf, gates)
            xb, xg, r2 = _out_proj(merged, wo_bf, xb, norm_mlp[l])
            u = _ff1(xg, r2, w1_bf)
            xb = _ff2(u, w2_bf, xb)
        outs.append(_rmsnorm(xb, norm_final, x.dtype))
    if batch == 1:
        return outs[0].reshape(1, seq, d)
    return jnp.stack(outs, axis=0)
```

```python
import functools
import math

import jax
import jax.numpy as jnp
from jax import lax
from jax.experimental import pallas as pl
from jax.experimental.pallas import tpu as pltpu

D_MODEL = 4096
HEAD_DIM = 128
DIL_PAIRS = ((128, 1), (512, 4), (2048, 16))
N_GROUPS = len(DIL_PAIRS)
HEADS_PER_GROUP = 4
GROUP_COLS = HEADS_PER_GROUP * HEAD_DIM
A_WIDTH = N_GROUPS * GROUP_COLS
A_OUT = GROUP_COLS
B_HEADS = 8
B_QK = B_HEADS * 2 * HEAD_DIM
B_V = B_HEADS * 2 * HEAD_DIM
D_FF = 4 * D_MODEL
ROPE_THETA = 500000.0
ROT_DIM = HEAD_DIM // 4
ROT_HALF = ROT_DIM // 2
EPS = 1e-6
QK_SCALE = 1.0 / math.sqrt(HEAD_DIM)
LOG2_E = math.log2(math.e)
WINDOW_STEPS = 128
ONES_ROWS = 16
MXU_COLS = 256
BF16_SUBLANES = 16

COL_QA = 0
COL_KA = A_WIDTH
COL_VA = 2 * A_WIDTH
COL_QB = 3 * A_WIDTH
COL_KB = COL_QB + B_QK
COL_VB = COL_KB + B_QK
COL_GATE = COL_VB + B_V
IN_COLS = COL_GATE + 2 * D_MODEL

VMEM_LIMIT_BYTES = 56 * 1024 * 1024

F32 = jnp.float32
BF16 = jnp.bfloat16


def _params(*semantics):
    return pltpu.CompilerParams(dimension_semantics=semantics,
                                vmem_limit_bytes=VMEM_LIMIT_BYTES)


def _rmsnorm_kernel(x_ref, g_ref, o_ref):
    x = x_ref[...]
    ms = jnp.mean(x * x, axis=-1, keepdims=True)
    o_ref[...] = ((x * lax.rsqrt(ms + EPS)) * g_ref[...]).astype(o_ref.dtype)


def _rmsnorm(x, g, out_dtype, rows=256):
    s, d = x.shape
    return pl.pallas_call(
        _rmsnorm_kernel,
        grid=(s // rows,),
        in_specs=[pl.BlockSpec((rows, d), lambda i: (i, 0)),
                  pl.BlockSpec((1, d), lambda i: (0, 0))],
        out_specs=pl.BlockSpec((rows, d), lambda i: (i, 0)),
        out_shape=jax.ShapeDtypeStruct((s, d), out_dtype),
        compiler_params=_params("parallel"),
        name="rmsnorm",
    )(x, g.reshape(1, d))


def _rope_tables(seq):
    pos = jnp.arange(seq, dtype=F32)
    inv = ROPE_THETA ** (-jnp.arange(0, ROT_DIM, 2, dtype=F32) / ROT_DIM)
    ang = pos[:, None] * inv[None, :]
    cos, sin = jnp.cos(ang), jnp.sin(ang)
    ones = jnp.ones((seq, HEAD_DIM - ROT_DIM), F32)
    zeros = jnp.zeros((seq, HEAD_DIM - ROT_DIM), F32)
    zh = jnp.zeros((seq, ROT_HALF), F32)
    cos_t = jnp.concatenate([cos, cos, ones], axis=1)
    sin_lo = jnp.concatenate([-sin, zh, zeros], axis=1)
    sin_hi = jnp.concatenate([zh, sin, zeros], axis=1)
    rot = jnp.stack([cos_t, sin_lo, sin_hi])
    ident = jnp.stack([jnp.ones_like(cos_t), jnp.zeros_like(cos_t), jnp.zeros_like(cos_t)])
    return jnp.stack([rot, ident])


def _rope_head(y, tab_ref, scale):
    x2 = pltpu.roll(y, HEAD_DIM - ROT_HALF, 1)
    x1 = pltpu.roll(y, ROT_HALF, 1)
    return (y * tab_ref[0, 0] + x2 * tab_ref[0, 1] + x1 * tab_ref[0, 2]) * scale


def _store_product(a_ref, w_ref, acc_ref, *_):
    acc_ref[...] = jnp.dot(a_ref[...], w_ref[...].astype(BF16), preferred_element_type=F32)


def _lagged_kernel(*refs, n_mm, n_ep, n_out, n_tiles, n_j, matmul, epilogue):
    mm_refs = refs[:n_mm]
    ep_refs = refs[n_mm:n_mm + n_ep]
    out_refs = refs[n_mm + n_ep:n_mm + n_ep + n_out]
    scratch = refs[n_mm + n_ep + n_out:]
    t = pl.program_id(0)
    jp = jnp.maximum(t - 1, 0) % n_j

    @pl.when(t == 0)
    def _():
        matmul(*mm_refs, *scratch)

    @pl.when(jnp.logical_and(t > 0, t < n_tiles))
    def _():
        epilogue(jp, *ep_refs, *out_refs, *scratch)
        matmul(*mm_refs, *scratch)

    @pl.when(t == n_tiles)
    def _():
        epilogue(jp, *ep_refs, *out_refs, *scratch)


def _lagged_matmul(name, epilogue, *, n_i, n_j, mm_args, mm_specs, ep_args, ep_specs,
                   out_shapes, out_specs, scratch_shapes, matmul=_store_product):
    n_tiles = n_i * n_j

    def cur(f):
        return lambda t: f(*divmod(jnp.minimum(t, n_tiles - 1), n_j))

    def prev(f):
        return lambda t: f(*divmod(jnp.maximum(t - 1, 0), n_j))

    in_specs = ([pl.BlockSpec(shape, cur(f)) for shape, f in mm_specs]
                + [pl.BlockSpec(shape, prev(f)) for shape, f in ep_specs])
    return pl.pallas_call(
        functools.partial(_lagged_kernel, n_mm=len(mm_args), n_ep=len(ep_args),
                          n_out=len(out_shapes), n_tiles=n_tiles, n_j=n_j,
                          matmul=matmul, epilogue=epilogue),
        grid=(n_tiles + 1,),
        in_specs=in_specs,
        out_specs=[pl.BlockSpec(shape, prev(f)) for shape, f in out_specs],
        out_shape=out_shapes,
        scratch_shapes=scratch_shapes,
        compiler_params=_params("arbitrary"),
        name=name,
    )(*mm_args, *ep_args)


def _proj_a_epilogue(jp, tab_ref, o_ref, acc_ref, y_ref, *, dilation):
    tm = acc_ref.shape[0]
    n = tm // dilation
    scale = jnp.where(jp == 0, QK_SCALE, 1.0).astype(F32)
    for hh in range(HEADS_PER_GROUP):
        y = _rope_head(acc_ref[:, hh * HEAD_DIM:(hh + 1) * HEAD_DIM], tab_ref, scale)
        if dilation == 1:
            o_ref[0, hh, 0] = y.astype(o_ref.dtype)
            continue
        y_ref[hh] = y
        for rho in range(dilation):
            o_ref[0, hh, rho] = y_ref[hh, pl.ds(rho, n, stride=dilation), :].astype(o_ref.dtype)


def _proj_a(h, w_in, tables, group, tm=1024):
    s, d = h.shape
    dilation = DIL_PAIRS[group][1]
    sub_len = s // dilation
    blocks_per_section = A_WIDTH // GROUP_COLS
    return _lagged_matmul(
        f"proj_a_g{group}", functools.partial(_proj_a_epilogue, dilation=dilation),
        n_i=s // tm, n_j=3,
        mm_args=[h, w_in],
        mm_specs=[((tm, d), lambda i, j: (i, 0)),
                  ((d, GROUP_COLS), lambda i, j: (0, j * blocks_per_section + group))],
        ep_args=[tables],
        ep_specs=[((1, 3, tm, HEAD_DIM), lambda i, j: (j // 2, 0, i, 0))],
        out_shapes=[jax.ShapeDtypeStruct((3, HEADS_PER_GROUP, dilation, sub_len, HEAD_DIM), BF16)],
        out_specs=[((1, HEADS_PER_GROUP, dilation, tm // dilation, HEAD_DIM),
                    lambda i, j: (j, 0, 0, i, 0))],
        scratch_shapes=[pltpu.VMEM((tm, GROUP_COLS), F32),
                        pltpu.VMEM((HEADS_PER_GROUP, tm, HEAD_DIM), F32)])[0]


def _proj_bqk_epilogue(jp, tab_ref, o_ref, acc_ref, *, q_tiles):
    scale = jnp.where(jp < q_tiles, QK_SCALE * LOG2_E, 1.0).astype(F32)
    for hh in range(o_ref.shape[0]):
        y = _rope_head(acc_ref[:, hh * HEAD_DIM:(hh + 1) * HEAD_DIM], tab_ref, scale)
        o_ref[hh] = y.astype(o_ref.dtype)


def _proj_bqk(h, w_in, tables, tm=1024, tn=512):
    s, d = h.shape
    heads_per_tile = tn // HEAD_DIM
    n = B_QK + B_QK
    return _lagged_matmul(
        "proj_b_qk", functools.partial(_proj_bqk_epilogue, q_tiles=B_QK // tn),
        n_i=s // tm, n_j=n // tn,
        mm_args=[h, w_in],
        mm_specs=[((tm, d), lambda i, j: (i, 0)),
                  ((d, tn), lambda i, j: (0, COL_QB // tn + j))],
        ep_args=[tables],
        ep_specs=[((1, 3, tm, HEAD_DIM), lambda i, j: (0, 0, i, 0))],
        out_shapes=[jax.ShapeDtypeStruct((n // HEAD_DIM, s, HEAD_DIM), BF16)],
        out_specs=[((heads_per_tile, tm, HEAD_DIM), lambda i, j: (j, i, 0))],
        scratch_shapes=[pltpu.VMEM((tm, tn), F32)])[0]


def _proj_bv_kernel(h_ref, w_ref, o_ref):
    acc = jnp.dot(h_ref[...], w_ref[...].astype(BF16), preferred_element_type=F32)
    heads, blocks, rows, tk = o_ref.shape
    width = rows - ONES_ROWS
    for hv in range(heads):
        vt = acc[:, hv * width:(hv + 1) * width].T
        for c in range(blocks):
            o_ref[hv, c, :width, :] = vt[:, c * tk:(c + 1) * tk].astype(o_ref.dtype)
            o_ref[hv, c, width:, :] = jnp.ones((ONES_ROWS, tk), o_ref.dtype)


def _proj_bv(h, w_in, tm=1024, tn=512, tk=512):
    s, d = h.shape
    width = 2 * HEAD_DIM
    rows = width + ONES_ROWS
    return pl.pallas_call(
        _proj_bv_kernel,
        grid=(s // tm, B_V // tn),
        in_specs=[pl.BlockSpec((tm, d), lambda i, j: (i, 0)),
                  pl.BlockSpec((d, tn), lambda i, j: (0, COL_VB // tn + j))],
        out_specs=pl.BlockSpec((tn // width, tm // tk, rows, tk), lambda i, j: (j, i, 0, 0)),
        out_shape=jax.ShapeDtypeStruct((B_HEADS, s // tk, rows, tk), BF16),
        compiler_params=_params("parallel", "arbitrary"),
        name="proj_b_v",
    )(h, w_in)


def _proj_gate_kernel(h_ref, w_ref, b_ref, o_ref):
    z = jnp.dot(h_ref[...], w_ref[...], preferred_element_type=F32) + b_ref[...]
    o_ref[...] = (0.5 * jnp.tanh(0.5 * z) + 0.5).astype(o_ref.dtype)


def _proj_gate(h, w_gate, b_gate, tm=1024, tn=1024):
    s, d = h.shape
    n = w_gate.shape[1]
    return pl.pallas_call(
        _proj_gate_kernel,
        grid=(s // tm, n // tn),
        in_specs=[pl.BlockSpec((tm, d), lambda i, j: (i, 0)),
                  pl.BlockSpec((d, tn), lambda i, j: (0, j)),
                  pl.BlockSpec((1, tn), lambda i, j: (0, j))],
        out_specs=pl.BlockSpec((tm, tn), lambda i, j: (i, j)),
        out_shape=jax.ShapeDtypeStruct((s, n), BF16),
        compiler_params=_params("parallel", "arbitrary"),
        name="proj_gate",
    )(h, w_gate, b_gate.reshape(1, n))


def _attn_a_kernel(q_ref, kc_ref, kp_ref, vc_ref, vp_ref, o_ref, lse_ref, kk_ref, vv_ref,
                   *, sub_len):
    w = WINDOW_STEPS
    t = q_ref.shape[1]
    c = pl.program_id(0)
    kk_ref[0:w] = kp_ref[0]
    kk_ref[w:] = kc_ref[0]
    vv_ref[0:w] = vp_ref[0]
    vv_ref[w:] = vc_ref[0]
    qp = lax.broadcasted_iota(jnp.int32, (w, 2 * w), 0)
    kp = lax.broadcasted_iota(jnp.int32, (w, 2 * w), 1)
    band = (kp >= qp) & (kp <= qp + w)
    band_start = band & (kp >= w)
    if t >= sub_len:
        masks = [band_start if (b * w) % sub_len == 0 else band for b in range(t // w)]
    else:
        is_start = (c % (sub_len // t)) == 0
        masks = [band & (kp >= jnp.where(is_start, w, 0))] + [band] * (t // w - 1)
    for b in range(t // w):
        q = q_ref[0, b * w:(b + 1) * w]
        k = kk_ref[b * w:(b + 2) * w]
        v = vv_ref[b * w:(b + 2) * w]
        s = lax.dot_general(q, k, (((1,), (1,)), ((), ())), preferred_element_type=F32)
        s = jnp.where(masks[b], s, -jnp.inf)
        m = jnp.max(s, axis=-1, keepdims=True)
        p = jnp.exp(s - m)
        l = jnp.sum(p, axis=-1, keepdims=True)
        o = jnp.dot(p.astype(v.dtype), v, preferred_element_type=F32)
        o_ref[b * w:(b + 1) * w] = o / l
        lse_ref[b * w:(b + 1) * w] = jnp.broadcast_to(m + jnp.log(l), (w, HEAD_DIM))


def _attn_a(qkv, t=2048):
    _, heads, dilation, sub_len, hd = qkv.shape
    rows = heads * dilation * sub_len
    flat = qkv.reshape(3, rows, hd)
    w = WINDOW_STEPS
    assert t % sub_len == 0 or sub_len % t == 0
    bpc = t // w
    cur = lambda sec: pl.BlockSpec((1, t, hd), lambda c: (sec, c, 0))
    prev = lambda sec: pl.BlockSpec((1, w, hd), lambda c: (sec, jnp.maximum(c * bpc - 1, 0), 0))
    out_spec = pl.BlockSpec((t, hd), lambda c: (c, 0))
    out, lse = pl.pallas_call(
        functools.partial(_attn_a_kernel, sub_len=sub_len),
        grid=(rows // t,),
        in_specs=[cur(0), cur(1), prev(1), cur(2), prev(2)],
        out_specs=[out_spec, out_spec],
        out_shape=[jax.ShapeDtypeStruct((rows, hd), F32)] * 2,
        scratch_shapes=[pltpu.VMEM((t + w, hd), BF16)] * 2,
        compiler_params=_params("parallel"),
        name=f"attn_a_r{dilation}",
    )(flat, flat, flat, flat, flat)
    shape = (heads, dilation, sub_len, hd)
    return out.reshape(shape), lse.reshape(shape)


def _mix_a_kernel(o0_ref, l0_ref, o1_ref, l1_ref, o2_ref, l2_ref, y_ref, nat_ref):
    tm = y_ref.shape[0]
    refs = ((o0_ref, l0_ref), (o1_ref, l1_ref), (o2_ref, l2_ref))
    r_max = DIL_PAIRS[-1][1]
    n = tm // r_max
    for hh in range(HEADS_PER_GROUP):
        for rho in range(r_max):
            outs, lses = [], []
            for (o_ref, l_ref), (_, r) in zip(refs, DIL_PAIRS):
                step = r_max // r
                rows = pl.ds(rho // r, n, stride=step) if step > 1 else pl.ds(0, n)
                outs.append(o_ref[hh, rho % r, rows, :])
                lses.append(l_ref[hh, rho % r, rows, :])
            m = jnp.maximum(jnp.maximum(lses[0], lses[1]), lses[2])
            ws = [jnp.exp(l - m) for l in lses]
            num = ws[0] * outs[0] + ws[1] * outs[1] + ws[2] * outs[2]
            den = ws[0] + ws[1] + ws[2]
            nat_ref[hh, pl.ds(rho, n, stride=r_max), :] = num / den
        y_ref[:, hh * HEAD_DIM:(hh + 1) * HEAD_DIM] = nat_ref[hh].astype(y_ref.dtype)


def _mix_a(group_outs, seq, tm=512):
    in_specs, args = [], []
    for (o, l), (_, r) in zip(group_outs, DIL_PAIRS):
        spec = pl.BlockSpec((HEADS_PER_GROUP, r, tm // r, HEAD_DIM), lambda i: (0, 0, i, 0))
        in_specs += [spec, spec]
        args += [o, l]
    return pl.pallas_call(
        _mix_a_kernel,
        grid=(seq // tm,),
        in_specs=in_specs,
        out_specs=pl.BlockSpec((tm, A_OUT), lambda i: (i, 0)),
        out_shape=jax.ShapeDtypeStruct((seq, A_OUT), BF16),
        scratch_shapes=[pltpu.VMEM((HEADS_PER_GROUP, tm, HEAD_DIM), F32)],
        compiler_params=_params("parallel"),
        name="mix_a",
    )(*args)


def _attn_b_kernel(lq1_ref, lk1_ref, lq2_ref, lk2_ref, g_ref, q_ref, k_ref, vt_ref, *rest,
                   lam_init, cast_cols):
    n_in = len(cast_cols)
    n_out = sum(len(cols) for cols in cast_cols)
    w_refs = rest[:n_in]
    o_ref = rest[n_in]
    wo_refs = rest[n_in + 1:n_in + 1 + n_out]
    m_ref, acc_ref, st_ref, mb_ref = rest[n_in + 1 + n_out:]
    wo_iter = iter(wo_refs)
    for w_ref, cols in zip(w_refs, cast_cols):
        for start, size in cols:
            wo_ref = next(wo_iter)
            wo_ref[...] = w_ref[:, start:start + size].astype(wo_ref.dtype)
    tq = q_ref.shape[1]
    tk = vt_ref.shape[3]
    width = vt_ref.shape[2] - ONES_ROWS
    qi = pl.program_id(1)
    m_ref[...] = jnp.full(m_ref.shape, -jnp.inf, F32)
    acc_ref[...] = jnp.zeros(acc_ref.shape, F32)

    def scores_to(slot, kv):
        start = pl.multiple_of(kv * tk, tk)
        for i in range(2):
            st = lax.dot_general(
                k_ref[i, pl.ds(start, tk), :], q_ref[i], (((1,), (1,)), ((), ())),
                preferred_element_type=F32)
            st_ref[slot, i] = st
            mb_ref[slot, i] = jnp.max(st, axis=0, keepdims=True)

    def softmax_pv(slot, kv, masked):
        vt = vt_ref[0, kv]
        for i in range(2):
            st = st_ref[slot, i]
            if masked:
                kpos = kv * tk + lax.broadcasted_iota(jnp.int32, st.shape, 0)
                qpos = qi * tq + lax.broadcasted_iota(jnp.int32, st.shape, 1)
                st = jnp.where(kpos <= qpos, st, -jnp.inf)
                m_blk = jnp.max(st, axis=0, keepdims=True)
            else:
                m_blk = mb_ref[slot, i]
            m_prev = m_ref[i]
            m_new = jnp.maximum(m_prev, m_blk)
            alpha = jnp.exp2(m_prev - m_new)
            p = jnp.exp2((st - m_new).astype(vt.dtype))
            acc_ref[i] = alpha * acc_ref[i] + jnp.dot(vt, p, preferred_element_type=F32)
            m_ref[i] = m_new

    scores_to(0, 0)

    def pair(j, carry):
        scores_to(1, 2 * j + 1)
        softmax_pv(0, 2 * j, masked=False)
        scores_to(0, 2 * j + 2)
        softmax_pv(1, 2 * j + 1, masked=False)
        return carry

    lax.fori_loop(0, qi // 2, pair, 0)

    @pl.when(qi % 2 == 0)
    def _():
        softmax_pv(0, qi, masked=True)

    @pl.when(qi % 2 == 1)
    def _():
        scores_to(1, qi)
        softmax_pv(0, qi - 1, masked=False)
        softmax_pv(1, qi, masked=True)

    lam = (jnp.exp(jnp.sum(lq1_ref[...] * lk1_ref[...], axis=-1, keepdims=True))
           - jnp.exp(jnp.sum(lq2_ref[...] * lk2_ref[...], axis=-1, keepdims=True))
           + lam_init)
    inv_l = [1.0 / acc_ref[i, width:width + 1, :] for i in range(2)]
    ot = acc_ref[0, :width, :] * inv_l[0] - lam * (acc_ref[1, :width, :] * inv_l[1])
    ms = jnp.mean(ot * ot, axis=0, keepdims=True)
    yt = (ot * lax.rsqrt(ms + EPS)) * g_ref[...]
    o_ref[...] = (yt * (1.0 - lam_init)).T.astype(o_ref.dtype)


def _attn_b(qk, vt, lam_vecs, subln_g, lam_init, cast_weights):
    _, s, hd = qk.shape
    _, n_kv, rows, tk = vt.shape
    width = rows - ONES_ROWS
    tq = tk
    n_q = s // tq
    n_steps = B_HEADS * n_q
    vec = pl.BlockSpec((1, hd), lambda h, i: (0, 0))
    chunk = lambda h, i: (h * n_q + i, 0)
    w_specs, wo_specs, wo_shapes = [], [], []
    for w, cols in cast_weights:
        assert w.shape[0] % (n_steps * BF16_SUBLANES) == 0, w.shape
        rows_per_step = w.shape[0] // n_steps
        w_specs.append(pl.BlockSpec((rows_per_step, w.shape[1]), chunk))
        for _, size in cols:
            wo_specs.append(pl.BlockSpec((rows_per_step, size), chunk))
            wo_shapes.append(jax.ShapeDtypeStruct((w.shape[0], size), BF16))
    return pl.pallas_call(
        functools.partial(_attn_b_kernel, lam_init=lam_init,
                          cast_cols=tuple(tuple(cols) for _, cols in cast_weights)),
        grid=(B_HEADS, n_q),
        in_specs=[vec, vec, vec, vec,
                  pl.BlockSpec((width, 1), lambda h, i: (0, 0)),
                  pl.BlockSpec((2, tq, hd), lambda h, i: (h, i, 0)),
                  pl.BlockSpec((2, s, hd), lambda h, i: (B_HEADS + h, 0, 0)),
                  pl.BlockSpec((1, n_kv, rows, tk), lambda h, i: (h, 0, 0, 0))] + w_specs,
        out_specs=[pl.BlockSpec((tq, width), lambda h, i: (i, h))] + wo_specs,
        out_shape=[jax.ShapeDtypeStruct((s, B_HEADS * width), BF16)] + wo_shapes,
        scratch_shapes=[pltpu.VMEM((2, 1, tq), F32),
                        pltpu.VMEM((2, rows, tq), F32),
                        pltpu.VMEM((2, 2, tk, tq), F32),
                        pltpu.VMEM((2, 2, 1, tq), F32)],
        compiler_params=_params("arbitrary", "arbitrary"),
        name="attn_b",
    )(*[x.reshape(1, hd) for x in lam_vecs], subln_g.reshape(width, 1), qk, qk, vt,
      *[w for w, _ in cast_weights])


def _merge_kernel(ya_ref, yb_ref, wa_ref, wb_ref, ga_ref, gb_ref, o_ref):
    pa = jnp.dot(ya_ref[...], wa_ref[...].astype(BF16), preferred_element_type=F32)
    pb = jnp.dot(yb_ref[...], wb_ref[...], preferred_element_type=F32)
    o_ref[...] = (ga_ref[...].astype(F32) * pa + gb_ref[...].astype(F32) * pb).astype(o_ref.dtype)


def _merge(ya, yb, wa, wb, gates, tm=1024, tn=1024):
    s = ya.shape[0]
    n = wa.shape[1]
    return pl.pallas_call(
        _merge_kernel,
        grid=(s // tm, n // tn),
        in_specs=[pl.BlockSpec((tm, ya.shape[1]), lambda i, j: (i, 0)),
                  pl.BlockSpec((tm, yb.shape[1]), lambda i, j: (i, 0)),
                  pl.BlockSpec((wa.shape[0], tn), lambda i, j: (0, j)),
                  pl.BlockSpec((wb.shape[0], tn), lambda i, j: (0, j)),
                  pl.BlockSpec((tm, tn), lambda i, j: (i, j)),
                  pl.BlockSpec((tm, tn), lambda i, j: (i, n // tn + j))],
        out_specs=pl.BlockSpec((tm, tn), lambda i, j: (i, j)),
        out_shape=jax.ShapeDtypeStruct((s, n), BF16),
        compiler_params=_params("parallel", "arbitrary"),
        name="merge",
    )(ya, yb, wa, wb, gates, gates)


def _out_proj_kernel(a_ref, b_ref, x_ref, g_ref, x1_ref, xg_ref, r2_ref, *, n_cols):
    j = pl.program_id(1)
    x1 = x_ref[...] + jnp.dot(a_ref[...], b_ref[...], preferred_element_type=F32)
    x1_ref[...] = x1
    xg_ref[...] = (x1 * g_ref[...]).astype(xg_ref.dtype)
    part = jnp.broadcast_to(jnp.sum(x1 * x1, axis=-1, keepdims=True), r2_ref.shape)

    @pl.when(j == 0)
    def _():
        r2_ref[...] = part

    @pl.when(j > 0)
    def _():
        r2_ref[...] += part

    @pl.when(j == n_cols // x_ref.shape[1] - 1)
    def _():
        r2_ref[...] = 1.0 / (r2_ref[...] * (1.0 / n_cols) + EPS)


def _out_proj(a, b, x, g, tm=1024, tn=512):
    s, kdim = a.shape
    n = b.shape[1]
    lanes = HEAD_DIM
    return pl.pallas_call(
        functools.partial(_out_proj_kernel, n_cols=n),
        grid=(s // tm, n // tn),
        in_specs=[pl.BlockSpec((tm, kdim), lambda i, j: (i, 0)),
                  pl.BlockSpec((kdim, tn), lambda i, j: (0, j)),
                  pl.BlockSpec((tm, tn), lambda i, j: (i, j)),
                  pl.BlockSpec((1, tn), lambda i, j: (0, j))],
        out_specs=[pl.BlockSpec((tm, tn), lambda i, j: (i, j)),
                   pl.BlockSpec((tm, tn), lambda i, j: (i, j)),
                   pl.BlockSpec((tm, lanes), lambda i, j: (i, 0))],
        out_shape=[jax.ShapeDtypeStruct((s, n), F32),
                   jax.ShapeDtypeStruct((s, n), BF16),
                   jax.ShapeDtypeStruct((s, lanes), F32)],
        compiler_params=_params("parallel", "arbitrary"),
        name="out_proj",
    )(a, b, x, g.reshape(1, n))


def _ff1_kernel(xg_ref, r2_ref, b_ref, o_ref):
    lanes = r2_ref.shape[1]
    u = jnp.maximum(jnp.dot(xg_ref[...], b_ref[...], preferred_element_type=F32), 0.0)
    for c in range(o_ref.shape[1] // lanes):
        uc = u[:, c * lanes:(c + 1) * lanes]
        o_ref[:, c * lanes:(c + 1) * lanes] = (uc * uc * r2_ref[...]).astype(o_ref.dtype)


def _ff1(xg, r2, b, tm=1024, tn=1024):
    s, kdim = xg.shape
    n = b.shape[1]
    return pl.pallas_call(
        _ff1_kernel,
        grid=(s // tm, n // tn),
        in_specs=[pl.BlockSpec((tm, kdim), lambda i, j: (i, 0)),
                  pl.BlockSpec((tm, r2.shape[1]), lambda i, j: (i, 0)),
                  pl.BlockSpec((kdim, tn), lambda i, j: (0, j))],
        out_specs=pl.BlockSpec((tm, tn), lambda i, j: (i, j)),
        out_shape=jax.ShapeDtypeStruct((s, n), BF16),
        compiler_params=_params("parallel", "arbitrary"),
        name="ff1",
    )(xg, r2, b)


def _ff2_kernel(a_ref, b_ref, x_ref, o_ref):
    @pl.when(pl.program_id(2) == 0)
    def _():
        o_ref[...] = x_ref[...]

    o_ref[...] += jnp.dot(a_ref[...], b_ref[...], preferred_element_type=F32)


def _ff2(a, b, x, tm=1024, tn=1024, tk=4096):
    s, kdim = a.shape
    n = b.shape[1]
    return pl.pallas_call(
        _ff2_kernel,
        grid=(s // tm, n // tn, kdim // tk),
        in_specs=[pl.BlockSpec((tm, tk), lambda i, j, k: (i, k)),
                  pl.BlockSpec((tk, tn), lambda i, j, k: (k, j)),
                  pl.BlockSpec((tm, tn), lambda i, j, k: (i, j))],
        out_specs=pl.BlockSpec((tm, tn), lambda i, j, k: (i, j)),
        out_shape=jax.ShapeDtypeStruct((s, n), F32),
        compiler_params=_params("parallel", "parallel", "arbitrary"),
        name="ff2",
    )(a, b, x)


def _lambda_init(layer_idx):
    return 0.8 - 0.6 * math.exp(-0.3 * layer_idx)


def kernel(x, norm_mix, w_in, b_gate, w_proj_a, w_proj_b, w_out, lambda_q1, lambda_k1,
           lambda_q2, lambda_k2, subln_g, norm_mlp, w_ff1, w_ff2, norm_final):
    batch, seq, d = x.shape
    depth = w_in.shape[0]
    tables = _rope_tables(seq)
    outs = []
    for b in range(batch):
        xb = x.reshape(seq, d) if batch == 1 else x[b]
        for l in range(depth):
            w_in_l = w_in[l]
            h = _rmsnorm(xb, norm_mix[l], BF16)
            qkb = _proj_bqk(h, w_in_l, tables)
            vb = _proj_bv(h, w_in_l)
            lam_init = _lambda_init(l)
            whole = lambda w: (w, [(0, w.shape[1])])
            yb, wa_in_bf, wg_bf, wb_bf, wo_bf, w1_bf, w2_bf = _attn_b(
                qkb, vb, (lambda_q1[l], lambda_k1[l], lambda_q2[l], lambda_k2[l]),
                subln_g[l], lam_init,
                [(w_in_l, [(COL_QA, 3 * A_WIDTH), (COL_GATE, 2 * D_MODEL)]),
                 whole(w_proj_b[l]), whole(w_out[l]), whole(w_ff1[l]), whole(w_ff2[l])])
            groups = []
            for g in range(N_GROUPS):
                groups.append(_attn_a(_proj_a(h, wa_in_bf, tables, g)))
            ya = _mix_a(groups, seq)
            gates = _proj_gate(h, wg_bf, b_gate[l])
            merged = _merge(ya, yb, w_proj_a[l], wb_bf, gates)
            xb, xg, r2 = _out_proj(merged, wo_bf, xb, norm_mlp[l])
            u = _ff1(xg, r2, w1_bf)
            xb = _ff2(u, w2_bf, xb)
        outs.append(_rmsnorm(xb, norm_final, x.dtype))
    if batch == 1:
        return outs[0].reshape(1, seq, d)
    return jnp.stack(outs, axis=0)
```

```python
import functools
import math

import jax
import jax.numpy as jnp
from jax import lax
from jax.experimental import pallas as pl
from jax.experimental.pallas import tpu as pltpu

D_MODEL = 4096
HEAD_DIM = 128
DIL_PAIRS = ((128, 1), (512, 4), (2048, 16))
N_GROUPS = len(DIL_PAIRS)
HEADS_PER_GROUP = 4
GROUP_COLS = HEADS_PER_GROUP * HEAD_DIM
A_WIDTH = N_GROUPS * GROUP_COLS
A_OUT = GROUP_COLS
B_HEADS = 8
B_QK = B_HEADS * 2 * HEAD_DIM
B_V = B_HEADS * 2 * HEAD_DIM
D_FF = 4 * D_MODEL
ROPE_THETA = 500000.0
ROT_DIM = HEAD_DIM // 4
ROT_HALF = ROT_DIM // 2
EPS = 1e-6
QK_SCALE = 1.0 / math.sqrt(HEAD_DIM)
LOG2_E = math.log2(math.e)
WINDOW_STEPS = 128
ONES_ROWS = 16
MXU_COLS = 256
BF16_SUBLANES = 16

COL_QA = 0
COL_KA = A_WIDTH
COL_VA = 2 * A_WIDTH
COL_QB = 3 * A_WIDTH
COL_KB = COL_QB + B_QK
COL_VB = COL_KB + B_QK
COL_GATE = COL_VB + B_V
IN_COLS = COL_GATE + 2 * D_MODEL

VMEM_LIMIT_BYTES = 56 * 1024 * 1024

F32 = jnp.float32
BF16 = jnp.bfloat16


def _params(*semantics):
    return pltpu.CompilerParams(dimension_semantics=semantics,
                                vmem_limit_bytes=VMEM_LIMIT_BYTES)


def _rmsnorm_kernel(x_ref, g_ref, o_ref):
    x = x_ref[...]
    ms = jnp.mean(x * x, axis=-1, keepdims=True)
    o_ref[...] = ((x * lax.rsqrt(ms + EPS)) * g_ref[...]).astype(o_ref.dtype)


def _rmsnorm(x, g, out_dtype, rows=256):
    s, d = x.shape
    return pl.pallas_call(
        _rmsnorm_kernel,
        grid=(s // rows,),
        in_specs=[pl.BlockSpec((rows, d), lambda i: (i, 0)),
                  pl.BlockSpec((1, d), lambda i: (0, 0))],
        out_specs=pl.BlockSpec((rows, d), lambda i: (i, 0)),
        out_shape=jax.ShapeDtypeStruct((s, d), out_dtype),
        compiler_params=_params("parallel"),
        name="rmsnorm",
    )(x, g.reshape(1, d))


def _rope_tables(seq):
    pos = jnp.arange(seq, dtype=F32)
    inv = ROPE_THETA ** (-jnp.arange(0, ROT_DIM, 2, dtype=F32) / ROT_DIM)
    ang = pos[:, None] * inv[None, :]
    cos, sin = jnp.cos(ang), jnp.sin(ang)
    ones = jnp.ones((seq, HEAD_DIM - ROT_DIM), F32)
    zeros = jnp.zeros((seq, HEAD_DIM - ROT_DIM), F32)
    zh = jnp.zeros((seq, ROT_HALF), F32)
    cos_t = jnp.concatenate([cos, cos, ones], axis=1)
    sin_lo = jnp.concatenate([-sin, zh, zeros], axis=1)
    sin_hi = jnp.concatenate([zh, sin, zeros], axis=1)
    rot = jnp.stack([cos_t, sin_lo, sin_hi])
    ident = jnp.stack([jnp.ones_like(cos_t), jnp.zeros_like(cos_t), jnp.zeros_like(cos_t)])
    return jnp.stack([rot, ident])


def _rope_head(y, tab_ref, scale):
    x2 = pltpu.roll(y, HEAD_DIM - ROT_HALF, 1)
    x1 = pltpu.roll(y, ROT_HALF, 1)
    return (y * tab_ref[0, 0] + x2 * tab_ref[0, 1] + x1 * tab_ref[0, 2]) * scale


def _store_product(a_ref, w_ref, acc_ref, *_):
    acc_ref[...] = jnp.dot(a_ref[...], w_ref[...].astype(BF16), preferred_element_type=F32)


def _lagged_kernel(*refs, n_mm, n_ep, n_out, n_tiles, n_j, matmul, epilogue):
    mm_refs = refs[:n_mm]
    ep_refs = refs[n_mm:n_mm + n_ep]
    out_refs = refs[n_mm + n_ep:n_mm + n_ep + n_out]
    scratch = refs[n_mm + n_ep + n_out:]
    t = pl.program_id(0)
    jp = jnp.maximum(t - 1, 0) % n_j

    @pl.when(t == 0)
    def _():
        matmul(*mm_refs, *scratch)

    @pl.when(jnp.logical_and(t > 0, t < n_tiles))
    def _():
        epilogue(jp, *ep_refs, *out_refs, *scratch)
        matmul(*mm_refs, *scratch)

    @pl.when(t == n_tiles)
    def _():
        epilogue(jp, *ep_refs, *out_refs, *scratch)


def _lagged_matmul(name, epilogue, *, n_i, n_j, mm_args, mm_specs, ep_args, ep_specs,
                   out_shapes, out_specs, scratch_shapes, matmul=_store_product):
    n_tiles = n_i * n_j

    def cur(f):
        return lambda t: f(*divmod(jnp.minimum(t, n_tiles - 1), n_j))

    def prev(f):
        return lambda t: f(*divmod(jnp.maximum(t - 1, 0), n_j))

    in_specs = ([pl.BlockSpec(shape, cur(f)) for shape, f in mm_specs]
                + [pl.BlockSpec(shape, prev(f)) for shape, f in ep_specs])
    return pl.pallas_call(
        functools.partial(_lagged_kernel, n_mm=len(mm_args), n_ep=len(ep_args),
                          n_out=len(out_shapes), n_tiles=n_tiles, n_j=n_j,
                          matmul=matmul, epilogue=epilogue),
        grid=(n_tiles + 1,),
        in_specs=in_specs,
        out_specs=[pl.BlockSpec(shape, prev(f)) for shape, f in out_specs],
        out_shape=out_shapes,
        scratch_shapes=scratch_shapes,
        compiler_params=_params("arbitrary"),
        name=name,
    )(*mm_args, *ep_args)


def _proj_a_epilogue(jp, tab_ref, o_ref, acc_ref, y_ref, *, dilation):
    tm = acc_ref.shape[0]
    n = tm // dilation
    scale = jnp.where(jp == 0, QK_SCALE, 1.0).astype(F32)
    for hh in range(HEADS_PER_GROUP):
        y = _rope_head(acc_ref[:, hh * HEAD_DIM:(hh + 1) * HEAD_DIM], tab_ref, scale)
        if dilation == 1:
            o_ref[0, hh, 0] = y.astype(o_ref.dtype)
            continue
        y_ref[hh] = y
        for rho in range(dilation):
            o_ref[0, hh, rho] = y_ref[hh, pl.ds(rho, n, stride=dilation), :].astype(o_ref.dtype)


def _proj_a(h, w_in, tables, group, tm=1024):
    s, d = h.shape
    dilation = DIL_PAIRS[group][1]
    sub_len = s // dilation
    blocks_per_section = A_WIDTH // GROUP_COLS
    return _lagged_matmul(
        f"proj_a_g{group}", functools.partial(_proj_a_epilogue, dilation=dilation),
        n_i=s // tm, n_j=3,
        mm_args=[h, w_in],
        mm_specs=[((tm, d), lambda i, j: (i, 0)),
                  ((d, GROUP_COLS), lambda i, j: (0, j * blocks_per_section + group))],
        ep_args=[tables],
        ep_specs=[((1, 3, tm, HEAD_DIM), lambda i, j: (j // 2, 0, i, 0))],
        out_shapes=[jax.ShapeDtypeStruct((3, HEADS_PER_GROUP, dilation, sub_len, HEAD_DIM), BF16)],
        out_specs=[((1, HEADS_PER_GROUP, dilation, tm // dilation, HEAD_DIM),
                    lambda i, j: (j, 0, 0, i, 0))],
        scratch_shapes=[pltpu.VMEM((tm, GROUP_COLS), F32),
                        pltpu.VMEM((HEADS_PER_GROUP, tm, HEAD_DIM), F32)])[0]


def _proj_bqk_epilogue(jp, tab_ref, o_ref, acc_ref, *, q_tiles):
    scale = jnp.where(jp < q_tiles, QK_SCALE * LOG2_E, 1.0).astype(F32)
    for hh in range(o_ref.shape[0]):
        y = _rope_head(acc_ref[:, hh * HEAD_DIM:(hh + 1) * HEAD_DIM], tab_ref, scale)
        o_ref[hh] = y.astype(o_ref.dtype)


def _proj_bqk(h, w_in, tables, tm=1024, tn=512):
    s, d = h.shape
    heads_per_tile = tn // HEAD_DIM
    n = B_QK + B_QK
    return _lagged_matmul(
        "proj_b_qk", functools.partial(_proj_bqk_epilogue, q_tiles=B_QK // tn),
        n_i=s // tm, n_j=n // tn,
        mm_args=[h, w_in],
        mm_specs=[((tm, d), lambda i, j: (i, 0)),
                  ((d, tn), lambda i, j: (0, COL_QB // tn + j))],
        ep_args=[tables],
        ep_specs=[((1, 3, tm, HEAD_DIM), lambda i, j: (0, 0, i, 0))],
        out_shapes=[jax.ShapeDtypeStruct((n // HEAD_DIM, s, HEAD_DIM), BF16)],
        out_specs=[((heads_per_tile, tm, HEAD_DIM), lambda i, j: (j, i, 0))],
        scratch_shapes=[pltpu.VMEM((tm, tn), F32)])[0]


def _proj_bv_kernel(h_ref, w_ref, o_ref):
    acc = jnp.dot(h_ref[...], w_ref[...].astype(BF16), preferred_element_type=F32)
    heads, blocks, rows, tk = o_ref.shape
    width = rows - ONES_ROWS
    for hv in range(heads):
        vt = acc[:, hv * width:(hv + 1) * width].T
        for c in range(blocks):
            o_ref[hv, c, :width, :] = vt[:, c * tk:(c + 1) * tk].astype(o_ref.dtype)
            o_ref[hv, c, width:, :] = jnp.ones((ONES_ROWS, tk), o_ref.dtype)


def _proj_bv(h, w_in, tm=1024, tn=512, tk=512):
    s, d = h.shape
    width = 2 * HEAD_DIM
    rows = width + ONES_ROWS
    return pl.pallas_call(
        _proj_bv_kernel,
        grid=(s // tm, B_V // tn),
        in_specs=[pl.BlockSpec((tm, d), lambda i, j: (i, 0)),
                  pl.BlockSpec((d, tn), lambda i, j: (0, COL_VB // tn + j))],
        out_specs=pl.BlockSpec((tn // width, tm // tk, rows, tk), lambda i, j: (j, i, 0, 0)),
        out_shape=jax.ShapeDtypeStruct((B_HEADS, s // tk, rows, tk), BF16),
        compiler_params=_params("parallel", "arbitrary"),
        name="proj_b_v",
    )(h, w_in)


def _proj_gate_kernel(h_ref, w_ref, b_ref, o_ref):
    z = jnp.dot(h_ref[...], w_ref[...], preferred_element_type=F32) + b_ref[...]
    o_ref[...] = (0.5 * jnp.tanh(0.5 * z) + 0.5).astype(o_ref.dtype)


def _proj_gate(h, w_gate, b_gate, tm=1024, tn=1024):
    s, d = h.shape
    n = w_gate.shape[1]
    return pl.pallas_call(
        _proj_gate_kernel,
        grid=(s // tm, n // tn),
        in_specs=[pl.BlockSpec((tm, d), lambda i, j: (i, 0)),
                  pl.BlockSpec((d, tn), lambda i, j: (0, j)),
                  pl.BlockSpec((1, tn), lambda i, j: (0, j))],
        out_specs=pl.BlockSpec((tm, tn), lambda i, j: (i, j)),
        out_shape=jax.ShapeDtypeStruct((s, n), BF16),
        compiler_params=_params("parallel", "arbitrary"),
        name="proj_gate",
    )(h, w_gate, b_gate.reshape(1, n))


def _attn_a_kernel(q_ref, kc_ref, kp_ref, vc_ref, vp_ref, o_ref, lse_ref, kk_ref, vv_ref,
                   *, sub_len):
    w = WINDOW_STEPS
    t = q_ref.shape[1]
    c = pl.program_id(0)
    kk_ref[0:w] = kp_ref[0]
    kk_ref[w:] = kc_ref[0]
    vv_ref[0:w] = vp_ref[0]
    vv_ref[w:] = vc_ref[0]
    qp = lax.broadcasted_iota(jnp.int32, (w, 2 * w), 0)
    kp = lax.broadcasted_iota(jnp.int32, (w, 2 * w), 1)
    band = (kp >= qp) & (kp <= qp + w)
    band_start = band & (kp >= w)
    if t >= sub_len:
        masks = [band_start if (b * w) % sub_len == 0 else band for b in range(t // w)]
    else:
        is_start = (c % (sub_len // t)) == 0
        masks = [band & (kp >= jnp.where(is_start, w, 0))] + [band] * (t // w - 1)
    for b in range(t // w):
        q = q_ref[0, b * w:(b + 1) * w]
        k = kk_ref[b * w:(b + 2) * w]
        v = vv_ref[b * w:(b + 2) * w]
        s = lax.dot_general(q, k, (((1,), (1,)), ((), ())), preferred_element_type=F32)
        s = jnp.where(masks[b], s, -jnp.inf)
        m = jnp.max(s, axis=-1, keepdims=True)
        p = jnp.exp(s - m)
        l = jnp.sum(p, axis=-1, keepdims=True)
        o = jnp.dot(p.astype(v.dtype), v, preferred_element_type=F32)
        o_ref[b * w:(b + 1) * w] = o / l
        lse_ref[b * w:(b + 1) * w] = jnp.broadcast_to(m + jnp.log(l), (w, HEAD_DIM))


def _attn_a(qkv, t=2048):
    _, heads, dilation, sub_len, hd = qkv.shape
    rows = heads * dilation * sub_len
    flat = qkv.reshape(3, rows, hd)
    w = WINDOW_STEPS
    assert t % sub_len == 0 or sub_len % t == 0
    bpc = t // w
    cur = lambda sec: pl.BlockSpec((1, t, hd), lambda c: (sec, c, 0))
    prev = lambda sec: pl.BlockSpec((1, w, hd), lambda c: (sec, jnp.maximum(c * bpc - 1, 0), 0))
    out_spec = pl.BlockSpec((t, hd), lambda c: (c, 0))
    out, lse = pl.pallas_call(
        functools.partial(_attn_a_kernel, sub_len=sub_len),
        grid=(rows // t,),
        in_specs=[cur(0), cur(1), prev(1), cur(2), prev(2)],
        out_specs=[out_spec, out_spec],
        out_shape=[jax.ShapeDtypeStruct((rows, hd), F32)] * 2,
        scratch_shapes=[pltpu.VMEM((t + w, hd), BF16)] * 2,
        compiler_params=_params("parallel"),
        name=f"attn_a_r{dilation}",
    )(flat, flat, flat, flat, flat)
    shape = (heads, dilation, sub_len, hd)
    return out.reshape(shape), lse.reshape(shape)


def _mix_a_kernel(o0_ref, l0_ref, o1_ref, l1_ref, o2_ref, l2_ref, y_ref, nat_ref):
    tm = y_ref.shape[0]
    refs = ((o0_ref, l0_ref), (o1_ref, l1_ref), (o2_ref, l2_ref))
    r_max = DIL_PAIRS[-1][1]
    n = tm // r_max
    for hh in range(HEADS_PER_GROUP):
        for rho in range(r_max):
            outs, lses = [], []
            for (o_ref, l_ref), (_, r) in zip(refs, DIL_PAIRS):
                step = r_max // r
                rows = pl.ds(rho // r, n, stride=step) if step > 1 else pl.ds(0, n)
                outs.append(o_ref[hh, rho % r, rows, :])
                lses.append(l_ref[hh, rho % r, rows, :])
            m = jnp.maximum(jnp.maximum(lses[0], lses[1]), lses[2])
            ws = [jnp.exp(l - m) for l in lses]
            num = ws[0] * outs[0] + ws[1] * outs[1] + ws[2] * outs[2]
            den = ws[0] + ws[1] + ws[2]
            nat_ref[hh, pl.ds(rho, n, stride=r_max), :] = num / den
        y_ref[:, hh * HEAD_DIM:(hh + 1) * HEAD_DIM] = nat_ref[hh].astype(y_ref.dtype)


def _mix_a(group_outs, seq, tm=512):
    in_specs, args = [], []
    for (o, l), (_, r) in zip(group_outs, DIL_PAIRS):
        spec = pl.BlockSpec((HEADS_PER_GROUP, r, tm // r, HEAD_DIM), lambda i: (0, 0, i, 0))
        in_specs += [spec, spec]
        args += [o, l]
    return pl.pallas_call(
        _mix_a_kernel,
        grid=(seq // tm,),
        in_specs=in_specs,
        out_specs=pl.BlockSpec((tm, A_OUT), lambda i: (i, 0)),
        out_shape=jax.ShapeDtypeStruct((seq, A_OUT), BF16),
        scratch_shapes=[pltpu.VMEM((HEADS_PER_GROUP, tm, HEAD_DIM), F32)],
        compiler_params=_params("parallel"),
        name="mix_a",
    )(*args)


def _attn_b_kernel(lq1_ref, lk1_ref, lq2_ref, lk2_ref, g_ref, q_ref, qn_ref, k_ref, vt_ref,
                   *rest, lam_init, cast_cols):
    n_in = len(cast_cols)
    n_out = sum(len(cols) for cols in cast_cols)
    w_refs = rest[:n_in]
    o_ref = rest[n_in]
    wo_refs = rest[n_in + 1:n_in + 1 + n_out]
    m_ref, acc_ref, st_ref, mb_ref = rest[n_in + 1 + n_out:]
    wo_iter = iter(wo_refs)
    for w_ref, cols in zip(w_refs, cast_cols):
        for start, size in cols:
            wo_ref = next(wo_iter)
            wo_ref[...] = w_ref[:, start:start + size].astype(wo_ref.dtype)
    tq = q_ref.shape[1]
    tk = vt_ref.shape[3]
    width = vt_ref.shape[2] - ONES_ROWS
    qi = pl.program_id(1)
    m_ref[...] = jnp.full(m_ref.shape, -jnp.inf, F32)
    acc_ref[...] = jnp.zeros(acc_ref.shape, F32)

    def scores_to(slot, kv, q_src=q_ref):
        start = pl.multiple_of(kv * tk, tk)
        for i in range(2):
            st = lax.dot_general(
                k_ref[i, pl.ds(start, tk), :], q_src[i], (((1,), (1,)), ((), ())),
                preferred_element_type=F32)
            st_ref[slot, i] = st
            mb_ref[slot, i] = jnp.max(st, axis=0, keepdims=True)

    def softmax_pv(slot, kv, masked):
        vt = vt_ref[0, kv]
        for i in range(2):
            st = st_ref[slot, i]
            if masked:
                kpos = kv * tk + lax.broadcasted_iota(jnp.int32, st.shape, 0)
                qpos = qi * tq + lax.broadcasted_iota(jnp.int32, st.shape, 1)
                st = jnp.where(kpos <= qpos, st, -jnp.inf)
                m_blk = jnp.max(st, axis=0, keepdims=True)
            else:
                m_blk = mb_ref[slot, i]
            m_prev = m_ref[i]
            m_new = jnp.maximum(m_prev, m_blk)
            alpha = jnp.exp2(m_prev - m_new)
            p = jnp.exp2((st - m_new).astype(vt.dtype))
            acc_ref[i] = alpha * acc_ref[i] + jnp.dot(vt, p, preferred_element_type=F32)
            m_ref[i] = m_new

    @pl.when(qi == 0)
    def _():
        scores_to(0, 0)

    def pair(j, carry):
        scores_to(1, 2 * j + 1)
        softmax_pv(0, 2 * j, masked=False)
        scores_to(0, 2 * j + 2)
        softmax_pv(1, 2 * j + 1, masked=False)
        return carry

    lax.fori_loop(0, qi // 2, pair, 0)

    @pl.when(qi % 2 == 0)
    def _():
        softmax_pv(0, qi, masked=True)
        scores_to(0, 0, qn_ref)

    @pl.when(qi % 2 == 1)
    def _():
        scores_to(1, qi)
        softmax_pv(0, qi - 1, masked=False)
        scores_to(0, 0, qn_ref)
        softmax_pv(1, qi, masked=True)

    lam = (jnp.exp(jnp.sum(lq1_ref[...] * lk1_ref[...], axis=-1, keepdims=True))
           - jnp.exp(jnp.sum(lq2_ref[...] * lk2_ref[...], axis=-1, keepdims=True))
           + lam_init)
    inv_l = [1.0 / acc_ref[i, width:width + 1, :] for i in range(2)]
    ot = acc_ref[0, :width, :] * inv_l[0] - lam * (acc_ref[1, :width, :] * inv_l[1])
    ms = jnp.mean(ot * ot, axis=0, keepdims=True)
    yt = (ot * lax.rsqrt(ms + EPS)) * g_ref[...]
    o_ref[...] = (yt * (1.0 - lam_init)).T.astype(o_ref.dtype)


def _attn_b(qk, vt, lam_vecs, subln_g, lam_init, cast_weights):
    _, s, hd = qk.shape
    _, n_kv, rows, tk = vt.shape
    width = rows - ONES_ROWS
    tq = tk
    n_q = s // tq
    n_steps = B_HEADS * n_q
    vec = pl.BlockSpec((1, hd), lambda h, i: (0, 0))
    chunk = lambda h, i: (h * n_q + i, 0)
    w_specs, wo_specs, wo_shapes = [], [], []
    for w, cols in cast_weights:
        assert w.shape[0] % (n_steps * BF16_SUBLANES) == 0, w.shape
        rows_per_step = w.shape[0] // n_steps
        w_specs.append(pl.BlockSpec((rows_per_step, w.shape[1]), chunk))
        for _, size in cols:
            wo_specs.append(pl.BlockSpec((rows_per_step, size), chunk))
            wo_shapes.append(jax.ShapeDtypeStruct((w.shape[0], size), BF16))
    return pl.pallas_call(
        functools.partial(_attn_b_kernel, lam_init=lam_init,
                          cast_cols=tuple(tuple(cols) for _, cols in cast_weights)),
        grid=(B_HEADS, n_q),
        in_specs=[vec, vec, vec, vec,
                  pl.BlockSpec((width, 1), lambda h, i: (0, 0)),
                  pl.BlockSpec((2, tq, hd), lambda h, i: (h, i, 0)),
                  pl.BlockSpec((2, tq, hd), lambda h, i: (h, jnp.minimum(i + 1, n_q - 1), 0)),
                  pl.BlockSpec((2, s, hd), lambda h, i: (B_HEADS + h, 0, 0)),
                  pl.BlockSpec((1, n_kv, rows, tk), lambda h, i: (h, 0, 0, 0))] + w_specs,
        out_specs=[pl.BlockSpec((tq, width), lambda h, i: (i, h))] + wo_specs,
        out_shape=[jax.ShapeDtypeStruct((s, B_HEADS * width), BF16)] + wo_shapes,
        scratch_shapes=[pltpu.VMEM((2, 1, tq), F32),
                        pltpu.VMEM((2, rows, tq), F32),
                        pltpu.VMEM((2, 2, tk, tq), F32),
                        pltpu.VMEM((2, 2, 1, tq), F32)],
        compiler_params=_params("arbitrary", "arbitrary"),
        name="attn_b",
    )(*[x.reshape(1, hd) for x in lam_vecs], subln_g.reshape(width, 1), qk, qk, qk, vt,
      *[w for w, _ in cast_weights])


def _merge_kernel(ya_ref, yb_ref, wa_ref, wb_ref, ga_ref, gb_ref, o_ref):
    pa = jnp.dot(ya_ref[...], wa_ref[...].astype(BF16), preferred_element_type=F32)
    pb = jnp.dot(yb_ref[...], wb_ref[...], preferred_element_type=F32)
    o_ref[...] = (ga_ref[...].astype(F32) * pa + gb_ref[...].astype(F32) * pb).astype(o_ref.dtype)


def _merge(ya, yb, wa, wb, gates, tm=1024, tn=1024):
    s = ya.shape[0]
    n = wa.shape[1]
    return pl.pallas_call(
        _merge_kernel,
        grid=(s // tm, n // tn),
        in_specs=[pl.BlockSpec((tm, ya.shape[1]), lambda i, j: (i, 0)),
                  pl.BlockSpec((tm, yb.shape[1]), lambda i, j: (i, 0)),
                  pl.BlockSpec((wa.shape[0], tn), lambda i, j: (0, j)),
                  pl.BlockSpec((wb.shape[0], tn), lambda i, j: (0, j)),
                  pl.BlockSpec((tm, tn), lambda i, j: (i, j)),
                  pl.BlockSpec((tm, tn), lambda i, j: (i, n // tn + j))],
        out_specs=pl.BlockSpec((tm, tn), lambda i, j: (i, j)),
        out_shape=jax.ShapeDtypeStruct((s, n), BF16),
        compiler_params=_params("parallel", "arbitrary"),
        name="merge",
    )(ya, yb, wa, wb, gates, gates)


def _out_proj_kernel(a_ref, b_ref, x_ref, g_ref, x1_ref, xg_ref, r2_ref, *, n_cols):
    j = pl.program_id(1)
    x1 = x_ref[...] + jnp.dot(a_ref[...], b_ref[...], preferred_element_type=F32)
    x1_ref[...] = x1
    xg_ref[...] = (x1 * g_ref[...]).astype(xg_ref.dtype)
    part = jnp.broadcast_to(jnp.sum(x1 * x1, axis=-1, keepdims=True), r2_ref.shape)

    @pl.when(j == 0)
    def _():
        r2_ref[...] = part

    @pl.when(j > 0)
    def _():
        r2_ref[...] += part

    @pl.when(j == n_cols // x_ref.shape[1] - 1)
    def _():
        r2_ref[...] = 1.0 / (r2_ref[...] * (1.0 / n_cols) + EPS)


def _out_proj(a, b, x, g, tm=1024, tn=512):
    s, kdim = a.shape
    n = b.shape[1]
    lanes = HEAD_DIM
    return pl.pallas_call(
        functools.partial(_out_proj_kernel, n_cols=n),
        grid=(s // tm, n // tn),
        in_specs=[pl.BlockSpec((tm, kdim), lambda i, j: (i, 0)),
                  pl.BlockSpec((kdim, tn), lambda i, j: (0, j)),
                  pl.BlockSpec((tm, tn), lambda i, j: (i, j)),
                  pl.BlockSpec((1, tn), lambda i, j: (0, j))],
        out_specs=[pl.BlockSpec((tm, tn), lambda i, j: (i, j)),
                   pl.BlockSpec((tm, tn), lambda i, j: (i, j)),
                   pl.BlockSpec((tm, lanes), lambda i, j: (i, 0))],
        out_shape=[jax.ShapeDtypeStruct((s, n), F32),
                   jax.ShapeDtypeStruct((s, n), BF16),
                   jax.ShapeDtypeStruct((s, lanes), F32)],
        compiler_params=_params("parallel", "arbitrary"),
        name="out_proj",
    )(a, b, x, g.reshape(1, n))


def _ff1_kernel(xg_ref, r2_ref, b_ref, o_ref):
    lanes = r2_ref.shape[1]
    u = jnp.maximum(jnp.dot(xg_ref[...], b_ref[...], preferred_element_type=F32), 0.0)
    for c in range(o_ref.shape[1] // lanes):
        uc = u[:, c * lanes:(c + 1) * lanes]
        o_ref[:, c * lanes:(c + 1) * lanes] = (uc * uc * r2_ref[...]).astype(o_ref.dtype)


def _ff1(xg, r2, b, tm=1024, tn=1024):
    s, kdim = xg.shape
    n = b.shape[1]
    return pl.pallas_call(
        _ff1_kernel,
        grid=(s // tm, n // tn),
        in_specs=[pl.BlockSpec((tm, kdim), lambda i, j: (i, 0)),
                  pl.BlockSpec((tm, r2.shape[1]), lambda i, j: (i, 0)),
                  pl.BlockSpec((kdim, tn), lambda i, j: (0, j))],
        out_specs=pl.BlockSpec((tm, tn), lambda i, j: (i, j)),
        out_shape=jax.ShapeDtypeStruct((s, n), BF16),
        compiler_params=_params("parallel", "arbitrary"),
        name="ff1",
    )(xg, r2, b)


def _ff2_kernel(a_ref, b_ref, x_ref, o_ref):
    @pl.when(pl.program_id(2) == 0)
    def _():
        o_ref[...] = x_ref[...]

    o_ref[...] += jnp.dot(a_ref[...], b_ref[...], preferred_element_type=F32)


def _ff2(a, b, x, tm=1024, tn=1024, tk=4096):
    s, kdim = a.shape
    n = b.shape[1]
    return pl.pallas_call(
        _ff2_kernel,
        grid=(s // tm, n // tn, kdim // tk),
        in_specs=[pl.BlockSpec((tm, tk), lambda i, j, k: (i, k)),
                  pl.BlockSpec((tk, tn), lambda i, j, k: (k, j)),
                  pl.BlockSpec((tm, tn), lambda i, j, k: (i, j))],
        out_specs=pl.BlockSpec((tm, tn), lambda i, j, k: (i, j)),
        out_shape=jax.ShapeDtypeStruct((s, n), F32),
        compiler_params=_params("parallel", "parallel", "arbitrary"),
        name="ff2",
    )(a, b, x)


def _lambda_init(layer_idx):
    return 0.8 - 0.6 * math.exp(-0.3 * layer_idx)


def kernel(x, norm_mix, w_in, b_gate, w_proj_a, w_proj_b, w_out, lambda_q1, lambda_k1,
           lambda_q2, lambda_k2, subln_g, norm_mlp, w_ff1, w_ff2, norm_final):
    batch, seq, d = x.shape
    depth = w_in.shape[0]
    tables = _rope_tables(seq)
    outs = []
    for b in range(batch):
        xb = x.reshape(seq, d) if batch == 1 else x[b]
        for l in range(depth):
            w_in_l = w_in[l]
            h = _rmsnorm(xb, norm_mix[l], BF16)
            qkb = _proj_bqk(h, w_in_l, tables)
            vb = _proj_bv(h, w_in_l)
            lam_init = _lambda_init(l)
            whole = lambda w: (w, [(0, w.shape[1])])
            yb, wa_in_bf, wg_bf, wb_bf, wo_bf, w1_bf, w2_bf = _attn_b(
                qkb, vb, (lambda_q1[l], lambda_k1[l], lambda_q2[l], lambda_k2[l]),
                subln_g[l], lam_init,
                [(w_in_l, [(COL_QA, 3 * A_WIDTH), (COL_GATE, 2 * D_MODEL)]),
                 whole(w_proj_b[l]), whole(w_out[l]), whole(w_ff1[l]), whole(w_ff2[l])])
            groups = []
            for g in range(N_GROUPS):
                groups.append(_attn_a(_proj_a(h, wa_in_bf, tables, g)))
            ya = _mix_a(groups, seq)
            gates = _proj_gate(h, wg_bf, b_gate[l])
            merged = _merge(ya, yb, w_proj_a[l], wb_bf, gates)
            xb, xg, r2 = _out_proj(merged, wo_bf, xb, norm_mlp[l])
            u = _ff1(xg, r2, w1_bf)
            xb = _ff2(u, w2_bf, xb)
        outs.append(_rmsnorm(xb, norm_final, x.dtype))
    if batch == 1:
        return outs[0].reshape(1, seq, d)
    return jnp.stack(outs, axis=0)
```

```python
import functools
import math

import jax
import jax.numpy as jnp
from jax import lax
from jax.experimental import pallas as pl
from jax.experimental.pallas import tpu as pltpu

D_MODEL = 4096
HEAD_DIM = 128
DIL_PAIRS = ((128, 1), (512, 4), (2048, 16))
N_GROUPS = len(DIL_PAIRS)
HEADS_PER_GROUP = 4
GROUP_COLS = HEADS_PER_GROUP * HEAD_DIM
A_WIDTH = N_GROUPS * GROUP_COLS
A_OUT = GROUP_COLS
B_HEADS = 8
B_QK = B_HEADS * 2 * HEAD_DIM
B_V = B_HEADS * 2 * HEAD_DIM
D_FF = 4 * D_MODEL
ROPE_THETA = 500000.0
ROT_DIM = HEAD_DIM // 4
ROT_HALF = ROT_DIM // 2
EPS = 1e-6
QK_SCALE = 1.0 / math.sqrt(HEAD_DIM)
LOG2_E = math.log2(math.e)
WINDOW_STEPS = 128
ONES_ROWS = 16
MXU_COLS = 256
BF16_SUBLANES = 16

COL_QA = 0
COL_KA = A_WIDTH
COL_VA = 2 * A_WIDTH
COL_QB = 3 * A_WIDTH
COL_KB = COL_QB + B_QK
COL_VB = COL_KB + B_QK
COL_GATE = COL_VB + B_V
IN_COLS = COL_GATE + 2 * D_MODEL

VMEM_LIMIT_BYTES = 56 * 1024 * 1024

F32 = jnp.float32
BF16 = jnp.bfloat16


def _params(*semantics):
    return pltpu.CompilerParams(dimension_semantics=semantics,
                                vmem_limit_bytes=VMEM_LIMIT_BYTES)


def _rmsnorm_kernel(x_ref, g_ref, o_ref):
    x = x_ref[...]
    ms = jnp.mean(x * x, axis=-1, keepdims=True)
    o_ref[...] = ((x * lax.rsqrt(ms + EPS)) * g_ref[...]).astype(o_ref.dtype)


def _rmsnorm(x, g, out_dtype, rows=256):
    s, d = x.shape
    return pl.pallas_call(
        _rmsnorm_kernel,
        grid=(s // rows,),
        in_specs=[pl.BlockSpec((rows, d), lambda i: (i, 0)),
                  pl.BlockSpec((1, d), lambda i: (0, 0))],
        out_specs=pl.BlockSpec((rows, d), lambda i: (i, 0)),
        out_shape=jax.ShapeDtypeStruct((s, d), out_dtype),
        compiler_params=_params("parallel"),
        name="rmsnorm",
    )(x, g.reshape(1, d))


def _rope_tables(seq):
    pos = jnp.arange(seq, dtype=F32)
    inv = ROPE_THETA ** (-jnp.arange(0, ROT_DIM, 2, dtype=F32) / ROT_DIM)
    ang = pos[:, None] * inv[None, :]
    cos, sin = jnp.cos(ang), jnp.sin(ang)
    ones = jnp.ones((seq, HEAD_DIM - ROT_DIM), F32)
    zeros = jnp.zeros((seq, HEAD_DIM - ROT_DIM), F32)
    zh = jnp.zeros((seq, ROT_HALF), F32)
    cos_t = jnp.concatenate([cos, cos, ones], axis=1)
    sin_lo = jnp.concatenate([-sin, zh, zeros], axis=1)
    sin_hi = jnp.concatenate([zh, sin, zeros], axis=1)
    rot = jnp.stack([cos_t, sin_lo, sin_hi])
    ident = jnp.stack([jnp.ones_like(cos_t), jnp.zeros_like(cos_t), jnp.zeros_like(cos_t)])
    return jnp.stack([rot, ident])


def _rope_head(y, tab_ref, scale):
    x2 = pltpu.roll(y, HEAD_DIM - ROT_HALF, 1)
    x1 = pltpu.roll(y, ROT_HALF, 1)
    return (y * tab_ref[0, 0] + x2 * tab_ref[0, 1] + x1 * tab_ref[0, 2]) * scale


def _store_product(a_ref, w_ref, acc_ref, *_):
    acc_ref[...] = jnp.dot(a_ref[...], w_ref[...].astype(BF16), preferred_element_type=F32)


def _lagged_kernel(*refs, n_mm, n_ep, n_out, n_tiles, n_j, matmul, epilogue):
    mm_refs = refs[:n_mm]
    ep_refs = refs[n_mm:n_mm + n_ep]
    out_refs = refs[n_mm + n_ep:n_mm + n_ep + n_out]
    scratch = refs[n_mm + n_ep + n_out:]
    t = pl.program_id(0)
    jp = jnp.maximum(t - 1, 0) % n_j

    @pl.when(t == 0)
    def _():
        matmul(*mm_refs, *scratch)

    @pl.when(jnp.logical_and(t > 0, t < n_tiles))
    def _():
        epilogue(jp, *ep_refs, *out_refs, *scratch)
        matmul(*mm_refs, *scratch)

    @pl.when(t == n_tiles)
    def _():
        epilogue(jp, *ep_refs, *out_refs, *scratch)


def _lagged_matmul(name, epilogue, *, n_i, n_j, mm_args, mm_specs, ep_args, ep_specs,
                   out_shapes, out_specs, scratch_shapes, matmul=_store_product):
    n_tiles = n_i * n_j

    def cur(f):
        return lambda t: f(*divmod(jnp.minimum(t, n_tiles - 1), n_j))

    def prev(f):
        return lambda t: f(*divmod(jnp.maximum(t - 1, 0), n_j))

    in_specs = ([pl.BlockSpec(shape, cur(f)) for shape, f in mm_specs]
                + [pl.BlockSpec(shape, prev(f)) for shape, f in ep_specs])
    return pl.pallas_call(
        functools.partial(_lagged_kernel, n_mm=len(mm_args), n_ep=len(ep_args),
                          n_out=len(out_shapes), n_tiles=n_tiles, n_j=n_j,
                          matmul=matmul, epilogue=epilogue),
        grid=(n_tiles + 1,),
        in_specs=in_specs,
        out_specs=[pl.BlockSpec(shape, prev(f)) for shape, f in out_specs],
        out_shape=out_shapes,
        scratch_shapes=scratch_shapes,
        compiler_params=_params("arbitrary"),
        name=name,
    )(*mm_args, *ep_args)


def _proj_a_epilogue(jp, tab_ref, o_ref, acc_ref, y_ref, *, dilation):
    tm = acc_ref.shape[0]
    n = tm // dilation
    scale = jnp.where(jp == 0, QK_SCALE, 1.0).astype(F32)
    for hh in range(HEADS_PER_GROUP):
        y = _rope_head(acc_ref[:, hh * HEAD_DIM:(hh + 1) * HEAD_DIM], tab_ref, scale)
        if dilation == 1:
            o_ref[0, hh, 0] = y.astype(o_ref.dtype)
            continue
        y_ref[hh] = y
        for rho in range(dilation):
            o_ref[0, hh, rho] = y_ref[hh, pl.ds(rho, n, stride=dilation), :].astype(o_ref.dtype)


def _proj_a(h, w_in, tables, group, tm=1024):
    s, d = h.shape
    dilation = DIL_PAIRS[group][1]
    sub_len = s // dilation
    blocks_per_section = A_WIDTH // GROUP_COLS
    return _lagged_matmul(
        f"proj_a_g{group}", functools.partial(_proj_a_epilogue, dilation=dilation),
        n_i=s // tm, n_j=3,
        mm_args=[h, w_in],
        mm_specs=[((tm, d), lambda i, j: (i, 0)),
                  ((d, GROUP_COLS), lambda i, j: (0, j * blocks_per_section + group))],
        ep_args=[tables],
        ep_specs=[((1, 3, tm, HEAD_DIM), lambda i, j: (j // 2, 0, i, 0))],
        out_shapes=[jax.ShapeDtypeStruct((3, HEADS_PER_GROUP, dilation, sub_len, HEAD_DIM), BF16)],
        out_specs=[((1, HEADS_PER_GROUP, dilation, tm // dilation, HEAD_DIM),
                    lambda i, j: (j, 0, 0, i, 0))],
        scratch_shapes=[pltpu.VMEM((tm, GROUP_COLS), F32),
                        pltpu.VMEM((HEADS_PER_GROUP, tm, HEAD_DIM), F32)])[0]


def _proj_bqk_epilogue(jp, tab_ref, o_ref, acc_ref, *, q_tiles):
    scale = jnp.where(jp < q_tiles, QK_SCALE * LOG2_E, 1.0).astype(F32)
    for hh in range(o_ref.shape[0]):
        y = _rope_head(acc_ref[:, hh * HEAD_DIM:(hh + 1) * HEAD_DIM], tab_ref, scale)
        o_ref[hh] = y.astype(o_ref.dtype)


def _proj_bqk(h, w_in, tables, tm=1024, tn=512):
    s, d = h.shape
    heads_per_tile = tn // HEAD_DIM
    n = B_QK + B_QK
    return _lagged_matmul(
        "proj_b_qk", functools.partial(_proj_bqk_epilogue, q_tiles=B_QK // tn),
        n_i=s // tm, n_j=n // tn,
        mm_args=[h, w_in],
        mm_specs=[((tm, d), lambda i, j: (i, 0)),
                  ((d, tn), lambda i, j: (0, COL_QB // tn + j))],
        ep_args=[tables],
        ep_specs=[((1, 3, tm, HEAD_DIM), lambda i, j: (0, 0, i, 0))],
        out_shapes=[jax.ShapeDtypeStruct((n // HEAD_DIM, s, HEAD_DIM), BF16)],
        out_specs=[((heads_per_tile, tm, HEAD_DIM), lambda i, j: (j, i, 0))],
        scratch_shapes=[pltpu.VMEM((tm, tn), F32)])[0]


def _proj_bv_kernel(h_ref, w_ref, o_ref):
    acc = jnp.dot(h_ref[...], w_ref[...].astype(BF16), preferred_element_type=F32)
    heads, blocks, rows, tk = o_ref.shape
    width = rows - ONES_ROWS
    for hv in range(heads):
        vt = acc[:, hv * width:(hv + 1) * width].T
        for c in range(blocks):
            o_ref[hv, c, :width, :] = vt[:, c * tk:(c + 1) * tk].astype(o_ref.dtype)
            o_ref[hv, c, width:, :] = jnp.ones((ONES_ROWS, tk), o_ref.dtype)


def _proj_bv(h, w_in, tm=1024, tn=512, tk=512):
    s, d = h.shape
    width = 2 * HEAD_DIM
    rows = width + ONES_ROWS
    return pl.pallas_call(
        _proj_bv_kernel,
        grid=(s // tm, B_V // tn),
        in_specs=[pl.BlockSpec((tm, d), lambda i, j: (i, 0)),
                  pl.BlockSpec((d, tn), lambda i, j: (0, COL_VB // tn + j))],
        out_specs=pl.BlockSpec((tn // width, tm // tk, rows, tk), lambda i, j: (j, i, 0, 0)),
        out_shape=jax.ShapeDtypeStruct((B_HEADS, s // tk, rows, tk), BF16),
        compiler_params=_params("parallel", "arbitrary"),
        name="proj_b_v",
    )(h, w_in)


def _proj_gate_kernel(h_ref, w_ref, b_ref, o_ref):
    z = jnp.dot(h_ref[...], w_ref[...], preferred_element_type=F32) + b_ref[...]
    o_ref[...] = (0.5 * jnp.tanh(0.5 * z) + 0.5).astype(o_ref.dtype)


def _proj_gate(h, w_gate, b_gate, tm=1024, tn=1024):
    s, d = h.shape
    n = w_gate.shape[1]
    return pl.pallas_call(
        _proj_gate_kernel,
        grid=(s // tm, n // tn),
        in_specs=[pl.BlockSpec((tm, d), lambda i, j: (i, 0)),
                  pl.BlockSpec((d, tn), lambda i, j: (0, j)),
                  pl.BlockSpec((1, tn), lambda i, j: (0, j))],
        out_specs=pl.BlockSpec((tm, tn), lambda i, j: (i, j)),
        out_shape=jax.ShapeDtypeStruct((s, n), BF16),
        compiler_params=_params("parallel", "arbitrary"),
        name="proj_gate",
    )(h, w_gate, b_gate.reshape(1, n))


def _attn_a_kernel(q_ref, kc_ref, kp_ref, vc_ref, vp_ref, o_ref, lse_ref, kk_ref, vv_ref,
                   *, sub_len):
    w = WINDOW_STEPS
    t = q_ref.shape[1]
    c = pl.program_id(0)
    kk_ref[0:w] = kp_ref[0]
    kk_ref[w:] = kc_ref[0]
    vv_ref[0:w] = vp_ref[0]
    vv_ref[w:] = vc_ref[0]
    qp = lax.broadcasted_iota(jnp.int32, (w, 2 * w), 0)
    kp = lax.broadcasted_iota(jnp.int32, (w, 2 * w), 1)
    band = (kp >= qp) & (kp <= qp + w)
    band_start = band & (kp >= w)
    if t >= sub_len:
        masks = [band_start if (b * w) % sub_len == 0 else band for b in range(t // w)]
    else:
        is_start = (c % (sub_len // t)) == 0
        masks = [band & (kp >= jnp.where(is_start, w, 0))] + [band] * (t // w - 1)
    for b in range(t // w):
        q = q_ref[0, b * w:(b + 1) * w]
        k = kk_ref[b * w:(b + 2) * w]
        v = vv_ref[b * w:(b + 2) * w]
        s = lax.dot_general(q, k, (((1,), (1,)), ((), ())), preferred_element_type=F32)
        s = jnp.where(masks[b], s, -jnp.inf)
        m = jnp.max(s, axis=-1, keepdims=True)
        p = jnp.exp(s - m)
        l = jnp.sum(p, axis=-1, keepdims=True)
        o = jnp.dot(p.astype(v.dtype), v, preferred_element_type=F32)
        o_ref[b * w:(b + 1) * w] = o / l
        lse_ref[b * w:(b + 1) * w] = jnp.broadcast_to(m + jnp.log(l), (w, HEAD_DIM))


def _attn_a(qkv, t=2048):
    _, heads, dilation, sub_len, hd = qkv.shape
    rows = heads * dilation * sub_len
    flat = qkv.reshape(3, rows, hd)
    w = WINDOW_STEPS
    assert t % sub_len == 0 or sub_len % t == 0
    bpc = t // w
    cur = lambda sec: pl.BlockSpec((1, t, hd), lambda c: (sec, c, 0))
    prev = lambda sec: pl.BlockSpec((1, w, hd), lambda c: (sec, jnp.maximum(c * bpc - 1, 0), 0))
    out_spec = pl.BlockSpec((t, hd), lambda c: (c, 0))
    out, lse = pl.pallas_call(
        functools.partial(_attn_a_kernel, sub_len=sub_len),
        grid=(rows // t,),
        in_specs=[cur(0), cur(1), prev(1), cur(2), prev(2)],
        out_specs=[out_spec, out_spec],
        out_shape=[jax.ShapeDtypeStruct((rows, hd), F32)] * 2,
        scratch_shapes=[pltpu.VMEM((t + w, hd), BF16)] * 2,
        compiler_params=_params("parallel"),
        name=f"attn_a_r{dilation}",
    )(flat, flat, flat, flat, flat)
    shape = (heads, dilation, sub_len, hd)
    return out.reshape(shape), lse.reshape(shape)


def _mix_a_kernel(o0_ref, l0_ref, o1_ref, l1_ref, o2_ref, l2_ref, y_ref, nat_ref):
    tm = y_ref.shape[0]
    refs = ((o0_ref, l0_ref), (o1_ref, l1_ref), (o2_ref, l2_ref))
    r_max = DIL_PAIRS[-1][1]
    n = tm // r_max
    for hh in range(HEADS_PER_GROUP):
        for rho in range(r_max):
            outs, lses = [], []
            for (o_ref, l_ref), (_, r) in zip(refs, DIL_PAIRS):
                step = r_max // r
                rows = pl.ds(rho // r, n, stride=step) if step > 1 else pl.ds(0, n)
                outs.append(o_ref[hh, rho % r, rows, :])
                lses.append(l_ref[hh, rho % r, rows, :])
            m = jnp.maximum(jnp.maximum(lses[0], lses[1]), lses[2])
            ws = [jnp.exp(l - m) for l in lses]
            num = ws[0] * outs[0] + ws[1] * outs[1] + ws[2] * outs[2]
            den = ws[0] + ws[1] + ws[2]
            nat_ref[hh, pl.ds(rho, n, stride=r_max), :] = num / den
        y_ref[:, hh * HEAD_DIM:(hh + 1) * HEAD_DIM] = nat_ref[hh].astype(y_ref.dtype)


def _mix_a(group_outs, seq, tm=512):
    in_specs, args = [], []
    for (o, l), (_, r) in zip(group_outs, DIL_PAIRS):
        spec = pl.BlockSpec((HEADS_PER_GROUP, r, tm // r, HEAD_DIM), lambda i: (0, 0, i, 0))
        in_specs += [spec, spec]
        args += [o, l]
    return pl.pallas_call(
        _mix_a_kernel,
        grid=(seq // tm,),
        in_specs=in_specs,
        out_specs=pl.BlockSpec((tm, A_OUT), lambda i: (i, 0)),
        out_shape=jax.ShapeDtypeStruct((seq, A_OUT), BF16),
        scratch_shapes=[pltpu.VMEM((HEADS_PER_GROUP, tm, HEAD_DIM), F32)],
        compiler_params=_params("parallel"),
        name="mix_a",
    )(*args)


def _attn_b_kernel(lq1_ref, lk1_ref, lq2_ref, lk2_ref, g_ref, q_ref, qn_ref, k_ref, vt_ref,
                   *rest, lam_init, cast_cols):
    n_in = len(cast_cols)
    n_out = sum(len(cols) for cols in cast_cols)
    w_refs = rest[:n_in]
    o_ref = rest[n_in]
    wo_refs = rest[n_in + 1:n_in + 1 + n_out]
    m_ref, acc_ref, st_ref, mb_ref = rest[n_in + 1 + n_out:]
    tq = q_ref.shape[1]
    tk = vt_ref.shape[3]
    width = vt_ref.shape[2] - ONES_ROWS
    qi = pl.program_id(1)
    m_ref[...] = jnp.full(m_ref.shape, -jnp.inf, F32)
    acc_ref[...] = jnp.zeros(acc_ref.shape, F32)

    def scores_to(slot, kv, q_src=q_ref):
        start = pl.multiple_of(kv * tk, tk)
        for i in range(2):
            st = lax.dot_general(
                k_ref[i, pl.ds(start, tk), :], q_src[i], (((1,), (1,)), ((), ())),
                preferred_element_type=F32)
            st_ref[slot, i] = st
            mb_ref[slot, i] = jnp.max(st, axis=0, keepdims=True)

    def softmax_pv(slot, kv, masked):
        vt = vt_ref[0, kv]
        for i in range(2):
            st = st_ref[slot, i]
            if masked:
                kpos = kv * tk + lax.broadcasted_iota(jnp.int32, st.shape, 0)
                qpos = qi * tq + lax.broadcasted_iota(jnp.int32, st.shape, 1)
                st = jnp.where(kpos <= qpos, st, -jnp.inf)
                m_blk = jnp.max(st, axis=0, keepdims=True)
            else:
                m_blk = mb_ref[slot, i]
            m_prev = m_ref[i]
            m_new = jnp.maximum(m_prev, m_blk)
            alpha = jnp.exp2(m_prev - m_new)
            p = jnp.exp2((st - m_new).astype(vt.dtype))
            acc_ref[i] = alpha * acc_ref[i] + jnp.dot(vt, p, preferred_element_type=F32)
            m_ref[i] = m_new

    @pl.when(qi == 0)
    def _():
        scores_to(0, 0)

    def pair(j, carry):
        scores_to(1, 2 * j + 1)
        softmax_pv(0, 2 * j, masked=False)
        scores_to(0, 2 * j + 2)
        softmax_pv(1, 2 * j + 1, masked=False)
        return carry

    lax.fori_loop(0, qi // 2, pair, 0)

    @pl.when(qi % 2 == 0)
    def _():
        softmax_pv(0, qi, masked=True)
        scores_to(0, 0, qn_ref)

    @pl.when(qi % 2 == 1)
    def _():
        scores_to(1, qi)
        softmax_pv(0, qi - 1, masked=False)
        scores_to(0, 0, qn_ref)
        softmax_pv(1, qi, masked=True)

    lam = (jnp.exp(jnp.sum(lq1_ref[...] * lk1_ref[...], axis=-1, keepdims=True))
           - jnp.exp(jnp.sum(lq2_ref[...] * lk2_ref[...], axis=-1, keepdims=True))
           + lam_init)
    inv_l = [1.0 / acc_ref[i, width:width + 1, :] for i in range(2)]
    ot = acc_ref[0, :width, :] * inv_l[0] - lam * (acc_ref[1, :width, :] * inv_l[1])
    ms = jnp.mean(ot * ot, axis=0, keepdims=True)
    yt = (ot * lax.rsqrt(ms + EPS)) * g_ref[...]
    o_ref[...] = (yt * (1.0 - lam_init)).T.astype(o_ref.dtype)

    wo_iter = iter(wo_refs)
    for w_ref, cols in zip(w_refs, cast_cols):
        for start, size in cols:
            wo_ref = next(wo_iter)
            wo_ref[...] = w_ref[:, start:start + size].astype(wo_ref.dtype)


def _attn_b(qk, vt, lam_vecs, subln_g, lam_init, cast_weights):
    _, s, hd = qk.shape
    _, n_kv, rows, tk = vt.shape
    width = rows - ONES_ROWS
    tq = tk
    n_q = s // tq
    n_steps = B_HEADS * n_q
    vec = pl.BlockSpec((1, hd), lambda h, i: (0, 0))
    chunk = lambda h, i: (h * n_q + i, 0)
    w_specs, wo_specs, wo_shapes = [], [], []
    for w, cols in cast_weights:
        assert w.shape[0] % (n_steps * BF16_SUBLANES) == 0, w.shape
        rows_per_step = w.shape[0] // n_steps
        w_specs.append(pl.BlockSpec((rows_per_step, w.shape[1]), chunk))
        for _, size in cols:
            wo_specs.append(pl.BlockSpec((rows_per_step, size), chunk))
            wo_shapes.append(jax.ShapeDtypeStruct((w.shape[0], size), BF16))
    return pl.pallas_call(
        functools.partial(_attn_b_kernel, lam_init=lam_init,
                          cast_cols=tuple(tuple(cols) for _, cols in cast_weights)),
        grid=(B_HEADS, n_q),
        in_specs=[vec, vec, vec, vec,
                  pl.BlockSpec((width, 1), lambda h, i: (0, 0)),
                  pl.BlockSpec((2, tq, hd), lambda h, i: (h, i, 0)),
                  pl.BlockSpec((2, tq, hd), lambda h, i: (h, jnp.minimum(i + 1, n_q - 1), 0)),
                  pl.BlockSpec((2, s, hd), lambda h, i: (B_HEADS + h, 0, 0)),
                  pl.BlockSpec((1, n_kv, rows, tk), lambda h, i: (h, 0, 0, 0))] + w_specs,
        out_specs=[pl.BlockSpec((tq, width), lambda h, i: (i, h))] + wo_specs,
        out_shape=[jax.ShapeDtypeStruct((s, B_HEADS * width), BF16)] + wo_shapes,
        scratch_shapes=[pltpu.VMEM((2, 1, tq), F32),
                        pltpu.VMEM((2, rows, tq), F32),
                        pltpu.VMEM((2, 2, tk, tq), F32),
                        pltpu.VMEM((2, 2, 1, tq), F32)],
        compiler_params=_params("arbitrary", "arbitrary"),
        name="attn_b",
    )(*[x.reshape(1, hd) for x in lam_vecs], subln_g.reshape(width, 1), qk, qk, qk, vt,
      *[w for w, _ in cast_weights])


def _merge_kernel(ya_ref, yb_ref, wa_ref, wb_ref, ga_ref, gb_ref, o_ref):
    pa = jnp.dot(ya_ref[...], wa_ref[...].astype(BF16), preferred_element_type=F32)
    pb = jnp.dot(yb_ref[...], wb_ref[...], preferred_element_type=F32)
    o_ref[...] = (ga_ref[...].astype(F32) * pa + gb_ref[...].astype(F32) * pb).astype(o_ref.dtype)


def _merge(ya, yb, wa, wb, gates, tm=1024, tn=1024):
    s = ya.shape[0]
    n = wa.shape[1]
    return pl.pallas_call(
        _merge_kernel,
        grid=(s // tm, n // tn),
        in_specs=[pl.BlockSpec((tm, ya.shape[1]), lambda i, j: (i, 0)),
                  pl.BlockSpec((tm, yb.shape[1]), lambda i, j: (i, 0)),
                  pl.BlockSpec((wa.shape[0], tn), lambda i, j: (0, j)),
                  pl.BlockSpec((wb.shape[0], tn), lambda i, j: (0, j)),
                  pl.BlockSpec((tm, tn), lambda i, j: (i, j)),
                  pl.BlockSpec((tm, tn), lambda i, j: (i, n // tn + j))],
        out_specs=pl.BlockSpec((tm, tn), lambda i, j: (i, j)),
        out_shape=jax.ShapeDtypeStruct((s, n), BF16),
        compiler_params=_params("parallel", "arbitrary"),
        name="merge",
    )(ya, yb, wa, wb, gates, gates)


def _out_proj_kernel(a_ref, b_ref, x_ref, g_ref, x1_ref, xg_ref, r2_ref, *, n_cols):
    j = pl.program_id(1)
    x1 = x_ref[...] + jnp.dot(a_ref[...], b_ref[...], preferred_element_type=F32)
    x1_ref[...] = x1
    xg_ref[...] = (x1 * g_ref[...]).astype(xg_ref.dtype)
    part = jnp.broadcast_to(jnp.sum(x1 * x1, axis=-1, keepdims=True), r2_ref.shape)

    @pl.when(j == 0)
    def _():
        r2_ref[...] = part

    @pl.when(j > 0)
    def _():
        r2_ref[...] += part

    @pl.when(j == n_cols // x_ref.shape[1] - 1)
    def _():
        r2_ref[...] = 1.0 / (r2_ref[...] * (1.0 / n_cols) + EPS)


def _out_proj(a, b, x, g, tm=1024, tn=512):
    s, kdim = a.shape
    n = b.shape[1]
    lanes = HEAD_DIM
    return pl.pallas_call(
        functools.partial(_out_proj_kernel, n_cols=n),
        grid=(s // tm, n // tn),
        in_specs=[pl.BlockSpec((tm, kdim), lambda i, j: (i, 0)),
                  pl.BlockSpec((kdim, tn), lambda i, j: (0, j)),
                  pl.BlockSpec((tm, tn), lambda i, j: (i, j)),
                  pl.BlockSpec((1, tn), lambda i, j: (0, j))],
        out_specs=[pl.BlockSpec((tm, tn), lambda i, j: (i, j)),
                   pl.BlockSpec((tm, tn), lambda i, j: (i, j)),
                   pl.BlockSpec((tm, lanes), lambda i, j: (i, 0))],
        out_shape=[jax.ShapeDtypeStruct((s, n), F32),
                   jax.ShapeDtypeStruct((s, n), BF16),
                   jax.ShapeDtypeStruct((s, lanes), F32)],
        compiler_params=_params("parallel", "arbitrary"),
        name="out_proj",
    )(a, b, x, g.reshape(1, n))


def _ff1_kernel(xg_ref, r2_ref, b_ref, o_ref):
    lanes = r2_ref.shape[1]
    u = jnp.maximum(jnp.dot(xg_ref[...], b_ref[...], preferred_element_type=F32), 0.0)
    for c in range(o_ref.shape[1] // lanes):
        uc = u[:, c * lanes:(c + 1) * lanes]
        o_ref[:, c * lanes:(c + 1) * lanes] = (uc * uc * r2_ref[...]).astype(o_ref.dtype)


def _ff1(xg, r2, b, tm=1024, tn=1024):
    s, kdim = xg.shape
    n = b.shape[1]
    return pl.pallas_call(
        _ff1_kernel,
        grid=(s // tm, n // tn),
        in_specs=[pl.BlockSpec((tm, kdim), lambda i, j: (i, 0)),
                  pl.BlockSpec((tm, r2.shape[1]), lambda i, j: (i, 0)),
                  pl.BlockSpec((kdim, tn), lambda i, j: (0, j))],
        out_specs=pl.BlockSpec((tm, tn), lambda i, j: (i, j)),
        out_shape=jax.ShapeDtypeStruct((s, n), BF16),
        compiler_params=_params("parallel", "arbitrary"),
        name="ff1",
    )(xg, r2, b)


def _ff2_kernel(a_ref, b_ref, x_ref, o_ref):
    @pl.when(pl.program_id(2) == 0)
    def _():
        o_ref[...] = x_ref[...]

    o_ref[...] += jnp.dot(a_ref[...], b_ref[...], preferred_element_type=F32)


def _ff2(a, b, x, tm=1024, tn=1024, tk=4096):
    s, kdim = a.shape
    n = b.shape[1]
    return pl.pallas_call(
        _ff2_kernel,
        grid=(s // tm, n // tn, kdim // tk),
        in_specs=[pl.BlockSpec((tm, tk), lambda i, j, k: (i, k)),
                  pl.BlockSpec((tk, tn), lambda i, j, k: (k, j)),
                  pl.BlockSpec((tm, tn), lambda i, j, k: (i, j))],
        out_specs=pl.BlockSpec((tm, tn), lambda i, j, k: (i, j)),
        out_shape=jax.ShapeDtypeStruct((s, n), F32),
        compiler_params=_params("parallel", "parallel", "arbitrary"),
        name="ff2",
    )(a, b, x)


def _lambda_init(layer_idx):
    return 0.8 - 0.6 * math.exp(-0.3 * layer_idx)


def kernel(x, norm_mix, w_in, b_gate, w_proj_a, w_proj_b, w_out, lambda_q1, lambda_k1,
           lambda_q2, lambda_k2, subln_g, norm_mlp, w_ff1, w_ff2, norm_final):
    batch, seq, d = x.shape
    depth = w_in.shape[0]
    tables = _rope_tables(seq)
    outs = []
    for b in range(batch):
        xb = x.reshape(seq, d) if batch == 1 else x[b]
        for l in range(depth):
            w_in_l = w_in[l]
            h = _rmsnorm(xb, norm_mix[l], BF16)
            qkb = _proj_bqk(h, w_in_l, tables)
            vb = _proj_bv(h, w_in_l)
            lam_init = _lambda_init(l)
            whole = lambda w: (w, [(0, w.shape[1])])
            yb, wa_in_bf, wg_bf, wb_bf, wo_bf, w1_bf, w2_bf = _attn_b(
                qkb, vb, (lambda_q1[l], lambda_k1[l], lambda_q2[l], lambda_k2[l]),
                subln_g[l], lam_init,
                [(w_in_l, [(COL_QA, 3 * A_WIDTH), (COL_GATE, 2 * D_MODEL)]),
                 whole(w_proj_b[l]), whole(w_out[l]), whole(w_ff1[l]), whole(w_ff2[l])])
            groups = []
            for g in range(N_GROUPS):
                groups.append(_attn_a(_proj_a(h, wa_in_bf, tables, g)))
            ya = _mix_a(groups, seq)
            gates = _proj_gate(h, wg_bf, b_gate[l])
            merged = _merge(ya, yb, w_proj_a[l], wb_bf, gates)
            xb, xg, r2 = _out_proj(merged, wo_bf, xb, norm_mlp[l])
            u = _ff1(xg, r2, w1_bf)
            xb = _ff2(u, w2_bf, xb)
        outs.append(_rmsnorm(xb, norm_final, x.dtype))
    if batch == 1:
        return outs[0].reshape(1, seq, d)
    return jnp.stack(outs, axis=0)
```

```python
import functools
import math

import jax
import jax.numpy as jnp
from jax import lax
from jax.experimental import pallas as pl
from jax.experimental.pallas import tpu as pltpu

D_MODEL = 4096
HEAD_DIM = 128
DIL_PAIRS = ((128, 1), (512, 4), (2048, 16))
N_GROUPS = len(DIL_PAIRS)
HEADS_PER_GROUP = 4
GROUP_COLS = HEADS_PER_GROUP * HEAD_DIM
A_WIDTH = N_GROUPS * GROUP_COLS
A_OUT = GROUP_COLS
B_HEADS = 8
B_QK = B_HEADS * 2 * HEAD_DIM
B_V = B_HEADS * 2 * HEAD_DIM
D_FF = 4 * D_MODEL
ROPE_THETA = 500000.0
ROT_DIM = HEAD_DIM // 4
ROT_HALF = ROT_DIM // 2
EPS = 1e-6
QK_SCALE = 1.0 / math.sqrt(HEAD_DIM)
LOG2_E = math.log2(math.e)
WINDOW_STEPS = 128
ONES_ROWS = 16
MXU_COLS = 256
BF16_SUBLANES = 16

COL_QA = 0
COL_KA = A_WIDTH
COL_VA = 2 * A_WIDTH
COL_QB = 3 * A_WIDTH
COL_KB = COL_QB + B_QK
COL_VB = COL_KB + B_QK
COL_GATE = COL_VB + B_V
IN_COLS = COL_GATE + 2 * D_MODEL

VMEM_LIMIT_BYTES = 56 * 1024 * 1024

F32 = jnp.float32
BF16 = jnp.bfloat16


def _params(*semantics):
    return pltpu.CompilerParams(dimension_semantics=semantics,
                                vmem_limit_bytes=VMEM_LIMIT_BYTES)


def _rmsnorm_kernel(x_ref, g_ref, o_ref):
    x = x_ref[...]
    ms = jnp.mean(x * x, axis=-1, keepdims=True)
    o_ref[...] = ((x * lax.rsqrt(ms + EPS)) * g_ref[...]).astype(o_ref.dtype)


def _rmsnorm(x, g, out_dtype, rows=512):
    s, d = x.shape
    return pl.pallas_call(
        _rmsnorm_kernel,
        grid=(s // rows,),
        in_specs=[pl.BlockSpec((rows, d), lambda i: (i, 0)),
                  pl.BlockSpec((1, d), lambda i: (0, 0))],
        out_specs=pl.BlockSpec((rows, d), lambda i: (i, 0)),
        out_shape=jax.ShapeDtypeStruct((s, d), out_dtype),
        compiler_params=_params("parallel"),
        name="rmsnorm",
    )(x, g.reshape(1, d))


def _rope_tables(seq):
    pos = jnp.arange(seq, dtype=F32)
    inv = ROPE_THETA ** (-jnp.arange(0, ROT_DIM, 2, dtype=F32) / ROT_DIM)
    ang = pos[:, None] * inv[None, :]
    cos, sin = jnp.cos(ang), jnp.sin(ang)
    ones = jnp.ones((seq, HEAD_DIM - ROT_DIM), F32)
    zeros = jnp.zeros((seq, HEAD_DIM - ROT_DIM), F32)
    zh = jnp.zeros((seq, ROT_HALF), F32)
    cos_t = jnp.concatenate([cos, cos, ones], axis=1)
    sin_lo = jnp.concatenate([-sin, zh, zeros], axis=1)
    sin_hi = jnp.concatenate([zh, sin, zeros], axis=1)
    rot = jnp.stack([cos_t, sin_lo, sin_hi])
    ident = jnp.stack([jnp.ones_like(cos_t), jnp.zeros_like(cos_t), jnp.zeros_like(cos_t)])
    return jnp.stack([rot, ident])


def _rope_head(y, tab_ref, scale):
    x2 = pltpu.roll(y, HEAD_DIM - ROT_HALF, 1)
    x1 = pltpu.roll(y, ROT_HALF, 1)
    return (y * tab_ref[0, 0] + x2 * tab_ref[0, 1] + x1 * tab_ref[0, 2]) * scale


def _store_product(a_ref, w_ref, acc_ref, *_):
    acc_ref[...] = jnp.dot(a_ref[...], w_ref[...].astype(BF16), preferred_element_type=F32)


def _lagged_kernel(*refs, n_mm, n_ep, n_out, n_tiles, n_j, matmul, epilogue):
    mm_refs = refs[:n_mm]
    ep_refs = refs[n_mm:n_mm + n_ep]
    out_refs = refs[n_mm + n_ep:n_mm + n_ep + n_out]
    scratch = refs[n_mm + n_ep + n_out:]
    t = pl.program_id(0)
    jp = jnp.maximum(t - 1, 0) % n_j

    @pl.when(t == 0)
    def _():
        matmul(*mm_refs, *scratch)

    @pl.when(jnp.logical_and(t > 0, t < n_tiles))
    def _():
        epilogue(jp, *ep_refs, *out_refs, *scratch)
        matmul(*mm_refs, *scratch)

    @pl.when(t == n_tiles)
    def _():
        epilogue(jp, *ep_refs, *out_refs, *scratch)


def _lagged_matmul(name, epilogue, *, n_i, n_j, mm_args, mm_specs, ep_args, ep_specs,
                   out_shapes, out_specs, scratch_shapes, matmul=_store_product):
    n_tiles = n_i * n_j

    def cur(f):
        return lambda t: f(*divmod(jnp.minimum(t, n_tiles - 1), n_j))

    def prev(f):
        return lambda t: f(*divmod(jnp.maximum(t - 1, 0), n_j))

    in_specs = ([pl.BlockSpec(shape, cur(f)) for shape, f in mm_specs]
                + [pl.BlockSpec(shape, prev(f)) for shape, f in ep_specs])
    return pl.pallas_call(
        functools.partial(_lagged_kernel, n_mm=len(mm_args), n_ep=len(ep_args),
                          n_out=len(out_shapes), n_tiles=n_tiles, n_j=n_j,
                          matmul=matmul, epilogue=epilogue),
        grid=(n_tiles + 1,),
        in_specs=in_specs,
        out_specs=[pl.BlockSpec(shape, prev(f)) for shape, f in out_specs],
        out_shape=out_shapes,
        scratch_shapes=scratch_shapes,
        compiler_params=_params("arbitrary"),
        name=name,
    )(*mm_args, *ep_args)


def _proj_a_epilogue(jp, tab_ref, o_ref, acc_ref, y_ref, *, dilation):
    tm = acc_ref.shape[0]
    n = tm // dilation
    scale = jnp.where(jp == 0, QK_SCALE, 1.0).astype(F32)
    for hh in range(HEADS_PER_GROUP):
        y = _rope_head(acc_ref[:, hh * HEAD_DIM:(hh + 1) * HEAD_DIM], tab_ref, scale)
        if dilation == 1:
            o_ref[0, hh, 0] = y.astype(o_ref.dtype)
            continue
        y_ref[hh] = y
        for rho in range(dilation):
            o_ref[0, hh, rho] = y_ref[hh, pl.ds(rho, n, stride=dilation), :].astype(o_ref.dtype)


def _proj_a(h, w_in, tables, group, tm=1024):
    s, d = h.shape
    dilation = DIL_PAIRS[group][1]
    sub_len = s // dilation
    blocks_per_section = A_WIDTH // GROUP_COLS
    return _lagged_matmul(
        f"proj_a_g{group}", functools.partial(_proj_a_epilogue, dilation=dilation),
        n_i=s // tm, n_j=3,
        mm_args=[h, w_in],
        mm_specs=[((tm, d), lambda i, j: (i, 0)),
                  ((d, GROUP_COLS), lambda i, j: (0, j * blocks_per_section + group))],
        ep_args=[tables],
        ep_specs=[((1, 3, tm, HEAD_DIM), lambda i, j: (j // 2, 0, i, 0))],
        out_shapes=[jax.ShapeDtypeStruct((3, HEADS_PER_GROUP, dilation, sub_len, HEAD_DIM), BF16)],
        out_specs=[((1, HEADS_PER_GROUP, dilation, tm // dilation, HEAD_DIM),
                    lambda i, j: (j, 0, 0, i, 0))],
        scratch_shapes=[pltpu.VMEM((tm, GROUP_COLS), F32),
                        pltpu.VMEM((HEADS_PER_GROUP, tm, HEAD_DIM), F32)])[0]


def _proj_bqk_epilogue(jp, tab_ref, o_ref, acc_ref, *, q_tiles):
    scale = jnp.where(jp < q_tiles, QK_SCALE * LOG2_E, 1.0).astype(F32)
    for hh in range(o_ref.shape[0]):
        y = _rope_head(acc_ref[:, hh * HEAD_DIM:(hh + 1) * HEAD_DIM], tab_ref, scale)
        o_ref[hh] = y.astype(o_ref.dtype)


def _proj_bqk(h, w_in, tables, tm=1024, tn=512):
    s, d = h.shape
    heads_per_tile = tn // HEAD_DIM
    n = B_QK + B_QK
    return _lagged_matmul(
        "proj_b_qk", functools.partial(_proj_bqk_epilogue, q_tiles=B_QK // tn),
        n_i=s // tm, n_j=n // tn,
        mm_args=[h, w_in],
        mm_specs=[((tm, d), lambda i, j: (i, 0)),
                  ((d, tn), lambda i, j: (0, COL_QB // tn + j))],
        ep_args=[tables],
        ep_specs=[((1, 3, tm, HEAD_DIM), lambda i, j: (0, 0, i, 0))],
        out_shapes=[jax.ShapeDtypeStruct((n // HEAD_DIM, s, HEAD_DIM), BF16)],
        out_specs=[((heads_per_tile, tm, HEAD_DIM), lambda i, j: (j, i, 0))],
        scratch_shapes=[pltpu.VMEM((tm, tn), F32)])[0]


def _proj_bv_kernel(h_ref, w_ref, o_ref):
    acc = jnp.dot(h_ref[...], w_ref[...].astype(BF16), preferred_element_type=F32)
    heads, blocks, rows, tk = o_ref.shape
    width = rows - ONES_ROWS
    for hv in range(heads):
        vt = acc[:, hv * width:(hv + 1) * width].T
        for c in range(blocks):
            o_ref[hv, c, :width, :] = vt[:, c * tk:(c + 1) * tk].astype(o_ref.dtype)
            o_ref[hv, c, width:, :] = jnp.ones((ONES_ROWS, tk), o_ref.dtype)


def _proj_bv(h, w_in, tm=1024, tn=512, tk=512):
    s, d = h.shape
    width = 2 * HEAD_DIM
    rows = width + ONES_ROWS
    return pl.pallas_call(
        _proj_bv_kernel,
        grid=(s // tm, B_V // tn),
        in_specs=[pl.BlockSpec((tm, d), lambda i, j: (i, 0)),
                  pl.BlockSpec((d, tn), lambda i, j: (0, COL_VB // tn + j))],
        out_specs=pl.BlockSpec((tn // width, tm // tk, rows, tk), lambda i, j: (j, i, 0, 0)),
        out_shape=jax.ShapeDtypeStruct((B_HEADS, s // tk, rows, tk), BF16),
        compiler_params=_params("parallel", "arbitrary"),
        name="proj_b_v",
    )(h, w_in)


def _proj_gate_kernel(h_ref, w_ref, b_ref, o_ref):
    z = jnp.dot(h_ref[...], w_ref[...], preferred_element_type=F32) + b_ref[...]
    o_ref[...] = (0.5 * jnp.tanh(0.5 * z) + 0.5).astype(o_ref.dtype)


def _proj_gate(h, w_gate, b_gate, tm=1024, tn=1024):
    s, d = h.shape
    n = w_gate.shape[1]
    return pl.pallas_call(
        _proj_gate_kernel,
        grid=(s // tm, n // tn),
        in_specs=[pl.BlockSpec((tm, d), lambda i, j: (i, 0)),
                  pl.BlockSpec((d, tn), lambda i, j: (0, j)),
                  pl.BlockSpec((1, tn), lambda i, j: (0, j))],
        out_specs=pl.BlockSpec((tm, tn), lambda i, j: (i, j)),
        out_shape=jax.ShapeDtypeStruct((s, n), BF16),
        compiler_params=_params("parallel", "arbitrary"),
        name="proj_gate",
    )(h, w_gate, b_gate.reshape(1, n))


def _attn_a_kernel(q_ref, kc_ref, kp_ref, vc_ref, vp_ref, o_ref, lse_ref, kk_ref, vv_ref,
                   *, sub_len):
    w = WINDOW_STEPS
    t = q_ref.shape[1]
    c = pl.program_id(0)
    kk_ref[0:w] = kp_ref[0]
    kk_ref[w:] = kc_ref[0]
    vv_ref[0:w] = vp_ref[0]
    vv_ref[w:] = vc_ref[0]
    qp = lax.broadcasted_iota(jnp.int32, (w, 2 * w), 0)
    kp = lax.broadcasted_iota(jnp.int32, (w, 2 * w), 1)
    band = (kp >= qp) & (kp <= qp + w)
    band_start = band & (kp >= w)
    if t >= sub_len:
        masks = [band_start if (b * w) % sub_len == 0 else band for b in range(t // w)]
    else:
        is_start = (c % (sub_len // t)) == 0
        masks = [band & (kp >= jnp.where(is_start, w, 0))] + [band] * (t // w - 1)
    for b in range(t // w):
        q = q_ref[0, b * w:(b + 1) * w]
        k = kk_ref[b * w:(b + 2) * w]
        v = vv_ref[b * w:(b + 2) * w]
        s = lax.dot_general(q, k, (((1,), (1,)), ((), ())), preferred_element_type=F32)
        s = jnp.where(masks[b], s, -jnp.inf)
        m = jnp.max(s, axis=-1, keepdims=True)
        p = jnp.exp(s - m)
        l = jnp.sum(p, axis=-1, keepdims=True)
        o = jnp.dot(p.astype(v.dtype), v, preferred_element_type=F32)
        o_ref[b * w:(b + 1) * w] = o / l
        lse_ref[b * w:(b + 1) * w] = jnp.broadcast_to(m + jnp.log(l), (w, HEAD_DIM))


def _attn_a(qkv, t=4096):
    _, heads, dilation, sub_len, hd = qkv.shape
    rows = heads * dilation * sub_len
    flat = qkv.reshape(3, rows, hd)
    w = WINDOW_STEPS
    assert t % sub_len == 0 or sub_len % t == 0
    bpc = t // w
    cur = lambda sec: pl.BlockSpec((1, t, hd), lambda c: (sec, c, 0))
    prev = lambda sec: pl.BlockSpec((1, w, hd), lambda c: (sec, jnp.maximum(c * bpc - 1, 0), 0))
    out_spec = pl.BlockSpec((t, hd), lambda c: (c, 0))
    out, lse = pl.pallas_call(
        functools.partial(_attn_a_kernel, sub_len=sub_len),
        grid=(rows // t,),
        in_specs=[cur(0), cur(1), prev(1), cur(2), prev(2)],
        out_specs=[out_spec, out_spec],
        out_shape=[jax.ShapeDtypeStruct((rows, hd), F32)] * 2,
        scratch_shapes=[pltpu.VMEM((t + w, hd), BF16)] * 2,
        compiler_params=_params("parallel"),
        name=f"attn_a_r{dilation}",
    )(flat, flat, flat, flat, flat)
    shape = (heads, dilation, sub_len, hd)
    return out.reshape(shape), lse.reshape(shape)


def _mix_a_kernel(o0_ref, l0_ref, o1_ref, l1_ref, o2_ref, l2_ref, y_ref, nat_ref):
    tm = y_ref.shape[0]
    refs = ((o0_ref, l0_ref), (o1_ref, l1_ref), (o2_ref, l2_ref))
    r_max = DIL_PAIRS[-1][1]
    n = tm // r_max
    for hh in range(HEADS_PER_GROUP):
        for rho in range(r_max):
            outs, lses = [], []
            for (o_ref, l_ref), (_, r) in zip(refs, DIL_PAIRS):
                step = r_max // r
                rows = pl.ds(rho // r, n, stride=step) if step > 1 else pl.ds(0, n)
                outs.append(o_ref[hh, rho % r, rows, :])
                lses.append(l_ref[hh, rho % r, rows, :])
            m = jnp.maximum(jnp.maximum(lses[0], lses[1]), lses[2])
            ws = [jnp.exp(l - m) for l in lses]
            num = ws[0] * outs[0] + ws[1] * outs[1] + ws[2] * outs[2]
            den = ws[0] + ws[1] + ws[2]
            nat_ref[hh, pl.ds(rho, n, stride=r_max), :] = num / den
        y_ref[:, hh * HEAD_DIM:(hh + 1) * HEAD_DIM] = nat_ref[hh].astype(y_ref.dtype)


def _mix_a(group_outs, seq, tm=512):
    in_specs, args = [], []
    for (o, l), (_, r) in zip(group_outs, DIL_PAIRS):
        spec = pl.BlockSpec((HEADS_PER_GROUP, r, tm // r, HEAD_DIM), lambda i: (0, 0, i, 0))
        in_specs += [spec, spec]
        args += [o, l]
    return pl.pallas_call(
        _mix_a_kernel,
        grid=(seq // tm,),
        in_specs=in_specs,
        out_specs=pl.BlockSpec((tm, A_OUT), lambda i: (i, 0)),
        out_shape=jax.ShapeDtypeStruct((seq, A_OUT), BF16),
        scratch_shapes=[pltpu.VMEM((HEADS_PER_GROUP, tm, HEAD_DIM), F32)],
        compiler_params=_params("parallel"),
        name="mix_a",
    )(*args)


def _attn_b_kernel(lq1_ref, lk1_ref, lq2_ref, lk2_ref, g_ref, q_ref, qn_ref, k_ref, vt_ref,
                   *rest, lam_init, cast_cols):
    n_in = len(cast_cols)
    n_out = sum(len(cols) for cols in cast_cols)
    w_refs = rest[:n_in]
    o_ref = rest[n_in]
    wo_refs = rest[n_in + 1:n_in + 1 + n_out]
    m_ref, acc_ref, st_ref, mb_ref = rest[n_in + 1 + n_out:]
    tq = q_ref.shape[1]
    tk = vt_ref.shape[3]
    width = vt_ref.shape[2] - ONES_ROWS
    qi = pl.program_id(1)
    m_ref[...] = jnp.full(m_ref.shape, -jnp.inf, F32)
    acc_ref[...] = jnp.zeros(acc_ref.shape, F32)

    def scores_to(slot, kv, q_src=q_ref):
        start = pl.multiple_of(kv * tk, tk)
        for i in range(2):
            st = lax.dot_general(
                k_ref[i, pl.ds(start, tk), :], q_src[i], (((1,), (1,)), ((), ())),
                preferred_element_type=F32)
            st_ref[slot, i] = st
            mb_ref[slot, i] = jnp.max(st, axis=0, keepdims=True)

    def softmax_pv(slot, kv, masked):
        vt = vt_ref[0, kv]
        for i in range(2):
            st = st_ref[slot, i]
            if masked:
                kpos = kv * tk + lax.broadcasted_iota(jnp.int32, st.shape, 0)
                qpos = qi * tq + lax.broadcasted_iota(jnp.int32, st.shape, 1)
                st = jnp.where(kpos <= qpos, st, -jnp.inf)
                m_blk = jnp.max(st, axis=0, keepdims=True)
            else:
                m_blk = mb_ref[slot, i]
            m_prev = m_ref[i]
            m_new = jnp.maximum(m_prev, m_blk)
            alpha = jnp.exp2(m_prev - m_new)
            p = jnp.exp2((st - m_new).astype(vt.dtype))
            acc_ref[i] = alpha * acc_ref[i] + jnp.dot(vt, p, preferred_element_type=F32)
            m_ref[i] = m_new

    @pl.when(qi == 0)
    def _():
        scores_to(0, 0)

    def pair(j, carry):
        scores_to(1, 2 * j + 1)
        softmax_pv(0, 2 * j, masked=False)
        scores_to(0, 2 * j + 2)
        softmax_pv(1, 2 * j + 1, masked=False)
        return carry

    lax.fori_loop(0, qi // 2, pair, 0)

    @pl.when(qi % 2 == 0)
    def _():
        softmax_pv(0, qi, masked=True)
        scores_to(0, 0, qn_ref)

    @pl.when(qi % 2 == 1)
    def _():
        scores_to(1, qi)
        softmax_pv(0, qi - 1, masked=False)
        scores_to(0, 0, qn_ref)
        softmax_pv(1, qi, masked=True)

    lam = (jnp.exp(jnp.sum(lq1_ref[...] * lk1_ref[...], axis=-1, keepdims=True))
           - jnp.exp(jnp.sum(lq2_ref[...] * lk2_ref[...], axis=-1, keepdims=True))
           + lam_init)
    inv_l = [1.0 / acc_ref[i, width:width + 1, :] for i in range(2)]
    ot = acc_ref[0, :width, :] * inv_l[0] - lam * (acc_ref[1, :width, :] * inv_l[1])
    ms = jnp.mean(ot * ot, axis=0, keepdims=True)
    yt = (ot * lax.rsqrt(ms + EPS)) * g_ref[...]
    o_ref[...] = (yt * (1.0 - lam_init)).T.astype(o_ref.dtype)

    wo_iter = iter(wo_refs)
    for w_ref, cols in zip(w_refs, cast_cols):
        for start, size in cols:
            wo_ref = next(wo_iter)
            wo_ref[...] = w_ref[:, start:start + size].astype(wo_ref.dtype)


def _attn_b(qk, vt, lam_vecs, subln_g, lam_init, cast_weights):
    _, s, hd = qk.shape
    _, n_kv, rows, tk = vt.shape
    width = rows - ONES_ROWS
    tq = tk
    n_q = s // tq
    n_steps = B_HEADS * n_q
    vec = pl.BlockSpec((1, hd), lambda h, i: (0, 0))
    chunk = lambda h, i: (h * n_q + i, 0)
    w_specs, wo_specs, wo_shapes = [], [], []
    for w, cols in cast_weights:
        assert w.shape[0] % (n_steps * BF16_SUBLANES) == 0, w.shape
        rows_per_step = w.shape[0] // n_steps
        w_specs.append(pl.BlockSpec((rows_per_step, w.shape[1]), chunk))
        for _, size in cols:
            wo_specs.append(pl.BlockSpec((rows_per_step, size), chunk))
            wo_shapes.append(jax.ShapeDtypeStruct((w.shape[0], size), BF16))
    return pl.pallas_call(
        functools.partial(_attn_b_kernel, lam_init=lam_init,
                          cast_cols=tuple(tuple(cols) for _, cols in cast_weights)),
        grid=(B_HEADS, n_q),
        in_specs=[vec, vec, vec, vec,
                  pl.BlockSpec((width, 1), lambda h, i: (0, 0)),
                  pl.BlockSpec((2, tq, hd), lambda h, i: (h, i, 0)),
                  pl.BlockSpec((2, tq, hd), lambda h, i: (h, jnp.minimum(i + 1, n_q - 1), 0)),
                  pl.BlockSpec((2, s, hd), lambda h, i: (B_HEADS + h, 0, 0)),
                  pl.BlockSpec((1, n_kv, rows, tk), lambda h, i: (h, 0, 0, 0))] + w_specs,
        out_specs=[pl.BlockSpec((tq, width), lambda h, i: (i, h))] + wo_specs,
        out_shape=[jax.ShapeDtypeStruct((s, B_HEADS * width), BF16)] + wo_shapes,
        scratch_shapes=[pltpu.VMEM((2, 1, tq), F32),
                        pltpu.VMEM((2, rows, tq), F32),
                        pltpu.VMEM((2, 2, tk, tq), F32),
                        pltpu.VMEM((2, 2, 1, tq), F32)],
        compiler_params=_params("arbitrary", "arbitrary"),
        name="attn_b",
    )(*[x.reshape(1, hd) for x in lam_vecs], subln_g.reshape(width, 1), qk, qk, qk, vt,
      *[w for w, _ in cast_weights])


def _merge_kernel(ya_ref, yb_ref, wa_ref, wb_ref, ga_ref, gb_ref, o_ref):
    pa = jnp.dot(ya_ref[...], wa_ref[...].astype(BF16), preferred_element_type=F32)
    pb = jnp.dot(yb_ref[...], wb_ref[...], preferred_element_type=F32)
    o_ref[...] = (ga_ref[...].astype(F32) * pa + gb_ref[...].astype(F32) * pb).astype(o_ref.dtype)


def _merge(ya, yb, wa, wb, gates, tm=1024, tn=1024):
    s = ya.shape[0]
    n = wa.shape[1]
    return pl.pallas_call(
        _merge_kernel,
        grid=(s // tm, n // tn),
        in_specs=[pl.BlockSpec((tm, ya.shape[1]), lambda i, j: (i, 0)),
                  pl.BlockSpec((tm, yb.shape[1]), lambda i, j: (i, 0)),
                  pl.BlockSpec((wa.shape[0], tn), lambda i, j: (0, j)),
                  pl.BlockSpec((wb.shape[0], tn), lambda i, j: (0, j)),
                  pl.BlockSpec((tm, tn), lambda i, j: (i, j)),
                  pl.BlockSpec((tm, tn), lambda i, j: (i, n // tn + j))],
        out_specs=pl.BlockSpec((tm, tn), lambda i, j: (i, j)),
        out_shape=jax.ShapeDtypeStruct((s, n), BF16),
        compiler_params=_params("parallel", "arbitrary"),
        name="merge",
    )(ya, yb, wa, wb, gates, gates)


def _out_proj_kernel(a_ref, b_ref, x_ref, g_ref, x1_ref, xg_ref, r2_ref, *, n_cols):
    j = pl.program_id(1)
    x1 = x_ref[...] + jnp.dot(a_ref[...], b_ref[...], preferred_element_type=F32)
    x1_ref[...] = x1
    xg_ref[...] = (x1 * g_ref[...]).astype(xg_ref.dtype)
    part = jnp.broadcast_to(jnp.sum(x1 * x1, axis=-1, keepdims=True), r2_ref.shape)

    @pl.when(j == 0)
    def _():
        r2_ref[...] = part

    @pl.when(j > 0)
    def _():
        r2_ref[...] += part

    @pl.when(j == n_cols // x_ref.shape[1] - 1)
    def _():
        r2_ref[...] = 1.0 / (r2_ref[...] * (1.0 / n_cols) + EPS)


def _out_proj(a, b, x, g, tm=1024, tn=512):
    s, kdim = a.shape
    n = b.shape[1]
    lanes = HEAD_DIM
    return pl.pallas_call(
        functools.partial(_out_proj_kernel, n_cols=n),
        grid=(s // tm, n // tn),
        in_specs=[pl.BlockSpec((tm, kdim), lambda i, j: (i, 0)),
                  pl.BlockSpec((kdim, tn), lambda i, j: (0, j)),
                  pl.BlockSpec((tm, tn), lambda i, j: (i, j)),
                  pl.BlockSpec((1, tn), lambda i, j: (0, j))],
        out_specs=[pl.BlockSpec((tm, tn), lambda i, j: (i, j)),
                   pl.BlockSpec((tm, tn), lambda i, j: (i, j)),
                   pl.BlockSpec((tm, lanes), lambda i, j: (i, 0))],
        out_shape=[jax.ShapeDtypeStruct((s, n), F32),
                   jax.ShapeDtypeStruct((s, n), BF16),
                   jax.ShapeDtypeStruct((s, lanes), F32)],
        compiler_params=_params("parallel", "arbitrary"),
        name="out_proj",
    )(a, b, x, g.reshape(1, n))


def _ff1_kernel(xg_ref, r2_ref, b_ref, o_ref):
    lanes = r2_ref.shape[1]
    u = jnp.maximum(jnp.dot(xg_ref[...], b_ref[...], preferred_element_type=F32), 0.0)
    for c in range(o_ref.shape[1] // lanes):
        uc = u[:, c * lanes:(c + 1) * lanes]
        o_ref[:, c * lanes:(c + 1) * lanes] = (uc * uc * r2_ref[...]).astype(o_ref.dtype)


def _ff1(xg, r2, b, tm=1024, tn=1024):
    s, kdim = xg.shape
    n = b.shape[1]
    return pl.pallas_call(
        _ff1_kernel,
        grid=(s // tm, n // tn),
        in_specs=[pl.BlockSpec((tm, kdim), lambda i, j: (i, 0)),
                  pl.BlockSpec((tm, r2.shape[1]), lambda i, j: (i, 0)),
                  pl.BlockSpec((kdim, tn), lambda i, j: (0, j))],
        out_specs=pl.BlockSpec((tm, tn), lambda i, j: (i, j)),
        out_shape=jax.ShapeDtypeStruct((s, n), BF16),
        compiler_params=_params("parallel", "arbitrary"),
        name="ff1",
    )(xg, r2, b)


def _ff2_kernel(a_ref, b_ref, x_ref, o_ref):
    @pl.when(pl.program_id(2) == 0)
    def _():
        o_ref[...] = x_ref[...]

    o_ref[...] += jnp.dot(a_ref[...], b_ref[...], preferred_element_type=F32)


def _ff2(a, b, x, tm=1024, tn=1024, tk=4096):
    s, kdim = a.shape
    n = b.shape[1]
    return pl.pallas_call(
        _ff2_kernel,
        grid=(s // tm, n // tn, kdim // tk),
        in_specs=[pl.BlockSpec((tm, tk), lambda i, j, k: (i, k)),
                  pl.BlockSpec((tk, tn), lambda i, j, k: (k, j)),
                  pl.BlockSpec((tm, tn), lambda i, j, k: (i, j))],
        out_specs=pl.BlockSpec((tm, tn), lambda i, j, k: (i, j)),
        out_shape=jax.ShapeDtypeStruct((s, n), F32),
        compiler_params=_params("parallel", "parallel", "arbitrary"),
        name="ff2",
    )(a, b, x)


def _lambda_init(layer_idx):
    return 0.8 - 0.6 * math.exp(-0.3 * layer_idx)


def kernel(x, norm_mix, w_in, b_gate, w_proj_a, w_proj_b, w_out, lambda_q1, lambda_k1,
           lambda_q2, lambda_k2, subln_g, norm_mlp, w_ff1, w_ff2, norm_final):
    batch, seq, d = x.shape
    depth = w_in.shape[0]
    tables = _rope_tables(seq)
    outs = []
    for b in range(batch):
        xb = x.reshape(seq, d) if batch == 1 else x[b]
        for l in range(depth):
            w_in_l = w_in[l]
            h = _rmsnorm(xb, norm_mix[l], BF16)
            qkb = _proj_bqk(h, w_in_l, tables)
            vb = _proj_bv(h, w_in_l)
            lam_init = _lambda_init(l)
            whole = lambda w: (w, [(0, w.shape[1])])
            yb, wa_in_bf, wg_bf, wb_bf, wo_bf, w1_bf, w2_bf = _attn_b(
                qkb, vb, (lambda_q1[l], lambda_k1[l], lambda_q2[l], lambda_k2[l]),
                subln_g[l], lam_init,
                [(w_in_l, [(COL_QA, 3 * A_WIDTH), (COL_GATE, 2 * D_MODEL)]),
                 whole(w_proj_b[l]), whole(w_out[l]), whole(w_ff1[l]), whole(w_ff2[l])])
            groups = []
            for g in range(N_GROUPS):
                groups.append(_attn_a(_proj_a(h, wa_in_bf, tables, g)))
            ya = _mix_a(groups, seq)
            gates = _proj_gate(h, wg_bf, b_gate[l])
            merged = _merge(ya, yb, w_proj_a[l], wb_bf, gates)
            xb, xg, r2 = _out_proj(merged, wo_bf, xb, norm_mlp[l])
            u = _ff1(xg, r2, w1_bf)
            xb = _ff2(u, w2_bf, xb)
        outs.append(_rmsnorm(xb, norm_final, x.dtype))
    if batch == 1:
        return outs[0].reshape(1, seq, d)
    return jnp.stack(outs, axis=0)
```

```python
import functools
import math

import jax
import jax.numpy as jnp
from jax import lax
from jax.experimental import pallas as pl
from jax.experimental.pallas import tpu as pltpu

D_MODEL = 4096
HEAD_DIM = 128
DIL_PAIRS = ((128, 1), (512, 4), (2048, 16))
N_GROUPS = len(DIL_PAIRS)
HEADS_PER_GROUP = 4
GROUP_COLS = HEADS_PER_GROUP * HEAD_DIM
A_WIDTH = N_GROUPS * GROUP_COLS
A_OUT = GROUP_COLS
B_HEADS = 8
B_QK = B_HEADS * 2 * HEAD_DIM
B_V = B_HEADS * 2 * HEAD_DIM
ROPE_THETA = 500000.0
ROT_DIM = HEAD_DIM // 4
ROT_HALF = ROT_DIM // 2
EPS = 1e-6
QK_SCALE = 1.0 / math.sqrt(HEAD_DIM)
LOG2_E = math.log2(math.e)
WINDOW_STEPS = 128
BF16_SUBLANES = 16
ONES_ROWS = BF16_SUBLANES

COL_QA = 0
COL_QB = 3 * A_WIDTH
COL_VB = COL_QB + 2 * B_QK
COL_GATE = COL_VB + B_V

VMEM_LIMIT_BYTES = 56 * 1024 * 1024

F32 = jnp.float32
BF16 = jnp.bfloat16


def _params(*semantics):
    return pltpu.CompilerParams(dimension_semantics=semantics,
                                vmem_limit_bytes=VMEM_LIMIT_BYTES)


def _rmsnorm_kernel(x_ref, g_ref, o_ref):
    x = x_ref[...]
    ms = jnp.mean(x * x, axis=-1, keepdims=True)
    o_ref[...] = ((x * lax.rsqrt(ms + EPS)) * g_ref[...]).astype(o_ref.dtype)


def _rmsnorm(x, g, out_dtype, rows=512):
    s, d = x.shape
    return pl.pallas_call(
        _rmsnorm_kernel,
        grid=(s // rows,),
        in_specs=[pl.BlockSpec((rows, d), lambda i: (i, 0)),
                  pl.BlockSpec((1, d), lambda i: (0, 0))],
        out_specs=pl.BlockSpec((rows, d), lambda i: (i, 0)),
        out_shape=jax.ShapeDtypeStruct((s, d), out_dtype),
        compiler_params=_params("parallel"),
        name="rmsnorm",
    )(x, g.reshape(1, d))


def _rope_tables(seq):
    pos = jnp.arange(seq, dtype=F32)
    inv = ROPE_THETA ** (-jnp.arange(0, ROT_DIM, 2, dtype=F32) / ROT_DIM)
    ang = pos[:, None] * inv[None, :]
    cos, sin = jnp.cos(ang), jnp.sin(ang)
    ones = jnp.ones((seq, HEAD_DIM - ROT_DIM), F32)
    zeros = jnp.zeros((seq, HEAD_DIM - ROT_DIM), F32)
    zh = jnp.zeros((seq, ROT_HALF), F32)
    cos_t = jnp.concatenate([cos, cos, ones], axis=1)
    sin_lo = jnp.concatenate([-sin, zh, zeros], axis=1)
    sin_hi = jnp.concatenate([zh, sin, zeros], axis=1)
    rot = jnp.stack([cos_t, sin_lo, sin_hi])
    ident = jnp.stack([jnp.ones_like(cos_t), jnp.zeros_like(cos_t), jnp.zeros_like(cos_t)])
    return jnp.stack([rot, ident])


def _rope_head(y, tab_ref, scale):
    x2 = pltpu.roll(y, HEAD_DIM - ROT_HALF, 1)
    x1 = pltpu.roll(y, ROT_HALF, 1)
    return (y * tab_ref[0, 0] + x2 * tab_ref[0, 1] + x1 * tab_ref[0, 2]) * scale


def _store_product(a_ref, w_ref, acc_ref, *_):
    acc_ref[...] = jnp.dot(a_ref[...], w_ref[...].astype(BF16), preferred_element_type=F32)


def _lagged_kernel(a_ref, w_ref, *refs, n_ep, n_out, n_tiles, n_j, epilogue):
    ep_refs = refs[:n_ep]
    out_refs = refs[n_ep:n_ep + n_out]
    scratch = refs[n_ep + n_out:]
    t = pl.program_id(0)
    jp = jnp.maximum(t - 1, 0) % n_j

    @pl.when(t == 0)
    def _():
        _store_product(a_ref, w_ref, *scratch)

    @pl.when(jnp.logical_and(t > 0, t < n_tiles))
    def _():
        epilogue(jp, *ep_refs, *out_refs, *scratch)
        _store_product(a_ref, w_ref, *scratch)

    @pl.when(t == n_tiles)
    def _():
        epilogue(jp, *ep_refs, *out_refs, *scratch)


def _lagged_matmul(name, epilogue, *, n_i, n_j, mm_args, mm_specs, ep_args, ep_specs,
                   out_shapes, out_specs, scratch_shapes):
    n_tiles = n_i * n_j

    def cur(f):
        return lambda t: f(*divmod(jnp.minimum(t, n_tiles - 1), n_j))

    def prev(f):
        return lambda t: f(*divmod(jnp.maximum(t - 1, 0), n_j))

    in_specs = ([pl.BlockSpec(shape, cur(f)) for shape, f in mm_specs]
                + [pl.BlockSpec(shape, prev(f)) for shape, f in ep_specs])
    return pl.pallas_call(
        functools.partial(_lagged_kernel, n_ep=len(ep_args), n_out=len(out_shapes),
                          n_tiles=n_tiles, n_j=n_j, epilogue=epilogue),
        grid=(n_tiles + 1,),
        in_specs=in_specs,
        out_specs=[pl.BlockSpec(shape, prev(f)) for shape, f in out_specs],
        out_shape=out_shapes,
        scratch_shapes=scratch_shapes,
        compiler_params=_params("arbitrary"),
        name=name,
    )(*mm_args, *ep_args)


def _proj_a_epilogue(jp, tab_ref, o_ref, acc_ref, y_ref, *, dilation):
    tm = acc_ref.shape[0]
    n = tm // dilation
    scale = jnp.where(jp == 0, QK_SCALE, 1.0).astype(F32)
    for hh in range(HEADS_PER_GROUP):
        y = _rope_head(acc_ref[:, hh * HEAD_DIM:(hh + 1) * HEAD_DIM], tab_ref, scale)
        if dilation == 1:
            o_ref[0, hh, 0] = y.astype(o_ref.dtype)
            continue
        y_ref[hh] = y
        for rho in range(dilation):
            o_ref[0, hh, rho] = y_ref[hh, pl.ds(rho, n, stride=dilation), :].astype(o_ref.dtype)


def _proj_a(h, w_in, tables, group, tm=1024):
    s, d = h.shape
    dilation = DIL_PAIRS[group][1]
    sub_len = s // dilation
    blocks_per_section = A_WIDTH // GROUP_COLS
    return _lagged_matmul(
        f"proj_a_g{group}", functools.partial(_proj_a_epilogue, dilation=dilation),
        n_i=s // tm, n_j=3,
        mm_args=[h, w_in],
        mm_specs=[((tm, d), lambda i, j: (i, 0)),
                  ((d, GROUP_COLS), lambda i, j: (0, j * blocks_per_section + group))],
        ep_args=[tables],
        ep_specs=[((1, 3, tm, HEAD_DIM), lambda i, j: (j // 2, 0, i, 0))],
        out_shapes=[jax.ShapeDtypeStruct((3, HEADS_PER_GROUP, dilation, sub_len, HEAD_DIM), BF16)],
        out_specs=[((1, HEADS_PER_GROUP, dilation, tm // dilation, HEAD_DIM),
                    lambda i, j: (j, 0, 0, i, 0))],
        scratch_shapes=[pltpu.VMEM((tm, GROUP_COLS), F32),
                        pltpu.VMEM((HEADS_PER_GROUP, tm, HEAD_DIM), F32)])[0]


def _proj_bqk_epilogue(jp, tab_ref, o_ref, acc_ref, *, q_tiles):
    scale = jnp.where(jp < q_tiles, QK_SCALE * LOG2_E, 1.0).astype(F32)
    for hh in range(o_ref.shape[0]):
        y = _rope_head(acc_ref[:, hh * HEAD_DIM:(hh + 1) * HEAD_DIM], tab_ref, scale)
        o_ref[hh] = y.astype(o_ref.dtype)


def _proj_bqk(h, w_in, tables, tm=1024, tn=512):
    s, d = h.shape
    heads_per_tile = tn // HEAD_DIM
    n = B_QK + B_QK
    return _lagged_matmul(
        "proj_b_qk", functools.partial(_proj_bqk_epilogue, q_tiles=B_QK // tn),
        n_i=s // tm, n_j=n // tn,
        mm_args=[h, w_in],
        mm_specs=[((tm, d), lambda i, j: (i, 0)),
                  ((d, tn), lambda i, j: (0, COL_QB // tn + j))],
        ep_args=[tables],
        ep_specs=[((1, 3, tm, HEAD_DIM), lambda i, j: (0, 0, i, 0))],
        out_shapes=[jax.ShapeDtypeStruct((n // HEAD_DIM, s, HEAD_DIM), BF16)],
        out_specs=[((heads_per_tile, tm, HEAD_DIM), lambda i, j: (j, i, 0))],
        scratch_shapes=[pltpu.VMEM((tm, tn), F32)])[0]


def _proj_bv_kernel(h_ref, w_ref, o_ref):
    acc = jnp.dot(h_ref[...], w_ref[...].astype(BF16), preferred_element_type=F32)
    heads, blocks, rows, tk = o_ref.shape
    width = rows - ONES_ROWS
    for hv in range(heads):
        vt = acc[:, hv * width:(hv + 1) * width].T
        for c in range(blocks):
            o_ref[hv, c, :width, :] = vt[:, c * tk:(c + 1) * tk].astype(o_ref.dtype)
            o_ref[hv, c, width:, :] = jnp.ones((ONES_ROWS, tk), o_ref.dtype)


def _proj_bv(h, w_in, tm=1024, tn=512, tk=512):
    s, d = h.shape
    width = 2 * HEAD_DIM
    rows = width + ONES_ROWS
    return pl.pallas_call(
        _proj_bv_kernel,
        grid=(s // tm, B_V // tn),
        in_specs=[pl.BlockSpec((tm, d), lambda i, j: (i, 0)),
                  pl.BlockSpec((d, tn), lambda i, j: (0, COL_VB // tn + j))],
        out_specs=pl.BlockSpec((tn // width, tm // tk, rows, tk), lambda i, j: (j, i, 0, 0)),
        out_shape=jax.ShapeDtypeStruct((B_HEADS, s // tk, rows, tk), BF16),
        compiler_params=_params("parallel", "arbitrary"),
        name="proj_b_v",
    )(h, w_in)


def _proj_gate_kernel(h_ref, w_ref, b_ref, o_ref):
    z = jnp.dot(h_ref[...], w_ref[...], preferred_element_type=F32) + b_ref[...]
    o_ref[...] = (0.5 * jnp.tanh(0.5 * z) + 0.5).astype(o_ref.dtype)


def _proj_gate(h, w_gate, b_gate, tm=1024, tn=1024):
    s, d = h.shape
    n = w_gate.shape[1]
    return pl.pallas_call(
        _proj_gate_kernel,
        grid=(s // tm, n // tn),
        in_specs=[pl.BlockSpec((tm, d), lambda i, j: (i, 0)),
                  pl.BlockSpec((d, tn), lambda i, j: (0, j)),
                  pl.BlockSpec((1, tn), lambda i, j: (0, j))],
        out_specs=pl.BlockSpec((tm, tn), lambda i, j: (i, j)),
        out_shape=jax.ShapeDtypeStruct((s, n), BF16),
        compiler_params=_params("parallel", "arbitrary"),
        name="proj_gate",
    )(h, w_gate, b_gate.reshape(1, n))


def _attn_a_kernel(q_ref, kc_ref, kp_ref, vc_ref, vp_ref, o_ref, lse_ref, kk_ref, vv_ref,
                   *, sub_len):
    w = WINDOW_STEPS
    t = q_ref.shape[1]
    c = pl.program_id(0)
    kk_ref[0:w] = kp_ref[0]
    kk_ref[w:] = kc_ref[0]
    vv_ref[0:w] = vp_ref[0]
    vv_ref[w:] = vc_ref[0]
    qp = lax.broadcasted_iota(jnp.int32, (w, 2 * w), 0)
    kp = lax.broadcasted_iota(jnp.int32, (w, 2 * w), 1)
    band = (kp >= qp) & (kp <= qp + w)
    band_start = band & (kp >= w)
    if t >= sub_len:
        masks = [band_start if (b * w) % sub_len == 0 else band for b in range(t // w)]
    else:
        is_start = (c % (sub_len // t)) == 0
        masks = [band & (kp >= jnp.where(is_start, w, 0))] + [band] * (t // w - 1)
    for b in range(t // w):
        q = q_ref[0, b * w:(b + 1) * w]
        k = kk_ref[b * w:(b + 2) * w]
        v = vv_ref[b * w:(b + 2) * w]
        s = lax.dot_general(q, k, (((1,), (1,)), ((), ())), preferred_element_type=F32)
        s = jnp.where(masks[b], s, -jnp.inf)
        m = jnp.max(s, axis=-1, keepdims=True)
        p = jnp.exp(s - m)
        l = jnp.sum(p, axis=-1, keepdims=True)
        o = jnp.dot(p.astype(v.dtype), v, preferred_element_type=F32)
        o_ref[b * w:(b + 1) * w] = o / l
        lse_ref[b * w:(b + 1) * w] = jnp.broadcast_to(m + jnp.log(l), (w, HEAD_DIM))


def _attn_a(qkv, t=4096):
    _, heads, dilation, sub_len, hd = qkv.shape
    rows = heads * dilation * sub_len
    flat = qkv.reshape(3, rows, hd)
    w = WINDOW_STEPS
    assert t % sub_len == 0 or sub_len % t == 0
    bpc = t // w
    cur = lambda sec: pl.BlockSpec((1, t, hd), lambda c: (sec, c, 0))
    prev = lambda sec: pl.BlockSpec((1, w, hd), lambda c: (sec, jnp.maximum(c * bpc - 1, 0), 0))
    out_spec = pl.BlockSpec((t, hd), lambda c: (c, 0))
    out, lse = pl.pallas_call(
        functools.partial(_attn_a_kernel, sub_len=sub_len),
        grid=(rows // t,),
        in_specs=[cur(0), cur(1), prev(1), cur(2), prev(2)],
        out_specs=[out_spec, out_spec],
        out_shape=[jax.ShapeDtypeStruct((rows, hd), F32)] * 2,
        scratch_shapes=[pltpu.VMEM((t + w, hd), BF16)] * 2,
        compiler_params=_params("parallel"),
        name=f"attn_a_r{dilation}",
    )(flat, flat, flat, flat, flat)
    shape = (heads, dilation, sub_len, hd)
    return out.reshape(shape), lse.reshape(shape)


def _mix_a_kernel(o0_ref, l0_ref, o1_ref, l1_ref, o2_ref, l2_ref, y_ref, nat_ref):
    tm = y_ref.shape[0]
    refs = ((o0_ref, l0_ref), (o1_ref, l1_ref), (o2_ref, l2_ref))
    r_max = DIL_PAIRS[-1][1]
    n = tm // r_max
    for hh in range(HEADS_PER_GROUP):
        for rho in range(r_max):
            outs, lses = [], []
            for (o_ref, l_ref), (_, r) in zip(refs, DIL_PAIRS):
                step = r_max // r
                rows = pl.ds(rho // r, n, stride=step) if step > 1 else pl.ds(0, n)
                outs.append(o_ref[hh, rho % r, rows, :])
                lses.append(l_ref[hh, rho % r, rows, :])
            m = jnp.maximum(jnp.maximum(lses[0], lses[1]), lses[2])
            ws = [jnp.exp(l - m) for l in lses]
            num = ws[0] * outs[0] + ws[1] * outs[1] + ws[2] * outs[2]
            den = ws[0] + ws[1] + ws[2]
            nat_ref[hh, pl.ds(rho, n, stride=r_max), :] = num / den
        y_ref[:, hh * HEAD_DIM:(hh + 1) * HEAD_DIM] = nat_ref[hh].astype(y_ref.dtype)


def _mix_a(group_outs, seq, tm=512):
    in_specs, args = [], []
    for (o, l), (_, r) in zip(group_outs, DIL_PAIRS):
        spec = pl.BlockSpec((HEADS_PER_GROUP, r, tm // r, HEAD_DIM), lambda i: (0, 0, i, 0))
        in_specs += [spec, spec]
        args += [o, l]
    return pl.pallas_call(
        _mix_a_kernel,
        grid=(seq // tm,),
        in_specs=in_specs,
        out_specs=pl.BlockSpec((tm, A_OUT), lambda i: (i, 0)),
        out_shape=jax.ShapeDtypeStruct((seq, A_OUT), BF16),
        scratch_shapes=[pltpu.VMEM((HEADS_PER_GROUP, tm, HEAD_DIM), F32)],
        compiler_params=_params("parallel"),
        name="mix_a",
    )(*args)


def _attn_b_kernel(lq1_ref, lk1_ref, lq2_ref, lk2_ref, g_ref, q_ref, qn_ref, k_ref, vt_ref,
                   *rest, lam_init, cast_cols):
    n_in = len(cast_cols)
    n_out = sum(len(cols) for cols in cast_cols)
    w_refs = rest[:n_in]
    o_ref = rest[n_in]
    wo_refs = rest[n_in + 1:n_in + 1 + n_out]
    m_ref, acc_ref, st_ref, mb_ref = rest[n_in + 1 + n_out:]
    tq = q_ref.shape[1]
    tk = vt_ref.shape[3]
    width = vt_ref.shape[2] - ONES_ROWS
    qi = pl.program_id(1)
    m_ref[...] = jnp.full(m_ref.shape, -jnp.inf, F32)
    acc_ref[...] = jnp.zeros(acc_ref.shape, F32)

    def scores_to(slot, kv, q_src=q_ref):
        start = pl.multiple_of(kv * tk, tk)
        for i in range(2):
            st = lax.dot_general(
                k_ref[i, pl.ds(start, tk), :], q_src[i], (((1,), (1,)), ((), ())),
                preferred_element_type=F32)
            st_ref[slot, i] = st
            mb_ref[slot, i] = jnp.max(st, axis=0, keepdims=True)

    def softmax_pv(slot, kv, masked):
        vt = vt_ref[0, kv]
        for i in range(2):
            st = st_ref[slot, i]
            if masked:
                kpos = kv * tk + lax.broadcasted_iota(jnp.int32, st.shape, 0)
                qpos = qi * tq + lax.broadcasted_iota(jnp.int32, st.shape, 1)
                st = jnp.where(kpos <= qpos, st, -jnp.inf)
                m_blk = jnp.max(st, axis=0, keepdims=True)
            else:
                m_blk = mb_ref[slot, i]
            m_prev = m_ref[i]
            m_new = jnp.maximum(m_prev, m_blk)
            alpha = jnp.exp2(m_prev - m_new)
            p = jnp.exp2((st - m_new).astype(vt.dtype))
            acc_ref[i] = alpha * acc_ref[i] + jnp.dot(vt, p, preferred_element_type=F32)
            m_ref[i] = m_new

    @pl.when(qi == 0)
    def _():
        scores_to(0, 0)

    def pair(j):
        scores_to(1, 2 * j + 1)
        softmax_pv(0, 2 * j, masked=False)
        scores_to(0, 2 * j + 2)
        softmax_pv(1, 2 * j + 1, masked=False)

    def two_pairs(j, carry):
        pair(2 * j)
        pair(2 * j + 1)
        return carry

    n_pairs = qi // 2
    lax.fori_loop(0, n_pairs // 2, two_pairs, 0)

    @pl.when(n_pairs % 2 == 1)
    def _():
        pair(n_pairs - 1)

    @pl.when(qi % 2 == 0)
    def _():
        softmax_pv(0, qi, masked=True)
        scores_to(0, 0, qn_ref)

    @pl.when(qi % 2 == 1)
    def _():
        scores_to(1, qi)
        softmax_pv(0, qi - 1, masked=False)
        scores_to(0, 0, qn_ref)
        softmax_pv(1, qi, masked=True)

    lam = (jnp.exp(jnp.sum(lq1_ref[...] * lk1_ref[...], axis=-1, keepdims=True))
           - jnp.exp(jnp.sum(lq2_ref[...] * lk2_ref[...], axis=-1, keepdims=True))
           + lam_init)
    inv_l = [1.0 / acc_ref[i, width:width + 1, :] for i in range(2)]
    ot = acc_ref[0, :width, :] * inv_l[0] - lam * (acc_ref[1, :width, :] * inv_l[1])
    ms = jnp.mean(ot * ot, axis=0, keepdims=True)
    yt = (ot * lax.rsqrt(ms + EPS)) * g_ref[...]
    o_ref[...] = (yt * (1.0 - lam_init)).T.astype(o_ref.dtype)

    wo_iter = iter(wo_refs)
    for w_ref, cols in zip(w_refs, cast_cols):
        for start, size in cols:
            wo_ref = next(wo_iter)
            wo_ref[...] = w_ref[:, start:start + size].astype(wo_ref.dtype)


def _attn_b(qk, vt, lam_vecs, subln_g, lam_init, cast_weights):
    _, s, hd = qk.shape
    _, n_kv, rows, tk = vt.shape
    width = rows - ONES_ROWS
    tq = tk
    n_q = s // tq
    n_steps = B_HEADS * n_q
    vec = pl.BlockSpec((1, hd), lambda h, i: (0, 0))
    chunk = lambda h, i: (h * n_q + i, 0)
    w_specs, wo_specs, wo_shapes = [], [], []
    for w, cols in cast_weights:
        assert w.shape[0] % (n_steps * BF16_SUBLANES) == 0, w.shape
        rows_per_step = w.shape[0] // n_steps
        w_specs.append(pl.BlockSpec((rows_per_step, w.shape[1]), chunk))
        for _, size in cols:
            wo_specs.append(pl.BlockSpec((rows_per_step, size), chunk))
            wo_shapes.append(jax.ShapeDtypeStruct((w.shape[0], size), BF16))
    return pl.pallas_call(
        functools.partial(_attn_b_kernel, lam_init=lam_init,
                          cast_cols=tuple(tuple(cols) for _, cols in cast_weights)),
        grid=(B_HEADS, n_q),
        in_specs=[vec, vec, vec, vec,
                  pl.BlockSpec((width, 1), lambda h, i: (0, 0)),
                  pl.BlockSpec((2, tq, hd), lambda h, i: (h, i, 0)),
                  pl.BlockSpec((2, tq, hd), lambda h, i: (h, jnp.minimum(i + 1, n_q - 1), 0)),
                  pl.BlockSpec((2, s, hd), lambda h, i: (B_HEADS + h, 0, 0)),
                  pl.BlockSpec((1, n_kv, rows, tk), lambda h, i: (h, 0, 0, 0))] + w_specs,
        out_specs=[pl.BlockSpec((tq, width), lambda h, i: (i, h))] + wo_specs,
        out_shape=[jax.ShapeDtypeStruct((s, B_HEADS * width), BF16)] + wo_shapes,
        scratch_shapes=[pltpu.VMEM((2, 1, tq), F32),
                        pltpu.VMEM((2, rows, tq), F32),
                        pltpu.VMEM((2, 2, tk, tq), F32),
                        pltpu.VMEM((2, 2, 1, tq), F32)],
        compiler_params=_params("arbitrary", "arbitrary"),
        name="attn_b",
    )(*[x.reshape(1, hd) for x in lam_vecs], subln_g.reshape(width, 1), qk, qk, qk, vt,
      *[w for w, _ in cast_weights])


def _merge_kernel(ya_ref, yb_ref, wa_ref, wb_ref, ga_ref, gb_ref, o_ref):
    pa = jnp.dot(ya_ref[...], wa_ref[...].astype(BF16), preferred_element_type=F32)
    pb = jnp.dot(yb_ref[...], wb_ref[...], preferred_element_type=F32)
    o_ref[...] = (ga_ref[...].astype(F32) * pa + gb_ref[...].astype(F32) * pb).astype(o_ref.dtype)


def _merge(ya, yb, wa, wb, gates, tm=1024, tn=1024):
    s = ya.shape[0]
    n = wa.shape[1]
    return pl.pallas_call(
        _merge_kernel,
        grid=(s // tm, n // tn),
        in_specs=[pl.BlockSpec((tm, ya.shape[1]), lambda i, j: (i, 0)),
                  pl.BlockSpec((tm, yb.shape[1]), lambda i, j: (i, 0)),
                  pl.BlockSpec((wa.shape[0], tn), lambda i, j: (0, j)),
                  pl.BlockSpec((wb.shape[0], tn), lambda i, j: (0, j)),
                  pl.BlockSpec((tm, tn), lambda i, j: (i, j)),
                  pl.BlockSpec((tm, tn), lambda i, j: (i, n // tn + j))],
        out_specs=pl.BlockSpec((tm, tn), lambda i, j: (i, j)),
        out_shape=jax.ShapeDtypeStruct((s, n), BF16),
        compiler_params=_params("parallel", "arbitrary"),
        name="merge",
    )(ya, yb, wa, wb, gates, gates)


def _out_proj_kernel(a_ref, b_ref, x_ref, g_ref, x1_ref, xg_ref, r2_ref, *, n_cols):
    j = pl.program_id(1)
    x1 = x_ref[...] + jnp.dot(a_ref[...], b_ref[...], preferred_element_type=F32)
    x1_ref[...] = x1
    xg_ref[...] = (x1 * g_ref[...]).astype(xg_ref.dtype)
    part = jnp.broadcast_to(jnp.sum(x1 * x1, axis=-1, keepdims=True), r2_ref.shape)

    @pl.when(j == 0)
    def _():
        r2_ref[...] = part

    @pl.when(j > 0)
    def _():
        r2_ref[...] += part

    @pl.when(j == n_cols // x_ref.shape[1] - 1)
    def _():
        r2_ref[...] = 1.0 / (r2_ref[...] * (1.0 / n_cols) + EPS)


def _out_proj(a, b, x, g, tm=1024, tn=512):
    s, kdim = a.shape
    n = b.shape[1]
    lanes = HEAD_DIM
    return pl.pallas_call(
        functools.partial(_out_proj_kernel, n_cols=n),
        grid=(s // tm, n // tn),
        in_specs=[pl.BlockSpec((tm, kdim), lambda i, j: (i, 0)),
                  pl.BlockSpec((kdim, tn), lambda i, j: (0, j)),
                  pl.BlockSpec((tm, tn), lambda i, j: (i, j)),
                  pl.BlockSpec((1, tn), lambda i, j: (0, j))],
        out_specs=[pl.BlockSpec((tm, tn), lambda i, j: (i, j)),
                   pl.BlockSpec((tm, tn), lambda i, j: (i, j)),
                   pl.BlockSpec((tm, lanes), lambda i, j: (i, 0))],
        out_shape=[jax.ShapeDtypeStruct((s, n), F32),
                   jax.ShapeDtypeStruct((s, n), BF16),
                   jax.ShapeDtypeStruct((s, lanes), F32)],
        compiler_params=_params("parallel", "arbitrary"),
        name="out_proj",
    )(a, b, x, g.reshape(1, n))


def _ff1_kernel(xg_ref, r2_ref, b_ref, o_ref):
    lanes = r2_ref.shape[1]
    u = jnp.maximum(jnp.dot(xg_ref[...], b_ref[...], preferred_element_type=F32), 0.0)
    for c in range(o_ref.shape[1] // lanes):
        uc = u[:, c * lanes:(c + 1) * lanes]
        o_ref[:, c * lanes:(c + 1) * lanes] = (uc * uc * r2_ref[...]).astype(o_ref.dtype)


def _ff1(xg, r2, b, tm=1024, tn=1024):
    s, kdim = xg.shape
    n = b.shape[1]
    return pl.pallas_call(
        _ff1_kernel,
        grid=(s // tm, n // tn),
        in_specs=[pl.BlockSpec((tm, kdim), lambda i, j: (i, 0)),
                  pl.BlockSpec((tm, r2.shape[1]), lambda i, j: (i, 0)),
                  pl.BlockSpec((kdim, tn), lambda i, j: (0, j))],
        out_specs=pl.BlockSpec((tm, tn), lambda i, j: (i, j)),
        out_shape=jax.ShapeDtypeStruct((s, n), BF16),
        compiler_params=_params("parallel", "arbitrary"),
        name="ff1",
    )(xg, r2, b)


def _ff2_kernel(a_ref, b_ref, x_ref, o_ref):
    @pl.when(pl.program_id(2) == 0)
    def _():
        o_ref[...] = x_ref[...]

    o_ref[...] += jnp.dot(a_ref[...], b_ref[...], preferred_element_type=F32)


def _ff2(a, b, x, tm=1024, tn=1024, tk=4096):
    s, kdim = a.shape
    n = b.shape[1]
    return pl.pallas_call(
        _ff2_kernel,
        grid=(s // tm, n // tn, kdim // tk),
        in_specs=[pl.BlockSpec((tm, tk), lambda i, j, k: (i, k)),
                  pl.BlockSpec((tk, tn), lambda i, j, k: (k, j)),
                  pl.BlockSpec((tm, tn), lambda i, j, k: (i, j))],
        out_specs=pl.BlockSpec((tm, tn), lambda i, j, k: (i, j)),
        out_shape=jax.ShapeDtypeStruct((s, n), F32),
        compiler_params=_params("parallel", "parallel", "arbitrary"),
        name="ff2",
    )(a, b, x)


def _lambda_init(layer_idx):
    return 0.8 - 0.6 * math.exp(-0.3 * layer_idx)


def kernel(x, norm_mix, w_in, b_gate, w_proj_a, w_proj_b, w_out, lambda_q1, lambda_k1,
           lambda_q2, lambda_k2, subln_g, norm_mlp, w_ff1, w_ff2, norm_final):
    batch, seq, d = x.shape
    depth = w_in.shape[0]
    tables = _rope_tables(seq)
    outs = []
    for b in range(batch):
        xb = x.reshape(seq, d) if batch == 1 else x[b]
        for l in range(depth):
            w_in_l = w_in[l]
            h = _rmsnorm(xb, norm_mix[l], BF16)
            qkb = _proj_bqk(h, w_in_l, tables)
            vb = _proj_bv(h, w_in_l)
            lam_init = _lambda_init(l)
            whole = lambda w: (w, [(0, w.shape[1])])
            yb, wa_in_bf, wg_bf, wb_bf, wo_bf, w1_bf, w2_bf = _attn_b(
                qkb, vb, (lambda_q1[l], lambda_k1[l], lambda_q2[l], lambda_k2[l]),
                subln_g[l], lam_init,
                [(w_in_l, [(COL_QA, 3 * A_WIDTH), (COL_GATE, 2 * D_MODEL)]),
                 whole(w_proj_b[l]), whole(w_out[l]), whole(w_ff1[l]), whole(w_ff2[l])])
            groups = []
            for g in range(N_GROUPS):
                groups.append(_attn_a(_proj_a(h, wa_in_bf, tables, g)))
            ya = _mix_a(groups, seq)
            gates = _proj_gate(h, wg_bf, b_gate[l])
            merged = _merge(ya, yb, w_proj_a[l], wb_bf, gates)
            xb, xg, r2 = _out_proj(merged, wo_bf, xb, norm_mlp[l])
            u = _ff1(xg, r2, w1_bf)
            xb = _ff2(u, w2_bf, xb)
        outs.append(_rmsnorm(xb, norm_final, x.dtype))
    if batch == 1:
        return outs[0].reshape(1, seq, d)
    return jnp.stack(outs, axis=0)
```

```python
import functools
import math

import jax
import jax.numpy as jnp
from jax import lax
from jax.experimental import pallas as pl
from jax.experimental.pallas import tpu as pltpu

D_MODEL = 4096
HEAD_DIM = 128
DIL_PAIRS = ((128, 1), (512, 4), (2048, 16))
N_GROUPS = len(DIL_PAIRS)
HEADS_PER_GROUP = 4
GROUP_COLS = HEADS_PER_GROUP * HEAD_DIM
A_WIDTH = N_GROUPS * GROUP_COLS
A_OUT = GROUP_COLS
B_HEADS = 8
B_QK = B_HEADS * 2 * HEAD_DIM
B_V = B_HEADS * 2 * HEAD_DIM
ROPE_THETA = 500000.0
ROT_DIM = HEAD_DIM // 4
ROT_HALF = ROT_DIM // 2
EPS = 1e-6
QK_SCALE = 1.0 / math.sqrt(HEAD_DIM)
LOG2_E = math.log2(math.e)
WINDOW_STEPS = 128
BF16_SUBLANES = 16
ONES_ROWS = BF16_SUBLANES

COL_QA = 0
COL_QB = 3 * A_WIDTH
COL_VB = COL_QB + 2 * B_QK
COL_GATE = COL_VB + B_V

VMEM_LIMIT_BYTES = 56 * 1024 * 1024

F32 = jnp.float32
BF16 = jnp.bfloat16


def _params(*semantics):
    return pltpu.CompilerParams(dimension_semantics=semantics,
                                vmem_limit_bytes=VMEM_LIMIT_BYTES)


def _rmsnorm_kernel(x_ref, g_ref, o_ref):
    x = x_ref[...]
    ms = jnp.mean(x * x, axis=-1, keepdims=True)
    o_ref[...] = ((x * lax.rsqrt(ms + EPS)) * g_ref[...]).astype(o_ref.dtype)


def _rmsnorm(x, g, out_dtype, rows=512):
    s, d = x.shape
    return pl.pallas_call(
        _rmsnorm_kernel,
        grid=(s // rows,),
        in_specs=[pl.BlockSpec((rows, d), lambda i: (i, 0)),
                  pl.BlockSpec((1, d), lambda i: (0, 0))],
        out_specs=pl.BlockSpec((rows, d), lambda i: (i, 0)),
        out_shape=jax.ShapeDtypeStruct((s, d), out_dtype),
        compiler_params=_params("parallel"),
        name="rmsnorm",
    )(x, g.reshape(1, d))


def _rope_tables(seq):
    pos = jnp.arange(seq, dtype=F32)
    inv = ROPE_THETA ** (-jnp.arange(0, ROT_DIM, 2, dtype=F32) / ROT_DIM)
    ang = pos[:, None] * inv[None, :]
    cos, sin = jnp.cos(ang), jnp.sin(ang)
    ones = jnp.ones((seq, HEAD_DIM - ROT_DIM), F32)
    zeros = jnp.zeros((seq, HEAD_DIM - ROT_DIM), F32)
    zh = jnp.zeros((seq, ROT_HALF), F32)
    cos_t = jnp.concatenate([cos, cos, ones], axis=1)
    sin_lo = jnp.concatenate([-sin, zh, zeros], axis=1)
    sin_hi = jnp.concatenate([zh, sin, zeros], axis=1)
    rot = jnp.stack([cos_t, sin_lo, sin_hi])
    ident = jnp.stack([jnp.ones_like(cos_t), jnp.zeros_like(cos_t), jnp.zeros_like(cos_t)])
    return jnp.stack([rot, ident])


def _rope_head(y, tab_ref, scale):
    x2 = pltpu.roll(y, HEAD_DIM - ROT_HALF, 1)
    x1 = pltpu.roll(y, ROT_HALF, 1)
    return (y * tab_ref[0, 0] + x2 * tab_ref[0, 1] + x1 * tab_ref[0, 2]) * scale


def _store_product(a_ref, w_ref, acc_ref, *_):
    acc_ref[...] = jnp.dot(a_ref[...], w_ref[...].astype(BF16), preferred_element_type=F32)


def _lagged_kernel(a_ref, w_ref, *refs, n_ep, n_out, n_tiles, n_j, epilogue):
    ep_refs = refs[:n_ep]
    out_refs = refs[n_ep:n_ep + n_out]
    scratch = refs[n_ep + n_out:]
    t = pl.program_id(0)
    jp = jnp.maximum(t - 1, 0) % n_j

    @pl.when(t == 0)
    def _():
        _store_product(a_ref, w_ref, *scratch)

    @pl.when(jnp.logical_and(t > 0, t < n_tiles))
    def _():
        epilogue(jp, *ep_refs, *out_refs, *scratch)
        _store_product(a_ref, w_ref, *scratch)

    @pl.when(t == n_tiles)
    def _():
        epilogue(jp, *ep_refs, *out_refs, *scratch)


def _lagged_matmul(name, epilogue, *, n_i, n_j, mm_args, mm_specs, ep_args, ep_specs,
                   out_shapes, out_specs, scratch_shapes):
    n_tiles = n_i * n_j

    def cur(f):
        return lambda t: f(*divmod(jnp.minimum(t, n_tiles - 1), n_j))

    def prev(f):
        return lambda t: f(*divmod(jnp.maximum(t - 1, 0), n_j))

    in_specs = ([pl.BlockSpec(shape, cur(f)) for shape, f in mm_specs]
                + [pl.BlockSpec(shape, prev(f)) for shape, f in ep_specs])
    return pl.pallas_call(
        functools.partial(_lagged_kernel, n_ep=len(ep_args), n_out=len(out_shapes),
                          n_tiles=n_tiles, n_j=n_j, epilogue=epilogue),
        grid=(n_tiles + 1,),
        in_specs=in_specs,
        out_specs=[pl.BlockSpec(shape, prev(f)) for shape, f in out_specs],
        out_shape=out_shapes,
        scratch_shapes=scratch_shapes,
        compiler_params=_params("arbitrary"),
        name=name,
    )(*mm_args, *ep_args)


def _proj_a_epilogue(jp, tab_ref, o_ref, acc_ref, y_ref, *, dilation):
    tm = acc_ref.shape[0]
    n = tm // dilation
    scale = jnp.where(jp == 0, QK_SCALE, 1.0).astype(F32)
    for hh in range(HEADS_PER_GROUP):
        y = _rope_head(acc_ref[:, hh * HEAD_DIM:(hh + 1) * HEAD_DIM], tab_ref, scale)
        if dilation == 1:
            o_ref[0, hh, 0] = y.astype(o_ref.dtype)
            continue
        y_ref[hh] = y
        for rho in range(dilation):
            o_ref[0, hh, rho] = y_ref[hh, pl.ds(rho, n, stride=dilation), :].astype(o_ref.dtype)


def _proj_a(h, w_in, tables, group, tm=1024):
    s, d = h.shape
    dilation = DIL_PAIRS[group][1]
    sub_len = s // dilation
    blocks_per_section = A_WIDTH // GROUP_COLS
    return _lagged_matmul(
        f"proj_a_g{group}", functools.partial(_proj_a_epilogue, dilation=dilation),
        n_i=s // tm, n_j=3,
        mm_args=[h, w_in],
        mm_specs=[((tm, d), lambda i, j: (i, 0)),
                  ((d, GROUP_COLS), lambda i, j: (0, j * blocks_per_section + group))],
        ep_args=[tables],
        ep_specs=[((1, 3, tm, HEAD_DIM), lambda i, j: (j // 2, 0, i, 0))],
        out_shapes=[jax.ShapeDtypeStruct((3, HEADS_PER_GROUP, dilation, sub_len, HEAD_DIM), BF16)],
        out_specs=[((1, HEADS_PER_GROUP, dilation, tm // dilation, HEAD_DIM),
                    lambda i, j: (j, 0, 0, i, 0))],
        scratch_shapes=[pltpu.VMEM((tm, GROUP_COLS), F32),
                        pltpu.VMEM((HEADS_PER_GROUP, tm, HEAD_DIM), F32)])[0]


def _proj_bqk_epilogue(jp, tab_ref, o_ref, acc_ref, *, q_tiles):
    scale = jnp.where(jp < q_tiles, QK_SCALE * LOG2_E, 1.0).astype(F32)
    for hh in range(o_ref.shape[0]):
        y = _rope_head(acc_ref[:, hh * HEAD_DIM:(hh + 1) * HEAD_DIM], tab_ref, scale)
        o_ref[hh] = y.astype(o_ref.dtype)


def _proj_bqk(h, w_in, tables, tm=1024, tn=512):
    s, d = h.shape
    heads_per_tile = tn // HEAD_DIM
    n = B_QK + B_QK
    return _lagged_matmul(
        "proj_b_qk", functools.partial(_proj_bqk_epilogue, q_tiles=B_QK // tn),
        n_i=s // tm, n_j=n // tn,
        mm_args=[h, w_in],
        mm_specs=[((tm, d), lambda i, j: (i, 0)),
                  ((d, tn), lambda i, j: (0, COL_QB // tn + j))],
        ep_args=[tables],
        ep_specs=[((1, 3, tm, HEAD_DIM), lambda i, j: (0, 0, i, 0))],
        out_shapes=[jax.ShapeDtypeStruct((n // HEAD_DIM, s, HEAD_DIM), BF16)],
        out_specs=[((heads_per_tile, tm, HEAD_DIM), lambda i, j: (j, i, 0))],
        scratch_shapes=[pltpu.VMEM((tm, tn), F32)])[0]


def _proj_bv_kernel(h_ref, w_ref, o_ref):
    acc = jnp.dot(h_ref[...], w_ref[...].astype(BF16), preferred_element_type=F32)
    heads, blocks, rows, tk = o_ref.shape
    width = rows - ONES_ROWS
    for hv in range(heads):
        vt = acc[:, hv * width:(hv + 1) * width].T
        for c in range(blocks):
            o_ref[hv, c, :width, :] = vt[:, c * tk:(c + 1) * tk].astype(o_ref.dtype)
            o_ref[hv, c, width:, :] = jnp.ones((ONES_ROWS, tk), o_ref.dtype)


def _proj_bv(h, w_in, tm=1024, tn=512, tk=512):
    s, d = h.shape
    width = 2 * HEAD_DIM
    rows = width + ONES_ROWS
    return pl.pallas_call(
        _proj_bv_kernel,
        grid=(s // tm, B_V // tn),
        in_specs=[pl.BlockSpec((tm, d), lambda i, j: (i, 0)),
                  pl.BlockSpec((d, tn), lambda i, j: (0, COL_VB // tn + j))],
        out_specs=pl.BlockSpec((tn // width, tm // tk, rows, tk), lambda i, j: (j, i, 0, 0)),
        out_shape=jax.ShapeDtypeStruct((B_HEADS, s // tk, rows, tk), BF16),
        compiler_params=_params("parallel", "arbitrary"),
        name="proj_b_v",
    )(h, w_in)


def _proj_gate_kernel(h_ref, w_ref, b_ref, o_ref):
    z = jnp.dot(h_ref[...], w_ref[...], preferred_element_type=F32) + b_ref[...]
    o_ref[...] = (0.5 * jnp.tanh(0.5 * z) + 0.5).astype(o_ref.dtype)


def _proj_gate(h, w_gate, b_gate, tm=1024, tn=1024):
    s, d = h.shape
    n = w_gate.shape[1]
    return pl.pallas_call(
        _proj_gate_kernel,
        grid=(s // tm, n // tn),
        in_specs=[pl.BlockSpec((tm, d), lambda i, j: (i, 0)),
                  pl.BlockSpec((d, tn), lambda i, j: (0, j)),
                  pl.BlockSpec((1, tn), lambda i, j: (0, j))],
        out_specs=pl.BlockSpec((tm, tn), lambda i, j: (i, j)),
        out_shape=jax.ShapeDtypeStruct((s, n), BF16),
        compiler_params=_params("parallel", "arbitrary"),
        name="proj_gate",
    )(h, w_gate, b_gate.reshape(1, n))


def _attn_a_kernel(q_ref, kc_ref, kp_ref, vc_ref, vp_ref, o_ref, lse_ref, kk_ref, vv_ref,
                   *, sub_len):
    w = WINDOW_STEPS
    t = q_ref.shape[1]
    c = pl.program_id(0)
    kk_ref[0:w] = kp_ref[0]
    kk_ref[w:] = kc_ref[0]
    vv_ref[0:w] = vp_ref[0]
    vv_ref[w:] = vc_ref[0]
    qp = lax.broadcasted_iota(jnp.int32, (w, 2 * w), 0)
    kp = lax.broadcasted_iota(jnp.int32, (w, 2 * w), 1)
    band = (kp >= qp) & (kp <= qp + w)
    band_start = band & (kp >= w)
    if t >= sub_len:
        masks = [band_start if (b * w) % sub_len == 0 else band for b in range(t // w)]
    else:
        is_start = (c % (sub_len // t)) == 0
        masks = [band & (kp >= jnp.where(is_start, w, 0))] + [band] * (t // w - 1)
    for b in range(t // w):
        q = q_ref[0, b * w:(b + 1) * w]
        k = kk_ref[b * w:(b + 2) * w]
        v = vv_ref[b * w:(b + 2) * w]
        s = lax.dot_general(q, k, (((1,), (1,)), ((), ())), preferred_element_type=F32)
        s = jnp.where(masks[b], s, -jnp.inf)
        m = jnp.max(s, axis=-1, keepdims=True)
        p = jnp.exp(s - m)
        l = jnp.sum(p, axis=-1, keepdims=True)
        o = jnp.dot(p.astype(v.dtype), v, preferred_element_type=F32)
        o_ref[b * w:(b + 1) * w] = o / l
        lse_ref[b * w:(b + 1) * w] = jnp.broadcast_to(m + jnp.log(l), (w, HEAD_DIM))


def _attn_a(qkv, t=4096):
    _, heads, dilation, sub_len, hd = qkv.shape
    rows = heads * dilation * sub_len
    flat = qkv.reshape(3, rows, hd)
    w = WINDOW_STEPS
    assert t % sub_len == 0 or sub_len % t == 0
    bpc = t // w
    cur = lambda sec: pl.BlockSpec((1, t, hd), lambda c: (sec, c, 0))
    prev = lambda sec: pl.BlockSpec((1, w, hd), lambda c: (sec, jnp.maximum(c * bpc - 1, 0), 0))
    out_spec = pl.BlockSpec((t, hd), lambda c: (c, 0))
    out, lse = pl.pallas_call(
        functools.partial(_attn_a_kernel, sub_len=sub_len),
        grid=(rows // t,),
        in_specs=[cur(0), cur(1), prev(1), cur(2), prev(2)],
        out_specs=[out_spec, out_spec],
        out_shape=[jax.ShapeDtypeStruct((rows, hd), F32)] * 2,
        scratch_shapes=[pltpu.VMEM((t + w, hd), BF16)] * 2,
        compiler_params=_params("parallel"),
        name=f"attn_a_r{dilation}",
    )(flat, flat, flat, flat, flat)
    shape = (heads, dilation, sub_len, hd)
    return out.reshape(shape), lse.reshape(shape)


def _mix_a_kernel(o0_ref, l0_ref, o1_ref, l1_ref, o2_ref, l2_ref, y_ref, nat_ref):
    tm = y_ref.shape[0]
    refs = ((o0_ref, l0_ref), (o1_ref, l1_ref), (o2_ref, l2_ref))
    r_max = DIL_PAIRS[-1][1]
    n = tm // r_max
    for hh in range(HEADS_PER_GROUP):
        for rho in range(r_max):
            outs, lses = [], []
            for (o_ref, l_ref), (_, r) in zip(refs, DIL_PAIRS):
                step = r_max // r
                rows = pl.ds(rho // r, n, stride=step) if step > 1 else pl.ds(0, n)
                outs.append(o_ref[hh, rho % r, rows, :])
                lses.append(l_ref[hh, rho % r, rows, :])
            m = jnp.maximum(jnp.maximum(lses[0], lses[1]), lses[2])
            ws = [jnp.exp(l - m) for l in lses]
            num = ws[0] * outs[0] + ws[1] * outs[1] + ws[2] * outs[2]
            den = ws[0] + ws[1] + ws[2]
            nat_ref[hh, pl.ds(rho, n, stride=r_max), :] = num / den
        y_ref[:, hh * HEAD_DIM:(hh + 1) * HEAD_DIM] = nat_ref[hh].astype(y_ref.dtype)


def _mix_a(group_outs, seq, tm=512):
    in_specs, args = [], []
    for (o, l), (_, r) in zip(group_outs, DIL_PAIRS):
        spec = pl.BlockSpec((HEADS_PER_GROUP, r, tm // r, HEAD_DIM), lambda i: (0, 0, i, 0))
        in_specs += [spec, spec]
        args += [o, l]
    return pl.pallas_call(
        _mix_a_kernel,
        grid=(seq // tm,),
        in_specs=in_specs,
        out_specs=pl.BlockSpec((tm, A_OUT), lambda i: (i, 0)),
        out_shape=jax.ShapeDtypeStruct((seq, A_OUT), BF16),
        scratch_shapes=[pltpu.VMEM((HEADS_PER_GROUP, tm, HEAD_DIM), F32)],
        compiler_params=_params("parallel"),
        name="mix_a",
    )(*args)


def _attn_b_kernel(lq1_ref, lk1_ref, lq2_ref, lk2_ref, g_ref, q_ref, qn_ref, k_ref, vt_ref,
                   *rest, lam_init, cast_cols):
    n_in = len(cast_cols)
    n_out = sum(len(cols) for cols in cast_cols)
    w_refs = rest[:n_in]
    o_ref = rest[n_in]
    wo_refs = rest[n_in + 1:n_in + 1 + n_out]
    m_ref, acc_ref, st_ref, mb_ref = rest[n_in + 1 + n_out:]
    tq = q_ref.shape[1]
    tk = vt_ref.shape[3]
    width = vt_ref.shape[2] - ONES_ROWS
    qi = pl.program_id(1)
    m_ref[...] = jnp.full(m_ref.shape, -jnp.inf, F32)

    @pl.when(jnp.logical_and(pl.program_id(0) == 0, qi == 0))
    def _():
        acc_ref[...] = jnp.zeros(acc_ref.shape, F32)

    def scores_to(slot, kv, q_src=q_ref):
        start = pl.multiple_of(kv * tk, tk)
        for i in range(2):
            st = lax.dot_general(
                k_ref[i, pl.ds(start, tk), :], q_src[i], (((1,), (1,)), ((), ())),
                preferred_element_type=F32)
            st_ref[slot, i] = st
            mb_ref[slot, i] = jnp.max(st, axis=0, keepdims=True)

    def softmax_pv(slot, kv, masked):
        vt = vt_ref[0, kv]
        for i in range(2):
            st = st_ref[slot, i]
            if masked:
                kpos = kv * tk + lax.broadcasted_iota(jnp.int32, st.shape, 0)
                qpos = qi * tq + lax.broadcasted_iota(jnp.int32, st.shape, 1)
                st = jnp.where(kpos <= qpos, st, -jnp.inf)
                m_blk = jnp.max(st, axis=0, keepdims=True)
            else:
                m_blk = mb_ref[slot, i]
            m_prev = m_ref[i]
            m_new = jnp.maximum(m_prev, m_blk)
            alpha = jnp.exp2(m_prev - m_new)
            p = jnp.exp2((st - m_new).astype(vt.dtype))
            acc_ref[i] = alpha * acc_ref[i] + jnp.dot(vt, p, preferred_element_type=F32)
            m_ref[i] = m_new

    @pl.when(qi == 0)
    def _():
        scores_to(0, 0)

    def pair(j):
        scores_to(1, 2 * j + 1)
        softmax_pv(0, 2 * j, masked=False)
        scores_to(0, 2 * j + 2)
        softmax_pv(1, 2 * j + 1, masked=False)

    def two_pairs(j, carry):
        pair(2 * j)
        pair(2 * j + 1)
        return carry

    n_pairs = qi // 2
    lax.fori_loop(0, n_pairs // 2, two_pairs, 0)

    @pl.when(n_pairs % 2 == 1)
    def _():
        pair(n_pairs - 1)

    @pl.when(qi % 2 == 0)
    def _():
        softmax_pv(0, qi, masked=True)
        scores_to(0, 0, qn_ref)

    @pl.when(qi % 2 == 1)
    def _():
        scores_to(1, qi)
        softmax_pv(0, qi - 1, masked=False)
        scores_to(0, 0, qn_ref)
        softmax_pv(1, qi, masked=True)

    lam = (jnp.exp(jnp.sum(lq1_ref[...] * lk1_ref[...], axis=-1, keepdims=True))
           - jnp.exp(jnp.sum(lq2_ref[...] * lk2_ref[...], axis=-1, keepdims=True))
           + lam_init)
    inv_l = [1.0 / acc_ref[i, width:width + 1, :] for i in range(2)]
    ot = acc_ref[0, :width, :] * inv_l[0] - lam * (acc_ref[1, :width, :] * inv_l[1])
    ms = jnp.mean(ot * ot, axis=0, keepdims=True)
    yt = (ot * lax.rsqrt(ms + EPS)) * g_ref[...]
    o_ref[...] = (yt * (1.0 - lam_init)).T.astype(o_ref.dtype)

    wo_iter = iter(wo_refs)
    for w_ref, cols in zip(w_refs, cast_cols):
        for start, size in cols:
            wo_ref = next(wo_iter)
            wo_ref[...] = w_ref[:, start:start + size].astype(wo_ref.dtype)


def _attn_b(qk, vt, lam_vecs, subln_g, lam_init, cast_weights):
    _, s, hd = qk.shape
    _, n_kv, rows, tk = vt.shape
    width = rows - ONES_ROWS
    tq = tk
    n_q = s // tq
    n_steps = B_HEADS * n_q
    vec = pl.BlockSpec((1, hd), lambda h, i: (0, 0))
    chunk = lambda h, i: (h * n_q + i, 0)
    w_specs, wo_specs, wo_shapes = [], [], []
    for w, cols in cast_weights:
        assert w.shape[0] % (n_steps * BF16_SUBLANES) == 0, w.shape
        rows_per_step = w.shape[0] // n_steps
        w_specs.append(pl.BlockSpec((rows_per_step, w.shape[1]), chunk))
        for _, size in cols:
            wo_specs.append(pl.BlockSpec((rows_per_step, size), chunk))
            wo_shapes.append(jax.ShapeDtypeStruct((w.shape[0], size), BF16))
    return pl.pallas_call(
        functools.partial(_attn_b_kernel, lam_init=lam_init,
                          cast_cols=tuple(tuple(cols) for _, cols in cast_weights)),
        grid=(B_HEADS, n_q),
        in_specs=[vec, vec, vec, vec,
                  pl.BlockSpec((width, 1), lambda h, i: (0, 0)),
                  pl.BlockSpec((2, tq, hd), lambda h, i: (h, i, 0)),
                  pl.BlockSpec((2, tq, hd), lambda h, i: (h, jnp.minimum(i + 1, n_q - 1), 0)),
                  pl.BlockSpec((2, s, hd), lambda h, i: (B_HEADS + h, 0, 0)),
                  pl.BlockSpec((1, n_kv, rows, tk), lambda h, i: (h, 0, 0, 0))] + w_specs,
        out_specs=[pl.BlockSpec((tq, width), lambda h, i: (i, h))] + wo_specs,
        out_shape=[jax.ShapeDtypeStruct((s, B_HEADS * width), BF16)] + wo_shapes,
        scratch_shapes=[pltpu.VMEM((2, 1, tq), F32),
                        pltpu.VMEM((2, rows, tq), F32),
                        pltpu.VMEM((2, 2, tk, tq), F32),
                        pltpu.VMEM((2, 2, 1, tq), F32)],
        compiler_params=_params("arbitrary", "arbitrary"),
        name="attn_b",
    )(*[x.reshape(1, hd) for x in lam_vecs], subln_g.reshape(width, 1), qk, qk, qk, vt,
      *[w for w, _ in cast_weights])


def _merge_kernel(ya_ref, yb_ref, wa_ref, wb_ref, ga_ref, gb_ref, o_ref):
    pa = jnp.dot(ya_ref[...], wa_ref[...].astype(BF16), preferred_element_type=F32)
    pb = jnp.dot(yb_ref[...], wb_ref[...], preferred_element_type=F32)
    o_ref[...] = (ga_ref[...].astype(F32) * pa + gb_ref[...].astype(F32) * pb).astype(o_ref.dtype)


def _merge(ya, yb, wa, wb, gates, tm=1024, tn=1024):
    s = ya.shape[0]
    n = wa.shape[1]
    return pl.pallas_call(
        _merge_kernel,
        grid=(s // tm, n // tn),
        in_specs=[pl.BlockSpec((tm, ya.shape[1]), lambda i, j: (i, 0)),
                  pl.BlockSpec((tm, yb.shape[1]), lambda i, j: (i, 0)),
                  pl.BlockSpec((wa.shape[0], tn), lambda i, j: (0, j)),
                  pl.BlockSpec((wb.shape[0], tn), lambda i, j: (0, j)),
                  pl.BlockSpec((tm, tn), lambda i, j: (i, j)),
                  pl.BlockSpec((tm, tn), lambda i, j: (i, n // tn + j))],
        out_specs=pl.BlockSpec((tm, tn), lambda i, j: (i, j)),
        out_shape=jax.ShapeDtypeStruct((s, n), BF16),
        compiler_params=_params("parallel", "arbitrary"),
        name="merge",
    )(ya, yb, wa, wb, gates, gates)


def _out_proj_kernel(a_ref, b_ref, x_ref, g_ref, x1_ref, xg_ref, r2_ref, *, n_cols):
    j = pl.program_id(1)
    x1 = x_ref[...] + jnp.dot(a_ref[...], b_ref[...], preferred_element_type=F32)
    x1_ref[...] = x1
    xg_ref[...] = (x1 * g_ref[...]).astype(xg_ref.dtype)
    part = jnp.broadcast_to(jnp.sum(x1 * x1, axis=-1, keepdims=True), r2_ref.shape)

    @pl.when(j == 0)
    def _():
        r2_ref[...] = part

    @pl.when(j > 0)
    def _():
        r2_ref[...] += part

    @pl.when(j == n_cols // x_ref.shape[1] - 1)
    def _():
        r2_ref[...] = 1.0 / (r2_ref[...] * (1.0 / n_cols) + EPS)


def _out_proj(a, b, x, g, tm=1024, tn=512):
    s, kdim = a.shape
    n = b.shape[1]
    lanes = HEAD_DIM
    return pl.pallas_call(
        functools.partial(_out_proj_kernel, n_cols=n),
        grid=(s // tm, n // tn),
        in_specs=[pl.BlockSpec((tm, kdim), lambda i, j: (i, 0)),
                  pl.BlockSpec((kdim, tn), lambda i, j: (0, j)),
                  pl.BlockSpec((tm, tn), lambda i, j: (i, j)),
                  pl.BlockSpec((1, tn), lambda i, j: (0, j))],
        out_specs=[pl.BlockSpec((tm, tn), lambda i, j: (i, j)),
                   pl.BlockSpec((tm, tn), lambda i, j: (i, j)),
                   pl.BlockSpec((tm, lanes), lambda i, j: (i, 0))],
        out_shape=[jax.ShapeDtypeStruct((s, n), F32),
                   jax.ShapeDtypeStruct((s, n), BF16),
                   jax.ShapeDtypeStruct((s, lanes), F32)],
        compiler_params=_params("parallel", "arbitrary"),
        name="out_proj",
    )(a, b, x, g.reshape(1, n))


def _ff1_kernel(xg_ref, r2_ref, b_ref, o_ref):
    lanes = r2_ref.shape[1]
    u = jnp.maximum(jnp.dot(xg_ref[...], b_ref[...], preferred_element_type=F32), 0.0)
    for c in range(o_ref.shape[1] // lanes):
        uc = u[:, c * lanes:(c + 1) * lanes]
        o_ref[:, c * lanes:(c + 1) * lanes] = (uc * uc * r2_ref[...]).astype(o_ref.dtype)


def _ff1(xg, r2, b, tm=1024, tn=1024):
    s, kdim = xg.shape
    n = b.shape[1]
    return pl.pallas_call(
        _ff1_kernel,
        grid=(s // tm, n // tn),
        in_specs=[pl.BlockSpec((tm, kdim), lambda i, j: (i, 0)),
                  pl.BlockSpec((tm, r2.shape[1]), lambda i, j: (i, 0)),
                  pl.BlockSpec((kdim, tn), lambda i, j: (0, j))],
        out_specs=pl.BlockSpec((tm, tn), lambda i, j: (i, j)),
        out_shape=jax.ShapeDtypeStruct((s, n), BF16),
        compiler_params=_params("parallel", "arbitrary"),
        name="ff1",
    )(xg, r2, b)


def _ff2_kernel(a_ref, b_ref, x_ref, o_ref):
    @pl.when(pl.program_id(2) == 0)
    def _():
        o_ref[...] = x_ref[...]

    o_ref[...] += jnp.dot(a_ref[...], b_ref[...], preferred_element_type=F32)


def _ff2(a, b, x, tm=1024, tn=1024, tk=4096):
    s, kdim = a.shape
    n = b.shape[1]
    return pl.pallas_call(
        _ff2_kernel,
        grid=(s // tm, n // tn, kdim // tk),
        in_specs=[pl.BlockSpec((tm, tk), lambda i, j, k: (i, k)),
                  pl.BlockSpec((tk, tn), lambda i, j, k: (k, j)),
                  pl.BlockSpec((tm, tn), lambda i, j, k: (i, j))],
        out_specs=pl.BlockSpec((tm, tn), lambda i, j, k: (i, j)),
        out_shape=jax.ShapeDtypeStruct((s, n), F32),
        compiler_params=_params("parallel", "parallel", "arbitrary"),
        name="ff2",
    )(a, b, x)


def _lambda_init(layer_idx):
    return 0.8 - 0.6 * math.exp(-0.3 * layer_idx)


def kernel(x, norm_mix, w_in, b_gate, w_proj_a, w_proj_b, w_out, lambda_q1, lambda_k1,
           lambda_q2, lambda_k2, subln_g, norm_mlp, w_ff1, w_ff2, norm_final):
    batch, seq, d = x.shape
    depth = w_in.shape[0]
    tables = _rope_tables(seq)
    outs = []
    for b in range(batch):
        xb = x.reshape(seq, d) if batch == 1 else x[b]
        for l in range(depth):
            w_in_l = w_in[l]
            h = _rmsnorm(xb, norm_mix[l], BF16)
            qkb = _proj_bqk(h, w_in_l, tables)
            vb = _proj_bv(h, w_in_l)
            lam_init = _lambda_init(l)
            whole = lambda w: (w, [(0, w.shape[1])])
            yb, wa_in_bf, wg_bf, wb_bf, wo_bf, w1_bf, w2_bf = _attn_b(
                qkb, vb, (lambda_q1[l], lambda_k1[l], lambda_q2[l], lambda_k2[l]),
                subln_g[l], lam_init,
                [(w_in_l, [(COL_QA, 3 * A_WIDTH), (COL_GATE, 2 * D_MODEL)]),
                 whole(w_proj_b[l]), whole(w_out[l]), whole(w_ff1[l]), whole(w_ff2[l])])
            groups = []
            for g in range(N_GROUPS):
                groups.append(_attn_a(_proj_a(h, wa_in_bf, tables, g)))
            ya = _mix_a(groups, seq)
            gates = _proj_gate(h, wg_bf, b_gate[l])
            merged = _merge(ya, yb, w_proj_a[l], wb_bf, gates)
            xb, xg, r2 = _out_proj(merged, wo_bf, xb, norm_mlp[l])
            u = _ff1(xg, r2, w1_bf)
            xb = _ff2(u, w2_bf, xb)
        outs.append(_rmsnorm(xb, norm_final, x.dtype))
    if batch == 1:
        return outs[0].reshape(1, seq, d)
    return jnp.stack(outs, axis=0)
```

```python
import functools
import math

import jax
import jax.numpy as jnp
from jax import lax
from jax.experimental import pallas as pl
from jax.experimental.pallas import tpu as pltpu

D_MODEL = 4096
HEAD_DIM = 128
DIL_PAIRS = ((128, 1), (512, 4), (2048, 16))
N_GROUPS = len(DIL_PAIRS)
HEADS_PER_GROUP = 4
GROUP_COLS = HEADS_PER_GROUP * HEAD_DIM
A_WIDTH = N_GROUPS * GROUP_COLS
A_OUT = GROUP_COLS
B_HEADS = 8
B_QK = B_HEADS * 2 * HEAD_DIM
B_V = B_HEADS * 2 * HEAD_DIM
ROPE_THETA = 500000.0
ROT_DIM = HEAD_DIM // 4
ROT_HALF = ROT_DIM // 2
EPS = 1e-6
QK_SCALE = 1.0 / math.sqrt(HEAD_DIM)
LOG2_E = math.log2(math.e)
WINDOW_STEPS = 128
BF16_SUBLANES = 16
ONES_ROWS = BF16_SUBLANES

COL_QA = 0
COL_QB = 3 * A_WIDTH
COL_VB = COL_QB + 2 * B_QK
COL_GATE = COL_VB + B_V

VMEM_LIMIT_BYTES = 56 * 1024 * 1024

F32 = jnp.float32
BF16 = jnp.bfloat16


def _params(*semantics):
    return pltpu.CompilerParams(dimension_semantics=semantics,
                                vmem_limit_bytes=VMEM_LIMIT_BYTES)


def _rmsnorm_kernel(x_ref, g_ref, o_ref):
    x = x_ref[...]
    ms = jnp.mean(x * x, axis=-1, keepdims=True)
    o_ref[...] = ((x * lax.rsqrt(ms + EPS)) * g_ref[...]).astype(o_ref.dtype)


def _rmsnorm(x, g, out_dtype, rows=512):
    s, d = x.shape
    return pl.pallas_call(
        _rmsnorm_kernel,
        grid=(s // rows,),
        in_specs=[pl.BlockSpec((rows, d), lambda i: (i, 0)),
                  pl.BlockSpec((1, d), lambda i: (0, 0))],
        out_specs=pl.BlockSpec((rows, d), lambda i: (i, 0)),
        out_shape=jax.ShapeDtypeStruct((s, d), out_dtype),
        compiler_params=_params("parallel"),
        name="rmsnorm",
    )(x, g.reshape(1, d))


def _rope_tables(seq):
    pos = jnp.arange(seq, dtype=F32)
    inv = ROPE_THETA ** (-jnp.arange(0, ROT_DIM, 2, dtype=F32) / ROT_DIM)
    ang = pos[:, None] * inv[None, :]
    cos, sin = jnp.cos(ang), jnp.sin(ang)
    ones = jnp.ones((seq, HEAD_DIM - ROT_DIM), F32)
    zeros = jnp.zeros((seq, HEAD_DIM - ROT_DIM), F32)
    zh = jnp.zeros((seq, ROT_HALF), F32)
    cos_t = jnp.concatenate([cos, cos, ones], axis=1)
    sin_lo = jnp.concatenate([-sin, zh, zeros], axis=1)
    sin_hi = jnp.concatenate([zh, sin, zeros], axis=1)
    rot = jnp.stack([cos_t, sin_lo, sin_hi])
    ident = jnp.stack([jnp.ones_like(cos_t), jnp.zeros_like(cos_t), jnp.zeros_like(cos_t)])
    return jnp.stack([rot, ident])


def _rope_head(y, tab_ref, scale):
    x2 = pltpu.roll(y, HEAD_DIM - ROT_HALF, 1)
    x1 = pltpu.roll(y, ROT_HALF, 1)
    return (y * tab_ref[0, 0] + x2 * tab_ref[0, 1] + x1 * tab_ref[0, 2]) * scale


def _store_product(a_ref, w_ref, acc_ref, *_):
    acc_ref[...] = jnp.dot(a_ref[...], w_ref[...].astype(BF16), preferred_element_type=F32)


def _lagged_kernel(a_ref, w_ref, *refs, n_ep, n_out, n_tiles, n_j, epilogue):
    ep_refs = refs[:n_ep]
    out_refs = refs[n_ep:n_ep + n_out]
    scratch = refs[n_ep + n_out:]
    t = pl.program_id(0)
    jp = jnp.maximum(t - 1, 0) % n_j

    @pl.when(t == 0)
    def _():
        _store_product(a_ref, w_ref, *scratch)

    @pl.when(jnp.logical_and(t > 0, t < n_tiles))
    def _():
        epilogue(jp, *ep_refs, *out_refs, *scratch)
        _store_product(a_ref, w_ref, *scratch)

    @pl.when(t == n_tiles)
    def _():
        epilogue(jp, *ep_refs, *out_refs, *scratch)


def _lagged_matmul(name, epilogue, *, n_i, n_j, mm_args, mm_specs, ep_args, ep_specs,
                   out_shapes, out_specs, scratch_shapes):
    n_tiles = n_i * n_j

    def cur(f):
        return lambda t: f(*divmod(jnp.minimum(t, n_tiles - 1), n_j))

    def prev(f):
        return lambda t: f(*divmod(jnp.maximum(t - 1, 0), n_j))

    in_specs = ([pl.BlockSpec(shape, cur(f)) for shape, f in mm_specs]
                + [pl.BlockSpec(shape, prev(f)) for shape, f in ep_specs])
    return pl.pallas_call(
        functools.partial(_lagged_kernel, n_ep=len(ep_args), n_out=len(out_shapes),
                          n_tiles=n_tiles, n_j=n_j, epilogue=epilogue),
        grid=(n_tiles + 1,),
        in_specs=in_specs,
        out_specs=[pl.BlockSpec(shape, prev(f)) for shape, f in out_specs],
        out_shape=out_shapes,
        scratch_shapes=scratch_shapes,
        compiler_params=_params("arbitrary"),
        name=name,
    )(*mm_args, *ep_args)


def _proj_a_epilogue(jp, tab_ref, o_ref, acc_ref, y_ref, *, dilation):
    tm = acc_ref.shape[0]
    n = tm // dilation
    scale = jnp.where(jp == 0, QK_SCALE, 1.0).astype(F32)
    for hh in range(HEADS_PER_GROUP):
        y = _rope_head(acc_ref[:, hh * HEAD_DIM:(hh + 1) * HEAD_DIM], tab_ref, scale)
        if dilation == 1:
            o_ref[0, hh, 0] = y.astype(o_ref.dtype)
            continue
        y_ref[hh] = y
        for rho in range(dilation):
            o_ref[0, hh, rho] = y_ref[hh, pl.ds(rho, n, stride=dilation), :].astype(o_ref.dtype)


def _proj_a(h, w_in, tables, group, tm=1024):
    s, d = h.shape
    dilation = DIL_PAIRS[group][1]
    sub_len = s // dilation
    blocks_per_section = A_WIDTH // GROUP_COLS
    return _lagged_matmul(
        f"proj_a_g{group}", functools.partial(_proj_a_epilogue, dilation=dilation),
        n_i=s // tm, n_j=3,
        mm_args=[h, w_in],
        mm_specs=[((tm, d), lambda i, j: (i, 0)),
                  ((d, GROUP_COLS), lambda i, j: (0, j * blocks_per_section + group))],
        ep_args=[tables],
        ep_specs=[((1, 3, tm, HEAD_DIM), lambda i, j: (j // 2, 0, i, 0))],
        out_shapes=[jax.ShapeDtypeStruct((3, HEADS_PER_GROUP, dilation, sub_len, HEAD_DIM), BF16)],
        out_specs=[((1, HEADS_PER_GROUP, dilation, tm // dilation, HEAD_DIM),
                    lambda i, j: (j, 0, 0, i, 0))],
        scratch_shapes=[pltpu.VMEM((tm, GROUP_COLS), F32),
                        pltpu.VMEM((HEADS_PER_GROUP, tm, HEAD_DIM), F32)])[0]


def _proj_bqk_epilogue(jp, tab_ref, o_ref, acc_ref, *, q_tiles):
    scale = jnp.where(jp < q_tiles, QK_SCALE * LOG2_E, 1.0).astype(F32)
    for hh in range(o_ref.shape[0]):
        y = _rope_head(acc_ref[:, hh * HEAD_DIM:(hh + 1) * HEAD_DIM], tab_ref, scale)
        o_ref[hh] = y.astype(o_ref.dtype)


def _proj_bqk(h, w_in, tables, tm=1024, tn=512):
    s, d = h.shape
    heads_per_tile = tn // HEAD_DIM
    n = B_QK + B_QK
    return _lagged_matmul(
        "proj_b_qk", functools.partial(_proj_bqk_epilogue, q_tiles=B_QK // tn),
        n_i=s // tm, n_j=n // tn,
        mm_args=[h, w_in],
        mm_specs=[((tm, d), lambda i, j: (i, 0)),
                  ((d, tn), lambda i, j: (0, COL_QB // tn + j))],
        ep_args=[tables],
        ep_specs=[((1, 3, tm, HEAD_DIM), lambda i, j: (0, 0, i, 0))],
        out_shapes=[jax.ShapeDtypeStruct((n // HEAD_DIM, s, HEAD_DIM), BF16)],
        out_specs=[((heads_per_tile, tm, HEAD_DIM), lambda i, j: (j, i, 0))],
        scratch_shapes=[pltpu.VMEM((tm, tn), F32)])[0]


def _proj_bv_kernel(h_ref, w_ref, o_ref):
    acc = jnp.dot(h_ref[...], w_ref[...].astype(BF16), preferred_element_type=F32)
    heads, blocks, rows, tk = o_ref.shape
    width = rows - ONES_ROWS
    for hv in range(heads):
        vt = acc[:, hv * width:(hv + 1) * width].T
        for c in range(blocks):
            o_ref[hv, c, :width, :] = vt[:, c * tk:(c + 1) * tk].astype(o_ref.dtype)
            o_ref[hv, c, width:, :] = jnp.ones((ONES_ROWS, tk), o_ref.dtype)


def _proj_bv(h, w_in, tm=1024, tn=512, tk=512):
    s, d = h.shape
    width = 2 * HEAD_DIM
    rows = width + ONES_ROWS
    return pl.pallas_call(
        _proj_bv_kernel,
        grid=(s // tm, B_V // tn),
        in_specs=[pl.BlockSpec((tm, d), lambda i, j: (i, 0)),
                  pl.BlockSpec((d, tn), lambda i, j: (0, COL_VB // tn + j))],
        out_specs=pl.BlockSpec((tn // width, tm // tk, rows, tk), lambda i, j: (j, i, 0, 0)),
        out_shape=jax.ShapeDtypeStruct((B_HEADS, s // tk, rows, tk), BF16),
        compiler_params=_params("parallel", "arbitrary"),
        name="proj_b_v",
    )(h, w_in)


def _proj_gate_kernel(h_ref, w_ref, b_ref, o_ref):
    z = jnp.dot(h_ref[...], w_ref[...], preferred_element_type=F32) + b_ref[...]
    o_ref[...] = (0.5 * jnp.tanh(0.5 * z) + 0.5).astype(o_ref.dtype)


def _proj_gate(h, w_gate, b_gate, tm=1024, tn=1024):
    s, d = h.shape
    n = w_gate.shape[1]
    return pl.pallas_call(
        _proj_gate_kernel,
        grid=(s // tm, n // tn),
        in_specs=[pl.BlockSpec((tm, d), lambda i, j: (i, 0)),
                  pl.BlockSpec((d, tn), lambda i, j: (0, j)),
                  pl.BlockSpec((1, tn), lambda i, j: (0, j))],
        out_specs=pl.BlockSpec((tm, tn), lambda i, j: (i, j)),
        out_shape=jax.ShapeDtypeStruct((s, n), BF16),
        compiler_params=_params("parallel", "arbitrary"),
        name="proj_gate",
    )(h, w_gate, b_gate.reshape(1, n))


def _attn_a_kernel(q_ref, kc_ref, kp_ref, vc_ref, vp_ref, o_ref, lse_ref, kk_ref, vv_ref,
                   *, sub_len):
    w = WINDOW_STEPS
    t = q_ref.shape[1]
    c = pl.program_id(0)
    kk_ref[0:w] = kp_ref[0]
    kk_ref[w:] = kc_ref[0]
    vv_ref[0:w] = vp_ref[0]
    vv_ref[w:] = vc_ref[0]
    qp = lax.broadcasted_iota(jnp.int32, (w, 2 * w), 0)
    kp = lax.broadcasted_iota(jnp.int32, (w, 2 * w), 1)
    band = (kp >= qp) & (kp <= qp + w)
    band_start = band & (kp >= w)
    if t >= sub_len:
        masks = [band_start if (b * w) % sub_len == 0 else band for b in range(t // w)]
    else:
        is_start = (c % (sub_len // t)) == 0
        masks = [band & (kp >= jnp.where(is_start, w, 0))] + [band] * (t // w - 1)
    for b in range(t // w):
        q = q_ref[0, b * w:(b + 1) * w]
        k = kk_ref[b * w:(b + 2) * w]
        v = vv_ref[b * w:(b + 2) * w]
        s = lax.dot_general(q, k, (((1,), (1,)), ((), ())), preferred_element_type=F32)
        s = jnp.where(masks[b], s, -jnp.inf)
        m = jnp.max(s, axis=-1, keepdims=True)
        p = jnp.exp(s - m)
        l = jnp.sum(p, axis=-1, keepdims=True)
        o = jnp.dot(p.astype(v.dtype), v, preferred_element_type=F32)
        o_ref[b * w:(b + 1) * w] = o / l
        lse_ref[b * w:(b + 1) * w] = jnp.broadcast_to(m + jnp.log(l), (w, HEAD_DIM))


def _attn_a(qkv, t=4096):
    _, heads, dilation, sub_len, hd = qkv.shape
    rows = heads * dilation * sub_len
    flat = qkv.reshape(3, rows, hd)
    w = WINDOW_STEPS
    assert t % sub_len == 0 or sub_len % t == 0
    bpc = t // w
    cur = lambda sec: pl.BlockSpec((1, t, hd), lambda c: (sec, c, 0))
    prev = lambda sec: pl.BlockSpec((1, w, hd), lambda c: (sec, jnp.maximum(c * bpc - 1, 0), 0))
    out_spec = pl.BlockSpec((t, hd), lambda c: (c, 0))
    out, lse = pl.pallas_call(
        functools.partial(_attn_a_kernel, sub_len=sub_len),
        grid=(rows // t,),
        in_specs=[cur(0), cur(1), prev(1), cur(2), prev(2)],
        out_specs=[out_spec, out_spec],
        out_shape=[jax.ShapeDtypeStruct((rows, hd), F32)] * 2,
        scratch_shapes=[pltpu.VMEM((t + w, hd), BF16)] * 2,
        compiler_params=_params("parallel"),
        name=f"attn_a_r{dilation}",
    )(flat, flat, flat, flat, flat)
    shape = (heads, dilation, sub_len, hd)
    return out.reshape(shape), lse.reshape(shape)


def _mix_a_kernel(o0_ref, l0_ref, o1_ref, l1_ref, o2_ref, l2_ref, y_ref, nat_ref):
    tm = y_ref.shape[0]
    refs = ((o0_ref, l0_ref), (o1_ref, l1_ref), (o2_ref, l2_ref))
    r_max = DIL_PAIRS[-1][1]
    n = tm // r_max
    for hh in range(HEADS_PER_GROUP):
        for rho in range(r_max):
            outs, lses = [], []
            for (o_ref, l_ref), (_, r) in zip(refs, DIL_PAIRS):
                step = r_max // r
                rows = pl.ds(rho // r, n, stride=step) if step > 1 else pl.ds(0, n)
                outs.append(o_ref[hh, rho % r, rows, :])
                lses.append(l_ref[hh, rho % r, rows, :])
            m = jnp.maximum(jnp.maximum(lses[0], lses[1]), lses[2])
            ws = [jnp.exp(l - m) for l in lses]
            num = ws[0] * outs[0] + ws[1] * outs[1] + ws[2] * outs[2]
            den = ws[0] + ws[1] + ws[2]
            nat_ref[hh, pl.ds(rho, n, stride=r_max), :] = num / den
        y_ref[:, hh * HEAD_DIM:(hh + 1) * HEAD_DIM] = nat_ref[hh].astype(y_ref.dtype)


def _mix_a(group_outs, seq, tm=512):
    in_specs, args = [], []
    for (o, l), (_, r) in zip(group_outs, DIL_PAIRS):
        spec = pl.BlockSpec((HEADS_PER_GROUP, r, tm // r, HEAD_DIM), lambda i: (0, 0, i, 0))
        in_specs += [spec, spec]
        args += [o, l]
    return pl.pallas_call(
        _mix_a_kernel,
        grid=(seq // tm,),
        in_specs=in_specs,
        out_specs=pl.BlockSpec((tm, A_OUT), lambda i: (i, 0)),
        out_shape=jax.ShapeDtypeStruct((seq, A_OUT), BF16),
        scratch_shapes=[pltpu.VMEM((HEADS_PER_GROUP, tm, HEAD_DIM), F32)],
        compiler_params=_params("parallel"),
        name="mix_a",
    )(*args)


def _attn_b_kernel(lq1_ref, lk1_ref, lq2_ref, lk2_ref, g_ref, q_ref, qn_ref, k_ref, vt_ref,
                   *rest, lam_init, cast_cols):
    n_in = len(cast_cols)
    n_out = sum(len(cols) for cols in cast_cols)
    w_refs = rest[:n_in]
    o_ref = rest[n_in]
    wo_refs = rest[n_in + 1:n_in + 1 + n_out]
    m_ref, acc_ref, st_ref, mb_ref = rest[n_in + 1 + n_out:]
    tq = q_ref.shape[1]
    tk = vt_ref.shape[3]
    width = vt_ref.shape[2] - ONES_ROWS
    qi = pl.program_id(1)
    m_ref[...] = jnp.full(m_ref.shape, -jnp.inf, F32)
    acc_ref[...] = jnp.zeros(acc_ref.shape, F32)

    def scores_to(slot, kv, q_src=q_ref):
        start = pl.multiple_of(kv * tk, tk)
        for i in range(2):
            st = lax.dot_general(
                k_ref[i, pl.ds(start, tk), :], q_src[i], (((1,), (1,)), ((), ())),
                preferred_element_type=F32)
            st_ref[slot, i] = st
            mb_ref[slot, i] = jnp.max(st, axis=0, keepdims=True)

    def softmax_pv(slot, kv, masked):
        vt = vt_ref[0, kv]
        for i in range(2):
            st = st_ref[slot, i]
            if masked:
                kpos = kv * tk + lax.broadcasted_iota(jnp.int32, st.shape, 0)
                qpos = qi * tq + lax.broadcasted_iota(jnp.int32, st.shape, 1)
                st = jnp.where(kpos <= qpos, st, -jnp.inf)
                m_blk = jnp.max(st, axis=0, keepdims=True)
            else:
                m_blk = mb_ref[slot, i]
            m_prev = m_ref[i]
            m_new = jnp.maximum(m_prev, m_blk)
            alpha = jnp.exp2(m_prev - m_new)
            p = jnp.exp2((st - m_new).astype(vt.dtype))
            acc_ref[i] = alpha * acc_ref[i] + jnp.dot(vt, p, preferred_element_type=F32)
            m_ref[i] = m_new

    @pl.when(qi == 0)
    def _():
        scores_to(0, 0)

    def pair(j):
        scores_to(1, 2 * j + 1)
        softmax_pv(0, 2 * j, masked=False)
        scores_to(0, 2 * j + 2)
        softmax_pv(1, 2 * j + 1, masked=False)

    def two_pairs(j, carry):
        pair(2 * j)
        pair(2 * j + 1)
        return carry

    n_pairs = qi // 2
    lax.fori_loop(0, n_pairs // 2, two_pairs, 0)

    @pl.when(n_pairs % 2 == 1)
    def _():
        pair(n_pairs - 1)

    @pl.when(qi % 2 == 0)
    def _():
        softmax_pv(0, qi, masked=True)
        scores_to(0, 0, qn_ref)

    @pl.when(qi % 2 == 1)
    def _():
        scores_to(1, qi)
        softmax_pv(0, qi - 1, masked=False)
        scores_to(0, 0, qn_ref)
        softmax_pv(1, qi, masked=True)

    lam = (jnp.exp(jnp.sum(lq1_ref[...] * lk1_ref[...], axis=-1, keepdims=True))
           - jnp.exp(jnp.sum(lq2_ref[...] * lk2_ref[...], axis=-1, keepdims=True))
           + lam_init)
    inv_l = [1.0 / acc_ref[i, width:width + 1, :] for i in range(2)]
    ot = acc_ref[0, :width, :] * inv_l[0] - lam * (acc_ref[1, :width, :] * inv_l[1])
    ms = jnp.mean(ot * ot, axis=0, keepdims=True)
    yt = (ot * lax.rsqrt(ms + EPS)) * g_ref[...]
    o_ref[...] = (yt * (1.0 - lam_init)).T.astype(o_ref.dtype)

    wo_iter = iter(wo_refs)
    for w_ref, cols in zip(w_refs, cast_cols):
        for start, size in cols:
            wo_ref = next(wo_iter)
            wo_ref[...] = w_ref[:, start:start + size].astype(wo_ref.dtype)


def _attn_b(qk, vt, lam_vecs, subln_g, lam_init, cast_weights):
    _, s, hd = qk.shape
    _, n_kv, rows, tk = vt.shape
    width = rows - ONES_ROWS
    tq = tk
    n_q = s // tq
    n_steps = B_HEADS * n_q
    vec = pl.BlockSpec((1, hd), lambda h, i: (0, 0))
    chunk = lambda h, i: (h * n_q + i, 0)
    w_specs, wo_specs, wo_shapes = [], [], []
    for w, cols in cast_weights:
        assert w.shape[0] % (n_steps * BF16_SUBLANES) == 0, w.shape
        rows_per_step = w.shape[0] // n_steps
        w_specs.append(pl.BlockSpec((rows_per_step, w.shape[1]), chunk))
        for _, size in cols:
            wo_specs.append(pl.BlockSpec((rows_per_step, size), chunk))
            wo_shapes.append(jax.ShapeDtypeStruct((w.shape[0], size), BF16))
    return pl.pallas_call(
        functools.partial(_attn_b_kernel, lam_init=lam_init,
                          cast_cols=tuple(tuple(cols) for _, cols in cast_weights)),
        grid=(B_HEADS, n_q),
        in_specs=[vec, vec, vec, vec,
                  pl.BlockSpec((width, 1), lambda h, i: (0, 0)),
                  pl.BlockSpec((2, tq, hd), lambda h, i: (h, i, 0)),
                  pl.BlockSpec((2, tq, hd), lambda h, i: (h, jnp.minimum(i + 1, n_q - 1), 0)),
                  pl.BlockSpec((2, s, hd), lambda h, i: (B_HEADS + h, 0, 0)),
                  pl.BlockSpec((1, n_kv, rows, tk), lambda h, i: (h, 0, 0, 0))] + w_specs,
        out_specs=[pl.BlockSpec((tq, width), lambda h, i: (i, h))] + wo_specs,
        out_shape=[jax.ShapeDtypeStruct((s, B_HEADS * width), BF16)] + wo_shapes,
        scratch_shapes=[pltpu.VMEM((2, 1, tq), F32),
                        pltpu.VMEM((2, rows, tq), F32),
                        pltpu.VMEM((2, 2, tk, tq), F32),
                        pltpu.VMEM((2, 2, 1, tq), F32)],
        compiler_params=_params("arbitrary", "arbitrary"),
        name="attn_b",
    )(*[x.reshape(1, hd) for x in lam_vecs], subln_g.reshape(width, 1), qk, qk, qk, vt,
      *[w for w, _ in cast_weights])


def _merge_kernel(ya_ref, yb_ref, wa_ref, wb_ref, ga_ref, gb_ref, o_ref):
    pa = jnp.dot(ya_ref[...], wa_ref[...].astype(BF16), preferred_element_type=F32)
    pb = jnp.dot(yb_ref[...], wb_ref[...], preferred_element_type=F32)
    o_ref[...] = (ga_ref[...].astype(F32) * pa + gb_ref[...].astype(F32) * pb).astype(o_ref.dtype)


def _merge(ya, yb, wa, wb, gates, tm=1024, tn=1024):
    s = ya.shape[0]
    n = wa.shape[1]
    return pl.pallas_call(
        _merge_kernel,
        grid=(s // tm, n // tn),
        in_specs=[pl.BlockSpec((tm, ya.shape[1]), lambda i, j: (i, 0)),
                  pl.BlockSpec((tm, yb.shape[1]), lambda i, j: (i, 0)),
                  pl.BlockSpec((wa.shape[0], tn), lambda i, j: (0, j)),
                  pl.BlockSpec((wb.shape[0], tn), lambda i, j: (0, j)),
                  pl.BlockSpec((tm, tn), lambda i, j: (i, j)),
                  pl.BlockSpec((tm, tn), lambda i, j: (i, n // tn + j))],
        out_specs=pl.BlockSpec((tm, tn), lambda i, j: (i, j)),
        out_shape=jax.ShapeDtypeStruct((s, n), BF16),
        compiler_params=_params("parallel", "arbitrary"),
        name="merge",
    )(ya, yb, wa, wb, gates, gates)


def _out_proj_kernel(a_ref, b_ref, x_ref, g_ref, x1_ref, xg_ref, r2_ref, *, n_cols):
    j = pl.program_id(1)
    x1 = x_ref[...] + jnp.dot(a_ref[...], b_ref[...], preferred_element_type=F32)
    x1_ref[...] = x1
    xg_ref[...] = (x1 * g_ref[...]).astype(xg_ref.dtype)
    part = jnp.broadcast_to(jnp.sum(x1 * x1, axis=-1, keepdims=True), r2_ref.shape)

    @pl.when(j == 0)
    def _():
        r2_ref[...] = part

    @pl.when(j > 0)
    def _():
        r2_ref[...] += part

    @pl.when(j == n_cols // x_ref.shape[1] - 1)
    def _():
        r2_ref[...] = 1.0 / (r2_ref[...] * (1.0 / n_cols) + EPS)


def _out_proj(a, b, x, g, tm=1024, tn=512):
    s, kdim = a.shape
    n = b.shape[1]
    lanes = HEAD_DIM
    return pl.pallas_call(
        functools.partial(_out_proj_kernel, n_cols=n),
        grid=(s // tm, n // tn),
        in_specs=[pl.BlockSpec((tm, kdim), lambda i, j: (i, 0)),
                  pl.BlockSpec((kdim, tn), lambda i, j: (0, j)),
                  pl.BlockSpec((tm, tn), lambda i, j: (i, j)),
                  pl.BlockSpec((1, tn), lambda i, j: (0, j))],
        out_specs=[pl.BlockSpec((tm, tn), lambda i, j: (i, j)),
                   pl.BlockSpec((tm, tn), lambda i, j: (i, j)),
                   pl.BlockSpec((tm, lanes), lambda i, j: (i, 0))],
        out_shape=[jax.ShapeDtypeStruct((s, n), F32),
                   jax.ShapeDtypeStruct((s, n), BF16),
                   jax.ShapeDtypeStruct((s, lanes), F32)],
        compiler_params=_params("parallel", "arbitrary"),
        name="out_proj",
    )(a, b, x, g.reshape(1, n))


def _ff1_kernel(xg_ref, r2_ref, b_ref, w2_ref, o_ref, w2o_ref):
    lanes = r2_ref.shape[1]
    u = jnp.maximum(jnp.dot(xg_ref[...], b_ref[...], preferred_element_type=F32), 0.0)
    for c in range(o_ref.shape[1] // lanes):
        uc = u[:, c * lanes:(c + 1) * lanes]
        o_ref[:, c * lanes:(c + 1) * lanes] = (uc * uc * r2_ref[...]).astype(o_ref.dtype)
    w2o_ref[...] = w2_ref[...].astype(w2o_ref.dtype)


def _ff1(xg, r2, b, w_next, tm=1024, tn=1024):
    s, kdim = xg.shape
    n = b.shape[1]
    n_i, n_j = s // tm, n // tn
    assert w_next.shape[0] % (n_i * n_j * BF16_SUBLANES) == 0, w_next.shape
    chunk = (w_next.shape[0] // (n_i * n_j), w_next.shape[1])
    chunk_map = lambda i, j: (i * n_j + j, 0)
    return pl.pallas_call(
        _ff1_kernel,
        grid=(n_i, n_j),
        in_specs=[pl.BlockSpec((tm, kdim), lambda i, j: (i, 0)),
                  pl.BlockSpec((tm, r2.shape[1]), lambda i, j: (i, 0)),
                  pl.BlockSpec((kdim, tn), lambda i, j: (0, j)),
                  pl.BlockSpec(chunk, chunk_map)],
        out_specs=[pl.BlockSpec((tm, tn), lambda i, j: (i, j)),
                   pl.BlockSpec(chunk, chunk_map)],
        out_shape=[jax.ShapeDtypeStruct((s, n), BF16),
                   jax.ShapeDtypeStruct(w_next.shape, BF16)],
        compiler_params=_params("arbitrary", "arbitrary"),
        name="ff1",
    )(xg, r2, b, w_next)


def _ff2_kernel(a_ref, b_ref, x_ref, o_ref):
    @pl.when(pl.program_id(2) == 0)
    def _():
        o_ref[...] = x_ref[...]

    o_ref[...] += jnp.dot(a_ref[...], b_ref[...], preferred_element_type=F32)


def _ff2(a, b, x, tm=1024, tn=1024, tk=4096):
    s, kdim = a.shape
    n = b.shape[1]
    return pl.pallas_call(
        _ff2_kernel,
        grid=(s // tm, n // tn, kdim // tk),
        in_specs=[pl.BlockSpec((tm, tk), lambda i, j, k: (i, k)),
                  pl.BlockSpec((tk, tn), lambda i, j, k: (k, j)),
                  pl.BlockSpec((tm, tn), lambda i, j, k: (i, j))],
        out_specs=pl.BlockSpec((tm, tn), lambda i, j, k: (i, j)),
        out_shape=jax.ShapeDtypeStruct((s, n), F32),
        compiler_params=_params("parallel", "parallel", "arbitrary"),
        name="ff2",
    )(a, b, x)


def _lambda_init(layer_idx):
    return 0.8 - 0.6 * math.exp(-0.3 * layer_idx)


def kernel(x, norm_mix, w_in, b_gate, w_proj_a, w_proj_b, w_out, lambda_q1, lambda_k1,
           lambda_q2, lambda_k2, subln_g, norm_mlp, w_ff1, w_ff2, norm_final):
    batch, seq, d = x.shape
    depth = w_in.shape[0]
    tables = _rope_tables(seq)
    outs = []
    for b in range(batch):
        xb = x.reshape(seq, d) if batch == 1 else x[b]
        for l in range(depth):
            w_in_l = w_in[l]
            h = _rmsnorm(xb, norm_mix[l], BF16)
            qkb = _proj_bqk(h, w_in_l, tables)
            vb = _proj_bv(h, w_in_l)
            lam_init = _lambda_init(l)
            whole = lambda w: (w, [(0, w.shape[1])])
            yb, wa_in_bf, wg_bf, wb_bf, wo_bf, w1_bf = _attn_b(
                qkb, vb, (lambda_q1[l], lambda_k1[l], lambda_q2[l], lambda_k2[l]),
                subln_g[l], lam_init,
                [(w_in_l, [(COL_QA, 3 * A_WIDTH), (COL_GATE, 2 * D_MODEL)]),
                 whole(w_proj_b[l]), whole(w_out[l]), whole(w_ff1[l])])
            groups = []
            for g in range(N_GROUPS):
                groups.append(_attn_a(_proj_a(h, wa_in_bf, tables, g)))
            ya = _mix_a(groups, seq)
            gates = _proj_gate(h, wg_bf, b_gate[l])
            merged = _merge(ya, yb, w_proj_a[l], wb_bf, gates)
            xb, xg, r2 = _out_proj(merged, wo_bf, xb, norm_mlp[l])
            u, w2_bf = _ff1(xg, r2, w1_bf, w_ff2[l])
            xb = _ff2(u, w2_bf, xb)
        outs.append(_rmsnorm(xb, norm_final, x.dtype))
    if batch == 1:
        return outs[0].reshape(1, seq, d)
    return jnp.stack(outs, axis=0)
```

```python
import functools
import math

import jax
import jax.numpy as jnp
from jax import lax
from jax.experimental import pallas as pl
from jax.experimental.pallas import tpu as pltpu

D_MODEL = 4096
HEAD_DIM = 128
DIL_PAIRS = ((128, 1), (512, 4), (2048, 16))
N_GROUPS = len(DIL_PAIRS)
HEADS_PER_GROUP = 4
GROUP_COLS = HEADS_PER_GROUP * HEAD_DIM
A_WIDTH = N_GROUPS * GROUP_COLS
A_OUT = GROUP_COLS
B_HEADS = 8
B_QK = B_HEADS * 2 * HEAD_DIM
B_V = B_HEADS * 2 * HEAD_DIM
ROPE_THETA = 500000.0
ROT_DIM = HEAD_DIM // 4
ROT_HALF = ROT_DIM // 2
EPS = 1e-6
QK_SCALE = 1.0 / math.sqrt(HEAD_DIM)
LOG2_E = math.log2(math.e)
WINDOW_STEPS = 128
BF16_SUBLANES = 16
ONES_ROWS = BF16_SUBLANES

COL_QA = 0
COL_QB = 3 * A_WIDTH
COL_VB = COL_QB + 2 * B_QK
COL_GATE = COL_VB + B_V

VMEM_LIMIT_BYTES = 56 * 1024 * 1024

F32 = jnp.float32
BF16 = jnp.bfloat16


def _params(*semantics):
    return pltpu.CompilerParams(dimension_semantics=semantics,
                                vmem_limit_bytes=VMEM_LIMIT_BYTES)


def _rmsnorm_kernel(x_ref, g_ref, o_ref):
    x = x_ref[...]
    ms = jnp.mean(x * x, axis=-1, keepdims=True)
    o_ref[...] = ((x * lax.rsqrt(ms + EPS)) * g_ref[...]).astype(o_ref.dtype)


def _rmsnorm(x, g, out_dtype, rows=512):
    s, d = x.shape
    return pl.pallas_call(
        _rmsnorm_kernel,
        grid=(s // rows,),
        in_specs=[pl.BlockSpec((rows, d), lambda i: (i, 0)),
                  pl.BlockSpec((1, d), lambda i: (0, 0))],
        out_specs=pl.BlockSpec((rows, d), lambda i: (i, 0)),
        out_shape=jax.ShapeDtypeStruct((s, d), out_dtype),
        compiler_params=_params("parallel"),
        name="rmsnorm",
    )(x, g.reshape(1, d))


def _rope_tables(seq):
    pos = jnp.arange(seq, dtype=F32)
    inv = ROPE_THETA ** (-jnp.arange(0, ROT_DIM, 2, dtype=F32) / ROT_DIM)
    ang = pos[:, None] * inv[None, :]
    cos, sin = jnp.cos(ang), jnp.sin(ang)
    ones = jnp.ones((seq, HEAD_DIM - ROT_DIM), F32)
    zeros = jnp.zeros((seq, HEAD_DIM - ROT_DIM), F32)
    zh = jnp.zeros((seq, ROT_HALF), F32)
    cos_t = jnp.concatenate([cos, cos, ones], axis=1)
    sin_lo = jnp.concatenate([-sin, zh, zeros], axis=1)
    sin_hi = jnp.concatenate([zh, sin, zeros], axis=1)
    rot = jnp.stack([cos_t, sin_lo, sin_hi])
    ident = jnp.stack([jnp.ones_like(cos_t), jnp.zeros_like(cos_t), jnp.zeros_like(cos_t)])
    return jnp.stack([rot, ident])


def _rope_head(y, tab_ref, scale):
    x2 = pltpu.roll(y, HEAD_DIM - ROT_HALF, 1)
    x1 = pltpu.roll(y, ROT_HALF, 1)
    return (y * tab_ref[0, 0] + x2 * tab_ref[0, 1] + x1 * tab_ref[0, 2]) * scale


def _store_product(a_ref, w_ref, acc_ref, *_):
    acc_ref[...] = jnp.dot(a_ref[...], w_ref[...].astype(BF16), preferred_element_type=F32)


def _lagged_kernel(a_ref, w_ref, *refs, n_ep, n_out, n_tiles, n_j, epilogue):
    ep_refs = refs[:n_ep]
    out_refs = refs[n_ep:n_ep + n_out]
    scratch = refs[n_ep + n_out:]
    t = pl.program_id(0)
    jp = jnp.maximum(t - 1, 0) % n_j

    @pl.when(t == 0)
    def _():
        _store_product(a_ref, w_ref, *scratch)

    @pl.when(jnp.logical_and(t > 0, t < n_tiles))
    def _():
        epilogue(jp, *ep_refs, *out_refs, *scratch)
        _store_product(a_ref, w_ref, *scratch)

    @pl.when(t == n_tiles)
    def _():
        epilogue(jp, *ep_refs, *out_refs, *scratch)


def _lagged_matmul(name, epilogue, *, n_i, n_j, mm_args, mm_specs, ep_args, ep_specs,
                   out_shapes, out_specs, scratch_shapes):
    n_tiles = n_i * n_j

    def cur(f):
        return lambda t: f(*divmod(jnp.minimum(t, n_tiles - 1), n_j))

    def prev(f):
        return lambda t: f(*divmod(jnp.maximum(t - 1, 0), n_j))

    in_specs = ([pl.BlockSpec(shape, cur(f)) for shape, f in mm_specs]
                + [pl.BlockSpec(shape, prev(f)) for shape, f in ep_specs])
    return pl.pallas_call(
        functools.partial(_lagged_kernel, n_ep=len(ep_args), n_out=len(out_shapes),
                          n_tiles=n_tiles, n_j=n_j, epilogue=epilogue),
        grid=(n_tiles + 1,),
        in_specs=in_specs,
        out_specs=[pl.BlockSpec(shape, prev(f)) for shape, f in out_specs],
        out_shape=out_shapes,
        scratch_shapes=scratch_shapes,
        compiler_params=_params("arbitrary"),
        name=name,
    )(*mm_args, *ep_args)


def _proj_a_epilogue(jp, tab_ref, o_ref, acc_ref, y_ref, *, dilation):
    tm = acc_ref.shape[0]
    n = tm // dilation
    scale = jnp.where(jp == 0, QK_SCALE, 1.0).astype(F32)
    for hh in range(HEADS_PER_GROUP):
        y = _rope_head(acc_ref[:, hh * HEAD_DIM:(hh + 1) * HEAD_DIM], tab_ref, scale)
        if dilation == 1:
            o_ref[0, hh, 0] = y.astype(o_ref.dtype)
            continue
        y_ref[hh] = y
        for rho in range(dilation):
            o_ref[0, hh, rho] = y_ref[hh, pl.ds(rho, n, stride=dilation), :].astype(o_ref.dtype)


def _proj_a(h, w_in, tables, group, tm=1024):
    s, d = h.shape
    dilation = DIL_PAIRS[group][1]
    sub_len = s // dilation
    blocks_per_section = A_WIDTH // GROUP_COLS
    return _lagged_matmul(
        f"proj_a_g{group}", functools.partial(_proj_a_epilogue, dilation=dilation),
        n_i=s // tm, n_j=3,
        mm_args=[h, w_in],
        mm_specs=[((tm, d), lambda i, j: (i, 0)),
                  ((d, GROUP_COLS), lambda i, j: (0, j * blocks_per_section + group))],
        ep_args=[tables],
        ep_specs=[((1, 3, tm, HEAD_DIM), lambda i, j: (j // 2, 0, i, 0))],
        out_shapes=[jax.ShapeDtypeStruct((3, HEADS_PER_GROUP, dilation, sub_len, HEAD_DIM), BF16)],
        out_specs=[((1, HEADS_PER_GROUP, dilation, tm // dilation, HEAD_DIM),
                    lambda i, j: (j, 0, 0, i, 0))],
        scratch_shapes=[pltpu.VMEM((tm, GROUP_COLS), F32),
                        pltpu.VMEM((HEADS_PER_GROUP, tm, HEAD_DIM), F32)])[0]


def _proj_bqk_epilogue(jp, tab_ref, o_ref, acc_ref, *, q_tiles):
    scale = jnp.where(jp < q_tiles, QK_SCALE * LOG2_E, 1.0).astype(F32)
    for hh in range(o_ref.shape[0]):
        y = _rope_head(acc_ref[:, hh * HEAD_DIM:(hh + 1) * HEAD_DIM], tab_ref, scale)
        o_ref[hh] = y.astype(o_ref.dtype)


def _proj_bqk(h, w_in, tables, tm=1024, tn=512):
    s, d = h.shape
    heads_per_tile = tn // HEAD_DIM
    n = B_QK + B_QK
    return _lagged_matmul(
        "proj_b_qk", functools.partial(_proj_bqk_epilogue, q_tiles=B_QK // tn),
        n_i=s // tm, n_j=n // tn,
        mm_args=[h, w_in],
        mm_specs=[((tm, d), lambda i, j: (i, 0)),
                  ((d, tn), lambda i, j: (0, COL_QB // tn + j))],
        ep_args=[tables],
        ep_specs=[((1, 3, tm, HEAD_DIM), lambda i, j: (0, 0, i, 0))],
        out_shapes=[jax.ShapeDtypeStruct((n // HEAD_DIM, s, HEAD_DIM), BF16)],
        out_specs=[((heads_per_tile, tm, HEAD_DIM), lambda i, j: (j, i, 0))],
        scratch_shapes=[pltpu.VMEM((tm, tn), F32)])[0]


def _proj_bv_kernel(h_ref, w_ref, o_ref):
    acc = jnp.dot(h_ref[...], w_ref[...].astype(BF16), preferred_element_type=F32)
    heads, blocks, rows, tk = o_ref.shape
    width = rows - ONES_ROWS
    for hv in range(heads):
        vt = acc[:, hv * width:(hv + 1) * width].T
        for c in range(blocks):
            o_ref[hv, c, :width, :] = vt[:, c * tk:(c + 1) * tk].astype(o_ref.dtype)
            o_ref[hv, c, width:, :] = jnp.ones((ONES_ROWS, tk), o_ref.dtype)


def _proj_bv(h, w_in, tm=1024, tn=512, tk=512):
    s, d = h.shape
    width = 2 * HEAD_DIM
    rows = width + ONES_ROWS
    return pl.pallas_call(
        _proj_bv_kernel,
        grid=(s // tm, B_V // tn),
        in_specs=[pl.BlockSpec((tm, d), lambda i, j: (i, 0)),
                  pl.BlockSpec((d, tn), lambda i, j: (0, COL_VB // tn + j))],
        out_specs=pl.BlockSpec((tn // width, tm // tk, rows, tk), lambda i, j: (j, i, 0, 0)),
        out_shape=jax.ShapeDtypeStruct((B_HEADS, s // tk, rows, tk), BF16),
        compiler_params=_params("parallel", "arbitrary"),
        name="proj_b_v",
    )(h, w_in)


def _proj_gate_kernel(h_ref, w_ref, b_ref, o_ref):
    z = jnp.dot(h_ref[...], w_ref[...], preferred_element_type=F32) + b_ref[...]
    o_ref[...] = (0.5 * jnp.tanh(0.5 * z) + 0.5).astype(o_ref.dtype)


def _proj_gate(h, w_gate, b_gate, tm=1024, tn=1024):
    s, d = h.shape
    n = w_gate.shape[1]
    return pl.pallas_call(
        _proj_gate_kernel,
        grid=(s // tm, n // tn),
        in_specs=[pl.BlockSpec((tm, d), lambda i, j: (i, 0)),
                  pl.BlockSpec((d, tn), lambda i, j: (0, j)),
                  pl.BlockSpec((1, tn), lambda i, j: (0, j))],
        out_specs=pl.BlockSpec((tm, tn), lambda i, j: (i, j)),
        out_shape=jax.ShapeDtypeStruct((s, n), BF16),
        compiler_params=_params("parallel", "arbitrary"),
        name="proj_gate",
    )(h, w_gate, b_gate.reshape(1, n))


def _attn_a_kernel(q_ref, kc_ref, kp_ref, vc_ref, vp_ref, o_ref, lse_ref, kk_ref, vv_ref,
                   *, sub_len):
    w = WINDOW_STEPS
    t = q_ref.shape[1]
    c = pl.program_id(0)
    kk_ref[0:w] = kp_ref[0]
    kk_ref[w:] = kc_ref[0]
    vv_ref[0:w] = vp_ref[0]
    vv_ref[w:] = vc_ref[0]
    qp = lax.broadcasted_iota(jnp.int32, (w, 2 * w), 0)
    kp = lax.broadcasted_iota(jnp.int32, (w, 2 * w), 1)
    band = (kp >= qp) & (kp <= qp + w)
    band_start = band & (kp >= w)
    if t >= sub_len:
        masks = [band_start if (b * w) % sub_len == 0 else band for b in range(t // w)]
    else:
        is_start = (c % (sub_len // t)) == 0
        masks = [band & (kp >= jnp.where(is_start, w, 0))] + [band] * (t // w - 1)
    for b in range(t // w):
        q = q_ref[0, b * w:(b + 1) * w]
        k = kk_ref[b * w:(b + 2) * w]
        v = vv_ref[b * w:(b + 2) * w]
        s = lax.dot_general(q, k, (((1,), (1,)), ((), ())), preferred_element_type=F32)
        s = jnp.where(masks[b], s, -jnp.inf)
        m = jnp.max(s, axis=-1, keepdims=True)
        p = jnp.exp(s - m)
        l = jnp.sum(p, axis=-1, keepdims=True)
        o = jnp.dot(p.astype(v.dtype), v, preferred_element_type=F32)
        o_ref[b * w:(b + 1) * w] = o / l
        lse_ref[b * w:(b + 1) * w] = jnp.broadcast_to(m + jnp.log(l), (w, HEAD_DIM))


def _attn_a(qkv, t=4096):
    _, heads, dilation, sub_len, hd = qkv.shape
    rows = heads * dilation * sub_len
    flat = qkv.reshape(3, rows, hd)
    w = WINDOW_STEPS
    assert t % sub_len == 0 or sub_len % t == 0
    bpc = t // w
    cur = lambda sec: pl.BlockSpec((1, t, hd), lambda c: (sec, c, 0))
    prev = lambda sec: pl.BlockSpec((1, w, hd), lambda c: (sec, jnp.maximum(c * bpc - 1, 0), 0))
    out_spec = pl.BlockSpec((t, hd), lambda c: (c, 0))
    out, lse = pl.pallas_call(
        functools.partial(_attn_a_kernel, sub_len=sub_len),
        grid=(rows // t,),
        in_specs=[cur(0), cur(1), prev(1), cur(2), prev(2)],
        out_specs=[out_spec, out_spec],
        out_shape=[jax.ShapeDtypeStruct((rows, hd), F32)] * 2,
        scratch_shapes=[pltpu.VMEM((t + w, hd), BF16)] * 2,
        compiler_params=_params("parallel"),
        name=f"attn_a_r{dilation}",
    )(flat, flat, flat, flat, flat)
    shape = (heads, dilation, sub_len, hd)
    return out.reshape(shape), lse.reshape(shape)


def _mix_a_kernel(o0_ref, l0_ref, o1_ref, l1_ref, o2_ref, l2_ref, y_ref, nat_ref):
    tm = y_ref.shape[0]
    refs = ((o0_ref, l0_ref), (o1_ref, l1_ref), (o2_ref, l2_ref))
    r_max = DIL_PAIRS[-1][1]
    n = tm // r_max
    for hh in range(HEADS_PER_GROUP):
        for rho in range(r_max):
            outs, lses = [], []
            for (o_ref, l_ref), (_, r) in zip(refs, DIL_PAIRS):
                step = r_max // r
                rows = pl.ds(rho // r, n, stride=step) if step > 1 else pl.ds(0, n)
                outs.append(o_ref[hh, rho % r, rows, :])
                lses.append(l_ref[hh, rho % r, rows, :])
            m = jnp.maximum(jnp.maximum(lses[0], lses[1]), lses[2])
            ws = [jnp.exp(l - m) for l in lses]
            num = ws[0] * outs[0] + ws[1] * outs[1] + ws[2] * outs[2]
            den = ws[0] + ws[1] + ws[2]
            nat_ref[hh, pl.ds(rho, n, stride=r_max), :] = num / den
        y_ref[:, hh * HEAD_DIM:(hh + 1) * HEAD_DIM] = nat_ref[hh].astype(y_ref.dtype)


def _mix_a(group_outs, seq, tm=512):
    in_specs, args = [], []
    for (o, l), (_, r) in zip(group_outs, DIL_PAIRS):
        spec = pl.BlockSpec((HEADS_PER_GROUP, r, tm // r, HEAD_DIM), lambda i: (0, 0, i, 0))
        in_specs += [spec, spec]
        args += [o, l]
    return pl.pallas_call(
        _mix_a_kernel,
        grid=(seq // tm,),
        in_specs=in_specs,
        out_specs=pl.BlockSpec((tm, A_OUT), lambda i: (i, 0)),
        out_shape=jax.ShapeDtypeStruct((seq, A_OUT), BF16),
        scratch_shapes=[pltpu.VMEM((HEADS_PER_GROUP, tm, HEAD_DIM), F32)],
        compiler_params=_params("parallel"),
        name="mix_a",
    )(*args)


def _attn_b_kernel(lq1_ref, lk1_ref, lq2_ref, lk2_ref, g_ref, q_ref, qn_ref, k_ref, vt_ref,
                   *rest, lam_init, cast_cols):
    n_in = len(cast_cols)
    n_out = sum(len(cols) for cols in cast_cols)
    w_refs = rest[:n_in]
    o_ref = rest[n_in]
    wo_refs = rest[n_in + 1:n_in + 1 + n_out]
    m_ref, acc_ref, st_ref, mb_ref = rest[n_in + 1 + n_out:]
    tq = q_ref.shape[1]
    tk = vt_ref.shape[3]
    width = vt_ref.shape[2] - ONES_ROWS
    qi = pl.program_id(1)
    m_ref[...] = jnp.full(m_ref.shape, -jnp.inf, F32)
    acc_ref[...] = jnp.zeros(acc_ref.shape, F32)

    def scores_to(slot, kv, q_src=q_ref, diagonal=False):
        start = pl.multiple_of(kv * tk, tk)
        for i in range(2):
            st = lax.dot_general(
                k_ref[i, pl.ds(start, tk), :], q_src[i], (((1,), (1,)), ((), ())),
                preferred_element_type=F32)
            if diagonal:
                key = lax.broadcasted_iota(jnp.int32, st.shape, 0)
                query = lax.broadcasted_iota(jnp.int32, st.shape, 1)
                st = jnp.where(key <= query, st, -jnp.inf)
            st_ref[slot, i] = st
            mb_ref[slot, i] = jnp.max(st, axis=0, keepdims=True)

    def softmax_pv(slot, kv):
        vt = vt_ref[0, kv]
        for i in range(2):
            st = st_ref[slot, i]
            m_prev = m_ref[i]
            m_new = jnp.maximum(m_prev, mb_ref[slot, i])
            alpha = jnp.exp2(m_prev - m_new)
            p = jnp.exp2((st - m_new).astype(vt.dtype))
            acc_ref[i] = alpha * acc_ref[i] + jnp.dot(vt, p, preferred_element_type=F32)
            m_ref[i] = m_new

    @pl.when(qi == 0)
    def _():
        scores_to(0, 0, diagonal=True)

    def pair(j):
        scores_to(1, 2 * j + 1)
        softmax_pv(0, 2 * j)
        scores_to(0, 2 * j + 2)
        softmax_pv(1, 2 * j + 1)

    def two_pairs(j, carry):
        pair(2 * j)
        pair(2 * j + 1)
        return carry

    n_pairs = jnp.maximum(qi - 1, 0) // 2
    lax.fori_loop(0, n_pairs // 2, two_pairs, 0)

    @pl.when(n_pairs % 2 == 1)
    def _():
        pair(n_pairs - 1)

    @pl.when(qi == 0)
    def _():
        softmax_pv(0, 0)
        scores_to(0, 0, qn_ref)

    @pl.when(qi % 2 == 1)
    def _():
        scores_to(1, qi, diagonal=True)
        softmax_pv(0, qi - 1)
        scores_to(0, 0, qn_ref)
        softmax_pv(1, qi)

    @pl.when(jnp.logical_and(qi % 2 == 0, qi > 0))
    def _():
        scores_to(1, qi - 1)
        softmax_pv(0, qi - 2)
        scores_to(0, qi, diagonal=True)
        softmax_pv(1, qi - 1)
        softmax_pv(0, qi)
        scores_to(0, 0, qn_ref)

    lam = (jnp.exp(jnp.sum(lq1_ref[...] * lk1_ref[...], axis=-1, keepdims=True))
           - jnp.exp(jnp.sum(lq2_ref[...] * lk2_ref[...], axis=-1, keepdims=True))
           + lam_init)
    inv_l = [1.0 / acc_ref[i, width:width + 1, :] for i in range(2)]
    ot = acc_ref[0, :width, :] * inv_l[0] - lam * (acc_ref[1, :width, :] * inv_l[1])
    ms = jnp.mean(ot * ot, axis=0, keepdims=True)
    yt = (ot * lax.rsqrt(ms + EPS)) * g_ref[...]
    o_ref[...] = (yt * (1.0 - lam_init)).T.astype(o_ref.dtype)

    wo_iter = iter(wo_refs)
    for w_ref, cols in zip(w_refs, cast_cols):
        for start, size in cols:
            wo_ref = next(wo_iter)
            wo_ref[...] = w_ref[:, start:start + size].astype(wo_ref.dtype)


def _attn_b(qk, vt, lam_vecs, subln_g, lam_init, cast_weights):
    _, s, hd = qk.shape
    _, n_kv, rows, tk = vt.shape
    width = rows - ONES_ROWS
    tq = tk
    n_q = s // tq
    n_steps = B_HEADS * n_q
    vec = pl.BlockSpec((1, hd), lambda h, i: (0, 0))
    chunk = lambda h, i: (h * n_q + i, 0)
    w_specs, wo_specs, wo_shapes = [], [], []
    for w, cols in cast_weights:
        assert w.shape[0] % (n_steps * BF16_SUBLANES) == 0, w.shape
        rows_per_step = w.shape[0] // n_steps
        w_specs.append(pl.BlockSpec((rows_per_step, w.shape[1]), chunk))
        for _, size in cols:
            wo_specs.append(pl.BlockSpec((rows_per_step, size), chunk))
            wo_shapes.append(jax.ShapeDtypeStruct((w.shape[0], size), BF16))
    return pl.pallas_call(
        functools.partial(_attn_b_kernel, lam_init=lam_init,
                          cast_cols=tuple(tuple(cols) for _, cols in cast_weights)),
        grid=(B_HEADS, n_q),
        in_specs=[vec, vec, vec, vec,
                  pl.BlockSpec((width, 1), lambda h, i: (0, 0)),
                  pl.BlockSpec((2, tq, hd), lambda h, i: (h, i, 0)),
                  pl.BlockSpec((2, tq, hd), lambda h, i: (h, jnp.minimum(i + 1, n_q - 1), 0)),
                  pl.BlockSpec((2, s, hd), lambda h, i: (B_HEADS + h, 0, 0)),
                  pl.BlockSpec((1, n_kv, rows, tk), lambda h, i: (h, 0, 0, 0))] + w_specs,
        out_specs=[pl.BlockSpec((tq, width), lambda h, i: (i, h))] + wo_specs,
        out_shape=[jax.ShapeDtypeStruct((s, B_HEADS * width), BF16)] + wo_shapes,
        scratch_shapes=[pltpu.VMEM((2, 1, tq), F32),
                        pltpu.VMEM((2, rows, tq), F32),
                        pltpu.VMEM((2, 2, tk, tq), F32),
                        pltpu.VMEM((2, 2, 1, tq), F32)],
        compiler_params=_params("arbitrary", "arbitrary"),
        name="attn_b",
    )(*[x.reshape(1, hd) for x in lam_vecs], subln_g.reshape(width, 1), qk, qk, qk, vt,
      *[w for w, _ in cast_weights])


def _merge_kernel(ya_ref, yb_ref, wa_ref, wb_ref, ga_ref, gb_ref, o_ref):
    pa = jnp.dot(ya_ref[...], wa_ref[...].astype(BF16), preferred_element_type=F32)
    pb = jnp.dot(yb_ref[...], wb_ref[...], preferred_element_type=F32)
    o_ref[...] = (ga_ref[...].astype(F32) * pa + gb_ref[...].astype(F32) * pb).astype(o_ref.dtype)


def _merge(ya, yb, wa, wb, gates, tm=1024, tn=1024):
    s = ya.shape[0]
    n = wa.shape[1]
    return pl.pallas_call(
        _merge_kernel,
        grid=(s // tm, n // tn),
        in_specs=[pl.BlockSpec((tm, ya.shape[1]), lambda i, j: (i, 0)),
                  pl.BlockSpec((tm, yb.shape[1]), lambda i, j: (i, 0)),
                  pl.BlockSpec((wa.shape[0], tn), lambda i, j: (0, j)),
                  pl.BlockSpec((wb.shape[0], tn), lambda i, j: (0, j)),
                  pl.BlockSpec((tm, tn), lambda i, j: (i, j)),
                  pl.BlockSpec((tm, tn), lambda i, j: (i, n // tn + j))],
        out_specs=pl.BlockSpec((tm, tn), lambda i, j: (i, j)),
        out_shape=jax.ShapeDtypeStruct((s, n), BF16),
        compiler_params=_params("parallel", "arbitrary"),
        name="merge",
    )(ya, yb, wa, wb, gates, gates)


def _out_proj_kernel(a_ref, b_ref, x_ref, g_ref, x1_ref, xg_ref, r2_ref, *, n_cols):
    j = pl.program_id(1)
    x1 = x_ref[...] + jnp.dot(a_ref[...], b_ref[...], preferred_element_type=F32)
    x1_ref[...] = x1
    xg_ref[...] = (x1 * g_ref[...]).astype(xg_ref.dtype)
    part = jnp.broadcast_to(jnp.sum(x1 * x1, axis=-1, keepdims=True), r2_ref.shape)

    @pl.when(j == 0)
    def _():
        r2_ref[...] = part

    @pl.when(j > 0)
    def _():
        r2_ref[...] += part

    @pl.when(j == n_cols // x_ref.shape[1] - 1)
    def _():
        r2_ref[...] = 1.0 / (r2_ref[...] * (1.0 / n_cols) + EPS)


def _out_proj(a, b, x, g, tm=1024, tn=512):
    s, kdim = a.shape
    n = b.shape[1]
    lanes = HEAD_DIM
    return pl.pallas_call(
        functools.partial(_out_proj_kernel, n_cols=n),
        grid=(s // tm, n // tn),
        in_specs=[pl.BlockSpec((tm, kdim), lambda i, j: (i, 0)),
                  pl.BlockSpec((kdim, tn), lambda i, j: (0, j)),
                  pl.BlockSpec((tm, tn), lambda i, j: (i, j)),
                  pl.BlockSpec((1, tn), lambda i, j: (0, j))],
        out_specs=[pl.BlockSpec((tm, tn), lambda i, j: (i, j)),
                   pl.BlockSpec((tm, tn), lambda i, j: (i, j)),
                   pl.BlockSpec((tm, lanes), lambda i, j: (i, 0))],
        out_shape=[jax.ShapeDtypeStruct((s, n), F32),
                   jax.ShapeDtypeStruct((s, n), BF16),
                   jax.ShapeDtypeStruct((s, lanes), F32)],
        compiler_params=_params("parallel", "arbitrary"),
        name="out_proj",
    )(a, b, x, g.reshape(1, n))


def _ff1_kernel(xg_ref, r2_ref, b_ref, o_ref):
    lanes = r2_ref.shape[1]
    u = jnp.maximum(jnp.dot(xg_ref[...], b_ref[...], preferred_element_type=F32), 0.0)
    for c in range(o_ref.shape[1] // lanes):
        uc = u[:, c * lanes:(c + 1) * lanes]
        o_ref[:, c * lanes:(c + 1) * lanes] = (uc * uc * r2_ref[...]).astype(o_ref.dtype)


def _ff1(xg, r2, b, tm=1024, tn=1024):
    s, kdim = xg.shape
    n = b.shape[1]
    return pl.pallas_call(
        _ff1_kernel,
        grid=(s // tm, n // tn),
        in_specs=[pl.BlockSpec((tm, kdim), lambda i, j: (i, 0)),
                  pl.BlockSpec((tm, r2.shape[1]), lambda i, j: (i, 0)),
                  pl.BlockSpec((kdim, tn), lambda i, j: (0, j))],
        out_specs=pl.BlockSpec((tm, tn), lambda i, j: (i, j)),
        out_shape=jax.ShapeDtypeStruct((s, n), BF16),
        compiler_params=_params("parallel", "arbitrary"),
        name="ff1",
    )(xg, r2, b)


def _ff2_kernel(a_ref, b_ref, x_ref, o_ref):
    @pl.when(pl.program_id(2) == 0)
    def _():
        o_ref[...] = x_ref[...]

    o_ref[...] += jnp.dot(a_ref[...], b_ref[...], preferred_element_type=F32)


def _ff2(a, b, x, tm=1024, tn=1024, tk=4096):
    s, kdim = a.shape
    n = b.shape[1]
    return pl.pallas_call(
        _ff2_kernel,
        grid=(s // tm, n // tn, kdim // tk),
        in_specs=[pl.BlockSpec((tm, tk), lambda i, j, k: (i, k)),
                  pl.BlockSpec((tk, tn), lambda i, j, k: (k, j)),
                  pl.BlockSpec((tm, tn), lambda i, j, k: (i, j))],
        out_specs=pl.BlockSpec((tm, tn), lambda i, j, k: (i, j)),
        out_shape=jax.ShapeDtypeStruct((s, n), F32),
        compiler_params=_params("parallel", "parallel", "arbitrary"),
        name="ff2",
    )(a, b, x)


def _lambda_init(layer_idx):
    return 0.8 - 0.6 * math.exp(-0.3 * layer_idx)


def kernel(x, norm_mix, w_in, b_gate, w_proj_a, w_proj_b, w_out, lambda_q1, lambda_k1,
           lambda_q2, lambda_k2, subln_g, norm_mlp, w_ff1, w_ff2, norm_final):
    batch, seq, d = x.shape
    depth = w_in.shape[0]
    tables = _rope_tables(seq)
    outs = []
    for b in range(batch):
        xb = x.reshape(seq, d) if batch == 1 else x[b]
        for l in range(depth):
            w_in_l = w_in[l]
            h = _rmsnorm(xb, norm_mix[l], BF16)
            qkb = _proj_bqk(h, w_in_l, tables)
            vb = _proj_bv(h, w_in_l)
            lam_init = _lambda_init(l)
            whole = lambda w: (w, [(0, w.shape[1])])
            yb, wa_in_bf, wg_bf, wb_bf, wo_bf, w1_bf, w2_bf = _attn_b(
                qkb, vb, (lambda_q1[l], lambda_k1[l], lambda_q2[l], lambda_k2[l]),
                subln_g[l], lam_init,
                [(w_in_l, [(COL_QA, 3 * A_WIDTH), (COL_GATE, 2 * D_MODEL)]),
                 whole(w_proj_b[l]), whole(w_out[l]), whole(w_ff1[l]), whole(w_ff2[l])])
            groups = []
            for g in range(N_GROUPS):
                groups.append(_attn_a(_proj_a(h, wa_in_bf, tables, g)))
            ya = _mix_a(groups, seq)
            gates = _proj_gate(h, wg_bf, b_gate[l])
            merged = _merge(ya, yb, w_proj_a[l], wb_bf, gates)
            xb, xg, r2 = _out_proj(merged, wo_bf, xb, norm_mlp[l])
            u = _ff1(xg, r2, w1_bf)
            xb = _ff2(u, w2_bf, xb)
        outs.append(_rmsnorm(xb, norm_final, x.dtype))
    if batch == 1:
        return outs[0].reshape(1, seq, d)
    return jnp.stack(outs, axis=0)
```

```python
import functools
import math

import jax
import jax.numpy as jnp
from jax import lax
from jax.experimental import pallas as pl
from jax.experimental.pallas import tpu as pltpu

D_MODEL = 4096
HEAD_DIM = 128
DIL_PAIRS = ((128, 1), (512, 4), (2048, 16))
N_GROUPS = len(DIL_PAIRS)
HEADS_PER_GROUP = 4
GROUP_COLS = HEADS_PER_GROUP * HEAD_DIM
A_WIDTH = N_GROUPS * GROUP_COLS
A_OUT = GROUP_COLS
B_HEADS = 8
B_QK = B_HEADS * 2 * HEAD_DIM
B_V = B_HEADS * 2 * HEAD_DIM
ROPE_THETA = 500000.0
ROT_DIM = HEAD_DIM // 4
ROT_HALF = ROT_DIM // 2
EPS = 1e-6
QK_SCALE = 1.0 / math.sqrt(HEAD_DIM)
LOG2_E = math.log2(math.e)
WINDOW_STEPS = 128
BF16_SUBLANES = 16
ONES_ROWS = BF16_SUBLANES

COL_QA = 0
COL_QB = 3 * A_WIDTH
COL_VB = COL_QB + 2 * B_QK
COL_GATE = COL_VB + B_V

VMEM_LIMIT_BYTES = 56 * 1024 * 1024

F32 = jnp.float32
BF16 = jnp.bfloat16


def _params(*semantics):
    return pltpu.CompilerParams(dimension_semantics=semantics,
                                vmem_limit_bytes=VMEM_LIMIT_BYTES)


def _rmsnorm_kernel(x_ref, g_ref, o_ref):
    x = x_ref[...]
    ms = jnp.mean(x * x, axis=-1, keepdims=True)
    o_ref[...] = ((x * lax.rsqrt(ms + EPS)) * g_ref[...]).astype(o_ref.dtype)


def _rmsnorm(x, g, out_dtype, rows=512):
    s, d = x.shape
    return pl.pallas_call(
        _rmsnorm_kernel,
        grid=(s // rows,),
        in_specs=[pl.BlockSpec((rows, d), lambda i: (i, 0)),
                  pl.BlockSpec((1, d), lambda i: (0, 0))],
        out_specs=pl.BlockSpec((rows, d), lambda i: (i, 0)),
        out_shape=jax.ShapeDtypeStruct((s, d), out_dtype),
        compiler_params=_params("parallel"),
        name="rmsnorm",
    )(x, g.reshape(1, d))


def _rope_tables(seq):
    pos = jnp.arange(seq, dtype=F32)
    inv = ROPE_THETA ** (-jnp.arange(0, ROT_DIM, 2, dtype=F32) / ROT_DIM)
    ang = pos[:, None] * inv[None, :]
    cos, sin = jnp.cos(ang), jnp.sin(ang)
    ones = jnp.ones((seq, HEAD_DIM - ROT_DIM), F32)
    zeros = jnp.zeros((seq, HEAD_DIM - ROT_DIM), F32)
    zh = jnp.zeros((seq, ROT_HALF), F32)
    cos_t = jnp.concatenate([cos, cos, ones], axis=1)
    sin_lo = jnp.concatenate([-sin, zh, zeros], axis=1)
    sin_hi = jnp.concatenate([zh, sin, zeros], axis=1)
    rot = jnp.stack([cos_t, sin_lo, sin_hi])
    ident = jnp.stack([jnp.ones_like(cos_t), jnp.zeros_like(cos_t), jnp.zeros_like(cos_t)])
    return jnp.stack([rot, ident])


def _rope_head(y, tab_ref, scale):
    x2 = pltpu.roll(y, HEAD_DIM - ROT_HALF, 1)
    x1 = pltpu.roll(y, ROT_HALF, 1)
    return (y * tab_ref[0, 0] + x2 * tab_ref[0, 1] + x1 * tab_ref[0, 2]) * scale


def _store_product(a_ref, w_ref, acc_ref, *_):
    acc_ref[...] = jnp.dot(a_ref[...], w_ref[...].astype(BF16), preferred_element_type=F32)


def _lagged_kernel(a_ref, w_ref, *refs, n_ep, n_out, n_tiles, n_j, epilogue):
    ep_refs = refs[:n_ep]
    out_refs = refs[n_ep:n_ep + n_out]
    scratch = refs[n_ep + n_out:]
    t = pl.program_id(0)
    jp = jnp.maximum(t - 1, 0) % n_j

    @pl.when(t == 0)
    def _():
        _store_product(a_ref, w_ref, *scratch)

    @pl.when(jnp.logical_and(t > 0, t < n_tiles))
    def _():
        epilogue(jp, *ep_refs, *out_refs, *scratch)
        _store_product(a_ref, w_ref, *scratch)

    @pl.when(t == n_tiles)
    def _():
        epilogue(jp, *ep_refs, *out_refs, *scratch)


def _lagged_matmul(name, epilogue, *, n_i, n_j, mm_args, mm_specs, ep_args, ep_specs,
                   out_shapes, out_specs, scratch_shapes):
    n_tiles = n_i * n_j

    def cur(f):
        return lambda t: f(*divmod(jnp.minimum(t, n_tiles - 1), n_j))

    def prev(f):
        return lambda t: f(*divmod(jnp.maximum(t - 1, 0), n_j))

    in_specs = ([pl.BlockSpec(shape, cur(f)) for shape, f in mm_specs]
                + [pl.BlockSpec(shape, prev(f)) for shape, f in ep_specs])
    return pl.pallas_call(
        functools.partial(_lagged_kernel, n_ep=len(ep_args), n_out=len(out_shapes),
                          n_tiles=n_tiles, n_j=n_j, epilogue=epilogue),
        grid=(n_tiles + 1,),
        in_specs=in_specs,
        out_specs=[pl.BlockSpec(shape, prev(f)) for shape, f in out_specs],
        out_shape=out_shapes,
        scratch_shapes=scratch_shapes,
        compiler_params=_params("arbitrary"),
        name=name,
    )(*mm_args, *ep_args)


def _proj_a_epilogue(jp, tab_ref, o_ref, acc_ref, y_ref, *, dilation):
    tm = acc_ref.shape[0]
    n = tm // dilation
    scale = jnp.where(jp == 0, QK_SCALE, 1.0).astype(F32)
    for hh in range(HEADS_PER_GROUP):
        y = _rope_head(acc_ref[:, hh * HEAD_DIM:(hh + 1) * HEAD_DIM], tab_ref, scale)
        if dilation == 1:
            o_ref[0, hh, 0] = y.astype(o_ref.dtype)
            continue
        y_ref[hh] = y
        for rho in range(dilation):
            o_ref[0, hh, rho] = y_ref[hh, pl.ds(rho, n, stride=dilation), :].astype(o_ref.dtype)


def _proj_a(h, w_in, tables, group, tm=1024):
    s, d = h.shape
    dilation = DIL_PAIRS[group][1]
    sub_len = s // dilation
    blocks_per_section = A_WIDTH // GROUP_COLS
    return _lagged_matmul(
        f"proj_a_g{group}", functools.partial(_proj_a_epilogue, dilation=dilation),
        n_i=s // tm, n_j=3,
        mm_args=[h, w_in],
        mm_specs=[((tm, d), lambda i, j: (i, 0)),
                  ((d, GROUP_COLS), lambda i, j: (0, j * blocks_per_section + group))],
        ep_args=[tables],
        ep_specs=[((1, 3, tm, HEAD_DIM), lambda i, j: (j // 2, 0, i, 0))],
        out_shapes=[jax.ShapeDtypeStruct((3, HEADS_PER_GROUP, dilation, sub_len, HEAD_DIM), BF16)],
        out_specs=[((1, HEADS_PER_GROUP, dilation, tm // dilation, HEAD_DIM),
                    lambda i, j: (j, 0, 0, i, 0))],
        scratch_shapes=[pltpu.VMEM((tm, GROUP_COLS), F32),
                        pltpu.VMEM((HEADS_PER_GROUP, tm, HEAD_DIM), F32)])[0]


def _proj_bqk_epilogue(jp, tab_ref, o_ref, acc_ref, *, q_tiles):
    scale = jnp.where(jp < q_tiles, QK_SCALE * LOG2_E, 1.0).astype(F32)
    for hh in range(o_ref.shape[0]):
        y = _rope_head(acc_ref[:, hh * HEAD_DIM:(hh + 1) * HEAD_DIM], tab_ref, scale)
        o_ref[hh] = y.astype(o_ref.dtype)


def _proj_bqk(h, w_in, tables, tm=1024, tn=512):
    s, d = h.shape
    heads_per_tile = tn // HEAD_DIM
    n = B_QK + B_QK
    return _lagged_matmul(
        "proj_b_qk", functools.partial(_proj_bqk_epilogue, q_tiles=B_QK // tn),
        n_i=s // tm, n_j=n // tn,
        mm_args=[h, w_in],
        mm_specs=[((tm, d), lambda i, j: (i, 0)),
                  ((d, tn), lambda i, j: (0, COL_QB // tn + j))],
        ep_args=[tables],
        ep_specs=[((1, 3, tm, HEAD_DIM), lambda i, j: (0, 0, i, 0))],
        out_shapes=[jax.ShapeDtypeStruct((n // HEAD_DIM, s, HEAD_DIM), BF16)],
        out_specs=[((heads_per_tile, tm, HEAD_DIM), lambda i, j: (j, i, 0))],
        scratch_shapes=[pltpu.VMEM((tm, tn), F32)])[0]


def _proj_bv_kernel(h_ref, w_ref, o_ref):
    acc = jnp.dot(h_ref[...], w_ref[...].astype(BF16), preferred_element_type=F32)
    heads, blocks, rows, tk = o_ref.shape
    width = rows - ONES_ROWS
    for hv in range(heads):
        vt = acc[:, hv * width:(hv + 1) * width].T
        for c in range(blocks):
            o_ref[hv, c, :width, :] = vt[:, c * tk:(c + 1) * tk].astype(o_ref.dtype)
            o_ref[hv, c, width:, :] = jnp.ones((ONES_ROWS, tk), o_ref.dtype)


def _proj_bv(h, w_in, tm=1024, tn=512, tk=512):
    s, d = h.shape
    width = 2 * HEAD_DIM
    rows = width + ONES_ROWS
    return pl.pallas_call(
        _proj_bv_kernel,
        grid=(s // tm, B_V // tn),
        in_specs=[pl.BlockSpec((tm, d), lambda i, j: (i, 0)),
                  pl.BlockSpec((d, tn), lambda i, j: (0, COL_VB // tn + j))],
        out_specs=pl.BlockSpec((tn // width, tm // tk, rows, tk), lambda i, j: (j, i, 0, 0)),
        out_shape=jax.ShapeDtypeStruct((B_HEADS, s // tk, rows, tk), BF16),
        compiler_params=_params("parallel", "arbitrary"),
        name="proj_b_v",
    )(h, w_in)


def _proj_gate_kernel(h_ref, w_ref, b_ref, o_ref):
    z = jnp.dot(h_ref[...], w_ref[...], preferred_element_type=F32) + b_ref[...]
    o_ref[...] = (0.5 * jnp.tanh(0.5 * z) + 0.5).astype(o_ref.dtype)


def _proj_gate(h, w_gate, b_gate, tm=1024, tn=1024):
    s, d = h.shape
    n = w_gate.shape[1]
    return pl.pallas_call(
        _proj_gate_kernel,
        grid=(s // tm, n // tn),
        in_specs=[pl.BlockSpec((tm, d), lambda i, j: (i, 0)),
                  pl.BlockSpec((d, tn), lambda i, j: (0, j)),
                  pl.BlockSpec((1, tn), lambda i, j: (0, j))],
        out_specs=pl.BlockSpec((tm, tn), lambda i, j: (i, j)),
        out_shape=jax.ShapeDtypeStruct((s, n), BF16),
        compiler_params=_params("parallel", "arbitrary"),
        name="proj_gate",
    )(h, w_gate, b_gate.reshape(1, n))


def _attn_a_kernel(q_ref, kc_ref, kp_ref, vc_ref, vp_ref, o_ref, lse_ref, kk_ref, vv_ref,
                   *, sub_len):
    w = WINDOW_STEPS
    t = q_ref.shape[1]
    c = pl.program_id(0)
    kk_ref[0:w] = kp_ref[0]
    kk_ref[w:] = kc_ref[0]
    vv_ref[0:w] = vp_ref[0]
    vv_ref[w:] = vc_ref[0]
    qp = lax.broadcasted_iota(jnp.int32, (w, 2 * w), 0)
    kp = lax.broadcasted_iota(jnp.int32, (w, 2 * w), 1)
    band = (kp >= qp) & (kp <= qp + w)
    band_start = band & (kp >= w)
    if t >= sub_len:
        masks = [band_start if (b * w) % sub_len == 0 else band for b in range(t // w)]
    else:
        is_start = (c % (sub_len // t)) == 0
        masks = [band & (kp >= jnp.where(is_start, w, 0))] + [band] * (t // w - 1)
    for b in range(t // w):
        q = q_ref[0, b * w:(b + 1) * w]
        k = kk_ref[b * w:(b + 2) * w]
        v = vv_ref[b * w:(b + 2) * w]
        s = lax.dot_general(q, k, (((1,), (1,)), ((), ())), preferred_element_type=F32)
        s = jnp.where(masks[b], s, -jnp.inf)
        m = jnp.max(s, axis=-1, keepdims=True)
        p = jnp.exp(s - m)
        l = jnp.sum(p, axis=-1, keepdims=True)
        o = jnp.dot(p.astype(v.dtype), v, preferred_element_type=F32)
        o_ref[b * w:(b + 1) * w] = o / l
        lse_ref[b * w:(b + 1) * w] = jnp.broadcast_to(m + jnp.log(l), (w, HEAD_DIM))


def _attn_a(qkv, t=4096):
    _, heads, dilation, sub_len, hd = qkv.shape
    rows = heads * dilation * sub_len
    flat = qkv.reshape(3, rows, hd)
    w = WINDOW_STEPS
    assert t % sub_len == 0 or sub_len % t == 0
    bpc = t // w
    cur = lambda sec: pl.BlockSpec((1, t, hd), lambda c: (sec, c, 0))
    prev = lambda sec: pl.BlockSpec((1, w, hd), lambda c: (sec, jnp.maximum(c * bpc - 1, 0), 0))
    out_spec = pl.BlockSpec((t, hd), lambda c: (c, 0))
    out, lse = pl.pallas_call(
        functools.partial(_attn_a_kernel, sub_len=sub_len),
        grid=(rows // t,),
        in_specs=[cur(0), cur(1), prev(1), cur(2), prev(2)],
        out_specs=[out_spec, out_spec],
        out_shape=[jax.ShapeDtypeStruct((rows, hd), F32)] * 2,
        scratch_shapes=[pltpu.VMEM((t + w, hd), BF16)] * 2,
        compiler_params=_params("parallel"),
        name=f"attn_a_r{dilation}",
    )(flat, flat, flat, flat, flat)
    shape = (heads, dilation, sub_len, hd)
    return out.reshape(shape), lse.reshape(shape)


def _mix_a_kernel(o0_ref, l0_ref, o1_ref, l1_ref, o2_ref, l2_ref, y_ref, nat_ref):
    tm = y_ref.shape[0]
    refs = ((o0_ref, l0_ref), (o1_ref, l1_ref), (o2_ref, l2_ref))
    r_max = DIL_PAIRS[-1][1]
    n = tm // r_max
    for hh in range(HEADS_PER_GROUP):
        for rho in range(r_max):
            outs, lses = [], []
            for (o_ref, l_ref), (_, r) in zip(refs, DIL_PAIRS):
                step = r_max // r
                rows = pl.ds(rho // r, n, stride=step) if step > 1 else pl.ds(0, n)
                outs.append(o_ref[hh, rho % r, rows, :])
                lses.append(l_ref[hh, rho % r, rows, :])
            m = jnp.maximum(jnp.maximum(lses[0], lses[1]), lses[2])
            ws = [jnp.exp(l - m) for l in lses]
            num = ws[0] * outs[0] + ws[1] * outs[1] + ws[2] * outs[2]
            den = ws[0] + ws[1] + ws[2]
            nat_ref[hh, pl.ds(rho, n, stride=r_max), :] = num / den
        y_ref[:, hh * HEAD_DIM:(hh + 1) * HEAD_DIM] = nat_ref[hh].astype(y_ref.dtype)


def _mix_a(group_outs, seq, tm=1024):
    in_specs, args = [], []
    for (o, l), (_, r) in zip(group_outs, DIL_PAIRS):
        spec = pl.BlockSpec((HEADS_PER_GROUP, r, tm // r, HEAD_DIM), lambda i: (0, 0, i, 0))
        in_specs += [spec, spec]
        args += [o, l]
    return pl.pallas_call(
        _mix_a_kernel,
        grid=(seq // tm,),
        in_specs=in_specs,
        out_specs=pl.BlockSpec((tm, A_OUT), lambda i: (i, 0)),
        out_shape=jax.ShapeDtypeStruct((seq, A_OUT), BF16),
        scratch_shapes=[pltpu.VMEM((HEADS_PER_GROUP, tm, HEAD_DIM), F32)],
        compiler_params=_params("parallel"),
        name="mix_a",
    )(*args)


def _attn_b_kernel(lq1_ref, lk1_ref, lq2_ref, lk2_ref, g_ref, q_ref, qn_ref, k_ref, vt_ref,
                   *rest, lam_init, cast_cols):
    n_in = len(cast_cols)
    n_out = sum(len(cols) for cols in cast_cols)
    w_refs = rest[:n_in]
    o_ref = rest[n_in]
    wo_refs = rest[n_in + 1:n_in + 1 + n_out]
    m_ref, acc_ref, st_ref, mb_ref = rest[n_in + 1 + n_out:]
    tq = q_ref.shape[1]
    tk = vt_ref.shape[3]
    width = vt_ref.shape[2] - ONES_ROWS
    qi = pl.program_id(1)
    m_ref[...] = jnp.full(m_ref.shape, -jnp.inf, F32)
    acc_ref[...] = jnp.zeros(acc_ref.shape, F32)

    def scores_to(slot, kv, q_src=q_ref, diagonal=False):
        start = pl.multiple_of(kv * tk, tk)
        for i in range(2):
            st = lax.dot_general(
                k_ref[i, pl.ds(start, tk), :], q_src[i], (((1,), (1,)), ((), ())),
                preferred_element_type=F32)
            if diagonal:
                key = lax.broadcasted_iota(jnp.int32, st.shape, 0)
                query = lax.broadcasted_iota(jnp.int32, st.shape, 1)
                st = jnp.where(key <= query, st, -jnp.inf)
            st_ref[slot, i] = st
            mb_ref[slot, i] = jnp.max(st, axis=0, keepdims=True)

    def softmax_pv(slot, kv):
        vt = vt_ref[0, kv]
        for i in range(2):
            st = st_ref[slot, i]
            m_prev = m_ref[i]
            m_new = jnp.maximum(m_prev, mb_ref[slot, i])
            alpha = jnp.exp2(m_prev - m_new)
            p = jnp.exp2((st - m_new).astype(vt.dtype))
            acc_ref[i] = alpha * acc_ref[i] + jnp.dot(vt, p, preferred_element_type=F32)
            m_ref[i] = m_new

    @pl.when(qi == 0)
    def _():
        scores_to(0, 0, diagonal=True)

    def pair(j):
        scores_to(1, 2 * j + 1)
        softmax_pv(0, 2 * j)
        scores_to(0, 2 * j + 2)
        softmax_pv(1, 2 * j + 1)

    def two_pairs(j, carry):
        pair(2 * j)
        pair(2 * j + 1)
        return carry

    n_pairs = jnp.maximum(qi - 1, 0) // 2
    lax.fori_loop(0, n_pairs // 2, two_pairs, 0)

    @pl.when(n_pairs % 2 == 1)
    def _():
        pair(n_pairs - 1)

    @pl.when(qi == 0)
    def _():
        softmax_pv(0, 0)
        scores_to(0, 0, qn_ref)

    @pl.when(qi % 2 == 1)
    def _():
        scores_to(1, qi, diagonal=True)
        softmax_pv(0, qi - 1)
        scores_to(0, 0, qn_ref)
        softmax_pv(1, qi)

    @pl.when(jnp.logical_and(qi % 2 == 0, qi > 0))
    def _():
        scores_to(1, qi - 1)
        softmax_pv(0, qi - 2)
        scores_to(0, qi, diagonal=True)
        softmax_pv(1, qi - 1)
        softmax_pv(0, qi)
        scores_to(0, 0, qn_ref)

    lam = (jnp.exp(jnp.sum(lq1_ref[...] * lk1_ref[...], axis=-1, keepdims=True))
           - jnp.exp(jnp.sum(lq2_ref[...] * lk2_ref[...], axis=-1, keepdims=True))
           + lam_init)
    inv_l = [1.0 / acc_ref[i, width:width + 1, :] for i in range(2)]
    ot = acc_ref[0, :width, :] * inv_l[0] - lam * (acc_ref[1, :width, :] * inv_l[1])
    ms = jnp.mean(ot * ot, axis=0, keepdims=True)
    yt = (ot * lax.rsqrt(ms + EPS)) * g_ref[...]
    o_ref[...] = (yt * (1.0 - lam_init)).T.astype(o_ref.dtype)

    wo_iter = iter(wo_refs)
    for w_ref, cols in zip(w_refs, cast_cols):
        for start, size in cols:
            wo_ref = next(wo_iter)
            wo_ref[...] = w_ref[:, start:start + size].astype(wo_ref.dtype)


def _attn_b(qk, vt, lam_vecs, subln_g, lam_init, cast_weights):
    _, s, hd = qk.shape
    _, n_kv, rows, tk = vt.shape
    width = rows - ONES_ROWS
    tq = tk
    n_q = s // tq
    n_steps = B_HEADS * n_q
    vec = pl.BlockSpec((1, hd), lambda h, i: (0, 0))
    chunk = lambda h, i: (h * n_q + i, 0)
    w_specs, wo_specs, wo_shapes = [], [], []
    for w, cols in cast_weights:
        assert w.shape[0] % (n_steps * BF16_SUBLANES) == 0, w.shape
        rows_per_step = w.shape[0] // n_steps
        w_specs.append(pl.BlockSpec((rows_per_step, w.shape[1]), chunk))
        for _, size in cols:
            wo_specs.append(pl.BlockSpec((rows_per_step, size), chunk))
            wo_shapes.append(jax.ShapeDtypeStruct((w.shape[0], size), BF16))
    return pl.pallas_call(
        functools.partial(_attn_b_kernel, lam_init=lam_init,
                          cast_cols=tuple(tuple(cols) for _, cols in cast_weights)),
        grid=(B_HEADS, n_q),
        in_specs=[vec, vec, vec, vec,
                  pl.BlockSpec((width, 1), lambda h, i: (0, 0)),
                  pl.BlockSpec((2, tq, hd), lambda h, i: (h, i, 0)),
                  pl.BlockSpec((2, tq, hd), lambda h, i: (h, jnp.minimum(i + 1, n_q - 1), 0)),
                  pl.BlockSpec((2, s, hd), lambda h, i: (B_HEADS + h, 0, 0)),
                  pl.BlockSpec((1, n_kv, rows, tk), lambda h, i: (h, 0, 0, 0))] + w_specs,
        out_specs=[pl.BlockSpec((tq, width), lambda h, i: (i, h))] + wo_specs,
        out_shape=[jax.ShapeDtypeStruct((s, B_HEADS * width), BF16)] + wo_shapes,
        scratch_shapes=[pltpu.VMEM((2, 1, tq), F32),
                        pltpu.VMEM((2, rows, tq), F32),
                        pltpu.VMEM((2, 2, tk, tq), F32),
                        pltpu.VMEM((2, 2, 1, tq), F32)],
        compiler_params=_params("arbitrary", "arbitrary"),
        name="attn_b",
    )(*[x.reshape(1, hd) for x in lam_vecs], subln_g.reshape(width, 1), qk, qk, qk, vt,
      *[w for w, _ in cast_weights])


def _merge_kernel(ya_ref, yb_ref, wa_ref, wb_ref, ga_ref, gb_ref, o_ref):
    pa = jnp.dot(ya_ref[...], wa_ref[...].astype(BF16), preferred_element_type=F32)
    pb = jnp.dot(yb_ref[...], wb_ref[...], preferred_element_type=F32)
    o_ref[...] = (ga_ref[...].astype(F32) * pa + gb_ref[...].astype(F32) * pb).astype(o_ref.dtype)


def _merge(ya, yb, wa, wb, gates, tm=1024, tn=1024):
    s = ya.shape[0]
    n = wa.shape[1]
    return pl.pallas_call(
        _merge_kernel,
        grid=(s // tm, n // tn),
        in_specs=[pl.BlockSpec((tm, ya.shape[1]), lambda i, j: (i, 0)),
                  pl.BlockSpec((tm, yb.shape[1]), lambda i, j: (i, 0)),
                  pl.BlockSpec((wa.shape[0], tn), lambda i, j: (0, j)),
                  pl.BlockSpec((wb.shape[0], tn), lambda i, j: (0, j)),
                  pl.BlockSpec((tm, tn), lambda i, j: (i, j)),
                  pl.BlockSpec((tm, tn), lambda i, j: (i, n // tn + j))],
        out_specs=pl.BlockSpec((tm, tn), lambda i, j: (i, j)),
        out_shape=jax.ShapeDtypeStruct((s, n), BF16),
        compiler_params=_params("parallel", "arbitrary"),
        name="merge",
    )(ya, yb, wa, wb, gates, gates)


def _out_proj_kernel(a_ref, b_ref, x_ref, g_ref, x1_ref, xg_ref, r2_ref, *, n_cols):
    j = pl.program_id(1)
    x1 = x_ref[...] + jnp.dot(a_ref[...], b_ref[...], preferred_element_type=F32)
    x1_ref[...] = x1
    xg_ref[...] = (x1 * g_ref[...]).astype(xg_ref.dtype)
    part = jnp.broadcast_to(jnp.sum(x1 * x1, axis=-1, keepdims=True), r2_ref.shape)

    @pl.when(j == 0)
    def _():
        r2_ref[...] = part

    @pl.when(j > 0)
    def _():
        r2_ref[...] += part

    @pl.when(j == n_cols // x_ref.shape[1] - 1)
    def _():
        r2_ref[...] = 1.0 / (r2_ref[...] * (1.0 / n_cols) + EPS)


def _out_proj(a, b, x, g, tm=1024, tn=512):
    s, kdim = a.shape
    n = b.shape[1]
    lanes = HEAD_DIM
    return pl.pallas_call(
        functools.partial(_out_proj_kernel, n_cols=n),
        grid=(s // tm, n // tn),
        in_specs=[pl.BlockSpec((tm, kdim), lambda i, j: (i, 0)),
                  pl.BlockSpec((kdim, tn), lambda i, j: (0, j)),
                  pl.BlockSpec((tm, tn), lambda i, j: (i, j)),
                  pl.BlockSpec((1, tn), lambda i, j: (0, j))],
        out_specs=[pl.BlockSpec((tm, tn), lambda i, j: (i, j)),
                   pl.BlockSpec((tm, tn), lambda i, j: (i, j)),
                   pl.BlockSpec((tm, lanes), lambda i, j: (i, 0))],
        out_shape=[jax.ShapeDtypeStruct((s, n), F32),
                   jax.ShapeDtypeStruct((s, n), BF16),
                   jax.ShapeDtypeStruct((s, lanes), F32)],
        compiler_params=_params("parallel", "arbitrary"),
        name="out_proj",
    )(a, b, x, g.reshape(1, n))


def _ff1_kernel(xg_ref, r2_ref, b_ref, o_ref):
    lanes = r2_ref.shape[1]
    u = jnp.maximum(jnp.dot(xg_ref[...], b_ref[...], preferred_element_type=F32), 0.0)
    for c in range(o_ref.shape[1] // lanes):
        uc = u[:, c * lanes:(c + 1) * lanes]
        o_ref[:, c * lanes:(c + 1) * lanes] = (uc * uc * r2_ref[...]).astype(o_ref.dtype)


def _ff1(xg, r2, b, tm=1024, tn=1024):
    s, kdim = xg.shape
    n = b.shape[1]
    return pl.pallas_call(
        _ff1_kernel,
        grid=(s // tm, n // tn),
        in_specs=[pl.BlockSpec((tm, kdim), lambda i, j: (i, 0)),
                  pl.BlockSpec((tm, r2.shape[1]), lambda i, j: (i, 0)),
                  pl.BlockSpec((kdim, tn), lambda i, j: (0, j))],
        out_specs=pl.BlockSpec((tm, tn), lambda i, j: (i, j)),
        out_shape=jax.ShapeDtypeStruct((s, n), BF16),
        compiler_params=_params("parallel", "arbitrary"),
        name="ff1",
    )(xg, r2, b)


def _ff2_kernel(a_ref, b_ref, x_ref, o_ref):
    @pl.when(pl.program_id(2) == 0)
    def _():
        o_ref[...] = x_ref[...]

    o_ref[...] += jnp.dot(a_ref[...], b_ref[...], preferred_element_type=F32)


def _ff2(a, b, x, tm=1024, tn=1024, tk=4096):
    s, kdim = a.shape
    n = b.shape[1]
    return pl.pallas_call(
        _ff2_kernel,
        grid=(s // tm, n // tn, kdim // tk),
        in_specs=[pl.BlockSpec((tm, tk), lambda i, j, k: (i, k)),
                  pl.BlockSpec((tk, tn), lambda i, j, k: (k, j)),
                  pl.BlockSpec((tm, tn), lambda i, j, k: (i, j))],
        out_specs=pl.BlockSpec((tm, tn), lambda i, j, k: (i, j)),
        out_shape=jax.ShapeDtypeStruct((s, n), F32),
        compiler_params=_params("parallel", "parallel", "arbitrary"),
        name="ff2",
    )(a, b, x)


def _lambda_init(layer_idx):
    return 0.8 - 0.6 * math.exp(-0.3 * layer_idx)


def kernel(x, norm_mix, w_in, b_gate, w_proj_a, w_proj_b, w_out, lambda_q1, lambda_k1,
           lambda_q2, lambda_k2, subln_g, norm_mlp, w_ff1, w_ff2, norm_final):
    batch, seq, d = x.shape
    depth = w_in.shape[0]
    tables = _rope_tables(seq)
    outs = []
    for b in range(batch):
        xb = x.reshape(seq, d) if batch == 1 else x[b]
        for l in range(depth):
            w_in_l = w_in[l]
            h = _rmsnorm(xb, norm_mix[l], BF16)
            qkb = _proj_bqk(h, w_in_l, tables)
            vb = _proj_bv(h, w_in_l)
            lam_init = _lambda_init(l)
            whole = lambda w: (w, [(0, w.shape[1])])
            yb, wa_in_bf, wg_bf, wb_bf, wo_bf, w1_bf, w2_bf = _attn_b(
                qkb, vb, (lambda_q1[l], lambda_k1[l], lambda_q2[l], lambda_k2[l]),
                subln_g[l], lam_init,
                [(w_in_l, [(COL_QA, 3 * A_WIDTH), (COL_GATE, 2 * D_MODEL)]),
                 whole(w_proj_b[l]), whole(w_out[l]), whole(w_ff1[l]), whole(w_ff2[l])])
            groups = []
            for g in range(N_GROUPS):
                groups.append(_attn_a(_proj_a(h, wa_in_bf, tables, g)))
            ya = _mix_a(groups, seq)
            gates = _proj_gate(h, wg_bf, b_gate[l])
            merged = _merge(ya, yb, w_proj_a[l], wb_bf, gates)
            xb, xg, r2 = _out_proj(merged, wo_bf, xb, norm_mlp[l])
            u = _ff1(xg, r2, w1_bf)
            xb = _ff2(u, w2_bf, xb)
        outs.append(_rmsnorm(xb, norm_final, x.dtype))
    if batch == 1:
        return outs[0].reshape(1, seq, d)
    return jnp.stack(outs, axis=0)
```

```python
import functools
import math

import jax
import jax.numpy as jnp
from jax import lax
from jax.experimental import pallas as pl
from jax.experimental.pallas import tpu as pltpu

D_MODEL = 4096
HEAD_DIM = 128
DIL_PAIRS = ((128, 1), (512, 4), (2048, 16))
N_GROUPS = len(DIL_PAIRS)
HEADS_PER_GROUP = 4
GROUP_COLS = HEADS_PER_GROUP * HEAD_DIM
A_WIDTH = N_GROUPS * GROUP_COLS
A_OUT = GROUP_COLS
B_HEADS = 8
B_QK = B_HEADS * 2 * HEAD_DIM
B_V = B_HEADS * 2 * HEAD_DIM
ROPE_THETA = 500000.0
ROT_DIM = HEAD_DIM // 4
ROT_HALF = ROT_DIM // 2
EPS = 1e-6
QK_SCALE = 1.0 / math.sqrt(HEAD_DIM)
LOG2_E = math.log2(math.e)
WINDOW_STEPS = 128
BF16_SUBLANES = 16
ONES_ROWS = BF16_SUBLANES

COL_QA = 0
COL_QB = 3 * A_WIDTH
COL_VB = COL_QB + 2 * B_QK
COL_GATE = COL_VB + B_V

VMEM_LIMIT_BYTES = 56 * 1024 * 1024

F32 = jnp.float32
BF16 = jnp.bfloat16


def _params(*semantics):
    return pltpu.CompilerParams(dimension_semantics=semantics,
                                vmem_limit_bytes=VMEM_LIMIT_BYTES)


def _rmsnorm_kernel(x_ref, g_ref, o_ref):
    x = x_ref[...]
    ms = jnp.mean(x * x, axis=-1, keepdims=True)
    o_ref[...] = ((x * lax.rsqrt(ms + EPS)) * g_ref[...]).astype(o_ref.dtype)


def _rmsnorm(x, g, out_dtype, rows=512):
    s, d = x.shape
    return pl.pallas_call(
        _rmsnorm_kernel,
        grid=(s // rows,),
        in_specs=[pl.BlockSpec((rows, d), lambda i: (i, 0)),
                  pl.BlockSpec((1, d), lambda i: (0, 0))],
        out_specs=pl.BlockSpec((rows, d), lambda i: (i, 0)),
        out_shape=jax.ShapeDtypeStruct((s, d), out_dtype),
        compiler_params=_params("parallel"),
        name="rmsnorm",
    )(x, g.reshape(1, d))


def _rope_tables(seq):
    pos = jnp.arange(seq, dtype=F32)
    inv = ROPE_THETA ** (-jnp.arange(0, ROT_DIM, 2, dtype=F32) / ROT_DIM)
    ang = pos[:, None] * inv[None, :]
    cos, sin = jnp.cos(ang), jnp.sin(ang)
    ones = jnp.ones((seq, HEAD_DIM - ROT_DIM), F32)
    zeros = jnp.zeros((seq, HEAD_DIM - ROT_DIM), F32)
    zh = jnp.zeros((seq, ROT_HALF), F32)
    cos_t = jnp.concatenate([cos, cos, ones], axis=1)
    sin_lo = jnp.concatenate([-sin, zh, zeros], axis=1)
    sin_hi = jnp.concatenate([zh, sin, zeros], axis=1)
    rot = jnp.stack([cos_t, sin_lo, sin_hi])
    ident = jnp.stack([jnp.ones_like(cos_t), jnp.zeros_like(cos_t), jnp.zeros_like(cos_t)])
    return jnp.stack([rot, ident])


def _rope_head(y, tab_ref, scale):
    x2 = pltpu.roll(y, HEAD_DIM - ROT_HALF, 1)
    x1 = pltpu.roll(y, ROT_HALF, 1)
    return (y * tab_ref[0, 0] + x2 * tab_ref[0, 1] + x1 * tab_ref[0, 2]) * scale


def _store_product(a_ref, w_ref, acc_ref, *_):
    acc_ref[...] = jnp.dot(a_ref[...], w_ref[...].astype(BF16), preferred_element_type=F32)


def _lagged_kernel(a_ref, w_ref, *refs, n_ep, n_out, n_tiles, n_j, epilogue):
    ep_refs = refs[:n_ep]
    out_refs = refs[n_ep:n_ep + n_out]
    scratch = refs[n_ep + n_out:]
    t = pl.program_id(0)
    jp = jnp.maximum(t - 1, 0) % n_j

    @pl.when(t == 0)
    def _():
        _store_product(a_ref, w_ref, *scratch)

    @pl.when(jnp.logical_and(t > 0, t < n_tiles))
    def _():
        epilogue(jp, *ep_refs, *out_refs, *scratch)
        _store_product(a_ref, w_ref, *scratch)

    @pl.when(t == n_tiles)
    def _():
        epilogue(jp, *ep_refs, *out_refs, *scratch)


def _lagged_matmul(name, epilogue, *, n_i, n_j, mm_args, mm_specs, ep_args, ep_specs,
                   out_shapes, out_specs, scratch_shapes):
    n_tiles = n_i * n_j

    def cur(f):
        return lambda t: f(*divmod(jnp.minimum(t, n_tiles - 1), n_j))

    def prev(f):
        return lambda t: f(*divmod(jnp.maximum(t - 1, 0), n_j))

    in_specs = ([pl.BlockSpec(shape, cur(f)) for shape, f in mm_specs]
                + [pl.BlockSpec(shape, prev(f)) for shape, f in ep_specs])
    return pl.pallas_call(
        functools.partial(_lagged_kernel, n_ep=len(ep_args), n_out=len(out_shapes),
                          n_tiles=n_tiles, n_j=n_j, epilogue=epilogue),
        grid=(n_tiles + 1,),
        in_specs=in_specs,
        out_specs=[pl.BlockSpec(shape, prev(f)) for shape, f in out_specs],
        out_shape=out_shapes,
        scratch_shapes=scratch_shapes,
        compiler_params=_params("arbitrary"),
        name=name,
    )(*mm_args, *ep_args)


def _proj_a_epilogue(jp, tab_ref, o_ref, acc_ref, y_ref, *, dilation):
    tm = acc_ref.shape[0]
    n = tm // dilation
    scale = jnp.where(jp == 0, QK_SCALE, 1.0).astype(F32)
    for hh in range(HEADS_PER_GROUP):
        y = _rope_head(acc_ref[:, hh * HEAD_DIM:(hh + 1) * HEAD_DIM], tab_ref, scale)
        if dilation == 1:
            o_ref[0, hh, 0] = y.astype(o_ref.dtype)
            continue
        y_ref[hh] = y
        for rho in range(dilation):
            o_ref[0, hh, rho] = y_ref[hh, pl.ds(rho, n, stride=dilation), :].astype(o_ref.dtype)


def _proj_a(h, w_in, tables, group, tm=1024):
    s, d = h.shape
    dilation = DIL_PAIRS[group][1]
    sub_len = s // dilation
    blocks_per_section = A_WIDTH // GROUP_COLS
    return _lagged_matmul(
        f"proj_a_g{group}", functools.partial(_proj_a_epilogue, dilation=dilation),
        n_i=s // tm, n_j=3,
        mm_args=[h, w_in],
        mm_specs=[((tm, d), lambda i, j: (i, 0)),
                  ((d, GROUP_COLS), lambda i, j: (0, j * blocks_per_section + group))],
        ep_args=[tables],
        ep_specs=[((1, 3, tm, HEAD_DIM), lambda i, j: (j // 2, 0, i, 0))],
        out_shapes=[jax.ShapeDtypeStruct((3, HEADS_PER_GROUP, dilation, sub_len, HEAD_DIM), BF16)],
        out_specs=[((1, HEADS_PER_GROUP, dilation, tm // dilation, HEAD_DIM),
                    lambda i, j: (j, 0, 0, i, 0))],
        scratch_shapes=[pltpu.VMEM((tm, GROUP_COLS), F32),
                        pltpu.VMEM((HEADS_PER_GROUP, tm, HEAD_DIM), F32)])[0]


def _proj_bqk_epilogue(jp, tab_ref, o_ref, acc_ref, *, q_tiles):
    scale = jnp.where(jp < q_tiles, QK_SCALE * LOG2_E, 1.0).astype(F32)
    for hh in range(o_ref.shape[0]):
        y = _rope_head(acc_ref[:, hh * HEAD_DIM:(hh + 1) * HEAD_DIM], tab_ref, scale)
        o_ref[hh] = y.astype(o_ref.dtype)


def _proj_bqk(h, w_in, tables, tm=1024, tn=512):
    s, d = h.shape
    heads_per_tile = tn // HEAD_DIM
    n = B_QK + B_QK
    return _lagged_matmul(
        "proj_b_qk", functools.partial(_proj_bqk_epilogue, q_tiles=B_QK // tn),
        n_i=s // tm, n_j=n // tn,
        mm_args=[h, w_in],
        mm_specs=[((tm, d), lambda i, j: (i, 0)),
                  ((d, tn), lambda i, j: (0, COL_QB // tn + j))],
        ep_args=[tables],
        ep_specs=[((1, 3, tm, HEAD_DIM), lambda i, j: (0, 0, i, 0))],
        out_shapes=[jax.ShapeDtypeStruct((n // HEAD_DIM, s, HEAD_DIM), BF16)],
        out_specs=[((heads_per_tile, tm, HEAD_DIM), lambda i, j: (j, i, 0))],
        scratch_shapes=[pltpu.VMEM((tm, tn), F32)])[0]


def _proj_bv_kernel(h_ref, w_ref, o_ref):
    acc = jnp.dot(h_ref[...], w_ref[...].astype(BF16), preferred_element_type=F32)
    heads, blocks, rows, tk = o_ref.shape
    width = rows - ONES_ROWS
    for hv in range(heads):
        vt = acc[:, hv * width:(hv + 1) * width].T
        for c in range(blocks):
            o_ref[hv, c, :width, :] = vt[:, c * tk:(c + 1) * tk].astype(o_ref.dtype)
            o_ref[hv, c, width:, :] = jnp.ones((ONES_ROWS, tk), o_ref.dtype)


def _proj_bv(h, w_in, tm=1024, tn=512, tk=512):
    s, d = h.shape
    width = 2 * HEAD_DIM
    rows = width + ONES_ROWS
    return pl.pallas_call(
        _proj_bv_kernel,
        grid=(s // tm, B_V // tn),
        in_specs=[pl.BlockSpec((tm, d), lambda i, j: (i, 0)),
                  pl.BlockSpec((d, tn), lambda i, j: (0, COL_VB // tn + j))],
        out_specs=pl.BlockSpec((tn // width, tm // tk, rows, tk), lambda i, j: (j, i, 0, 0)),
        out_shape=jax.ShapeDtypeStruct((B_HEADS, s // tk, rows, tk), BF16),
        compiler_params=_params("parallel", "arbitrary"),
        name="proj_b_v",
    )(h, w_in)


def _proj_gate_kernel(h_ref, w_ref, b_ref, o_ref):
    z = jnp.dot(h_ref[...], w_ref[...], preferred_element_type=F32) + b_ref[...]
    o_ref[...] = (0.5 * jnp.tanh(0.5 * z) + 0.5).astype(o_ref.dtype)


def _proj_gate(h, w_gate, b_gate, tm=1024, tn=1024):
    s, d = h.shape
    n = w_gate.shape[1]
    return pl.pallas_call(
        _proj_gate_kernel,
        grid=(s // tm, n // tn),
        in_specs=[pl.BlockSpec((tm, d), lambda i, j: (i, 0)),
                  pl.BlockSpec((d, tn), lambda i, j: (0, j)),
                  pl.BlockSpec((1, tn), lambda i, j: (0, j))],
        out_specs=pl.BlockSpec((tm, tn), lambda i, j: (i, j)),
        out_shape=jax.ShapeDtypeStruct((s, n), BF16),
        compiler_params=_params("parallel", "arbitrary"),
        name="proj_gate",
    )(h, w_gate, b_gate.reshape(1, n))


def _attn_a_kernel(q_ref, kc_ref, kp_ref, vc_ref, vp_ref, o_ref, lse_ref, kk_ref, vv_ref,
                   *, sub_len):
    w = WINDOW_STEPS
    t = q_ref.shape[1]
    c = pl.program_id(0)
    kk_ref[0:w] = kp_ref[0]
    kk_ref[w:] = kc_ref[0]
    vv_ref[0:w] = vp_ref[0]
    vv_ref[w:] = vc_ref[0]
    qp = lax.broadcasted_iota(jnp.int32, (w, 2 * w), 0)
    kp = lax.broadcasted_iota(jnp.int32, (w, 2 * w), 1)
    band = (kp >= qp) & (kp <= qp + w)
    band_start = band & (kp >= w)
    if t >= sub_len:
        masks = [band_start if (b * w) % sub_len == 0 else band for b in range(t // w)]
    else:
        is_start = (c % (sub_len // t)) == 0
        masks = [band & (kp >= jnp.where(is_start, w, 0))] + [band] * (t // w - 1)
    for b in range(t // w):
        q = q_ref[0, b * w:(b + 1) * w]
        k = kk_ref[b * w:(b + 2) * w]
        v = vv_ref[b * w:(b + 2) * w]
        s = lax.dot_general(q, k, (((1,), (1,)), ((), ())), preferred_element_type=F32)
        s = jnp.where(masks[b], s, -jnp.inf)
        m = jnp.max(s, axis=-1, keepdims=True)
        p = jnp.exp(s - m)
        l = jnp.sum(p, axis=-1, keepdims=True)
        o = jnp.dot(p.astype(v.dtype), v, preferred_element_type=F32)
        o_ref[b * w:(b + 1) * w] = o / l
        lse_ref[b * w:(b + 1) * w] = jnp.broadcast_to(m + jnp.log(l), (w, HEAD_DIM))


def _attn_a(qkv, t=4096):
    _, heads, dilation, sub_len, hd = qkv.shape
    rows = heads * dilation * sub_len
    flat = qkv.reshape(3, rows, hd)
    w = WINDOW_STEPS
    assert t % sub_len == 0 or sub_len % t == 0
    bpc = t // w
    cur = lambda sec: pl.BlockSpec((1, t, hd), lambda c: (sec, c, 0))
    prev = lambda sec: pl.BlockSpec((1, w, hd), lambda c: (sec, jnp.maximum(c * bpc - 1, 0), 0))
    out_spec = pl.BlockSpec((t, hd), lambda c: (c, 0))
    out, lse = pl.pallas_call(
        functools.partial(_attn_a_kernel, sub_len=sub_len),
        grid=(rows // t,),
        in_specs=[cur(0), cur(1), prev(1), cur(2), prev(2)],
        out_specs=[out_spec, out_spec],
        out_shape=[jax.ShapeDtypeStruct((rows, hd), F32)] * 2,
        scratch_shapes=[pltpu.VMEM((t + w, hd), BF16)] * 2,
        compiler_params=_params("parallel"),
        name=f"attn_a_r{dilation}",
    )(flat, flat, flat, flat, flat)
    shape = (heads, dilation, sub_len, hd)
    return out.reshape(shape), lse.reshape(shape)


def _mix_a_kernel(o0_ref, l0_ref, o1_ref, l1_ref, o2_ref, l2_ref, y_ref, nat_ref):
    tm = y_ref.shape[0]
    refs = ((o0_ref, l0_ref), (o1_ref, l1_ref), (o2_ref, l2_ref))
    r_max = DIL_PAIRS[-1][1]
    n = tm // r_max
    for hh in range(HEADS_PER_GROUP):
        for rho in range(r_max):
            outs, lses = [], []
            for (o_ref, l_ref), (_, r) in zip(refs, DIL_PAIRS):
                step = r_max // r
                rows = pl.ds(rho // r, n, stride=step) if step > 1 else pl.ds(0, n)
                outs.append(o_ref[hh, rho % r, rows, :])
                lses.append(l_ref[hh, rho % r, rows, :])
            m = jnp.maximum(jnp.maximum(lses[0], lses[1]), lses[2])
            ws = [jnp.exp(l - m) for l in lses]
            num = ws[0] * outs[0] + ws[1] * outs[1] + ws[2] * outs[2]
            den = ws[0] + ws[1] + ws[2]
            nat_ref[hh, pl.ds(rho, n, stride=r_max), :] = num / den
        y_ref[:, hh * HEAD_DIM:(hh + 1) * HEAD_DIM] = nat_ref[hh].astype(y_ref.dtype)


def _mix_a(group_outs, seq, tm=512):
    in_specs, args = [], []
    for (o, l), (_, r) in zip(group_outs, DIL_PAIRS):
        spec = pl.BlockSpec((HEADS_PER_GROUP, r, tm // r, HEAD_DIM), lambda i: (0, 0, i, 0))
        in_specs += [spec, spec]
        args += [o, l]
    return pl.pallas_call(
        _mix_a_kernel,
        grid=(seq // tm,),
        in_specs=in_specs,
        out_specs=pl.BlockSpec((tm, A_OUT), lambda i: (i, 0)),
        out_shape=jax.ShapeDtypeStruct((seq, A_OUT), BF16),
        scratch_shapes=[pltpu.VMEM((HEADS_PER_GROUP, tm, HEAD_DIM), F32)],
        compiler_params=_params("parallel"),
        name="mix_a",
    )(*args)


def _attn_b_kernel(lq1_ref, lk1_ref, lq2_ref, lk2_ref, g_ref, q_ref, qn_ref, k_ref, vt_ref,
                   *rest, lam_init, cast_cols):
    n_in = len(cast_cols)
    n_out = sum(len(cols) for cols in cast_cols)
    w_refs = rest[:n_in]
    o_ref = rest[n_in]
    wo_refs = rest[n_in + 1:n_in + 1 + n_out]
    m_ref, acc_ref, st_ref, mb_ref = rest[n_in + 1 + n_out:]
    tq = q_ref.shape[1]
    tk = vt_ref.shape[3]
    width = vt_ref.shape[2] - ONES_ROWS
    qi = pl.program_id(1)
    m_ref[...] = jnp.full(m_ref.shape, -jnp.inf, F32)
    acc_ref[...] = jnp.zeros(acc_ref.shape, F32)

    def scores_to(slot, kv, q_src=q_ref, diagonal=False):
        start = pl.multiple_of(kv * tk, tk)
        for i in range(2):
            st = lax.dot_general(
                k_ref[i, pl.ds(start, tk), :], q_src[i], (((1,), (1,)), ((), ())),
                preferred_element_type=F32)
            if diagonal:
                key = lax.broadcasted_iota(jnp.int32, st.shape, 0)
                query = lax.broadcasted_iota(jnp.int32, st.shape, 1)
                st = jnp.where(key <= query, st, -jnp.inf)
            st_ref[slot, i] = st
            mb_ref[slot, i] = jnp.max(st, axis=0, keepdims=True)

    def softmax_pv(slot, kv):
        vt = vt_ref[0, kv]
        for i in range(2):
            st = st_ref[slot, i]
            m_prev = m_ref[i]
            m_new = jnp.maximum(m_prev, mb_ref[slot, i])
            alpha = jnp.exp2(m_prev - m_new)
            p = jnp.exp2((st - m_new).astype(vt.dtype))
            acc_ref[i] = alpha * acc_ref[i] + jnp.dot(vt, p, preferred_element_type=F32)
            m_ref[i] = m_new

    @pl.when(qi == 0)
    def _():
        scores_to(0, 0, diagonal=True)

    def pair(j):
        scores_to(1, 2 * j + 1)
        softmax_pv(0, 2 * j)
        scores_to(0, 2 * j + 2)
        softmax_pv(1, 2 * j + 1)

    def two_pairs(j, carry):
        pair(2 * j)
        pair(2 * j + 1)
        return carry

    n_pairs = jnp.maximum(qi - 1, 0) // 2
    lax.fori_loop(0, n_pairs // 2, two_pairs, 0)

    @pl.when(n_pairs % 2 == 1)
    def _():
        pair(n_pairs - 1)

    @pl.when(qi == 0)
    def _():
        softmax_pv(0, 0)
        scores_to(0, 0, qn_ref)

    @pl.when(qi % 2 == 1)
    def _():
        scores_to(1, qi, diagonal=True)
        softmax_pv(0, qi - 1)
        scores_to(0, 0, qn_ref)
        softmax_pv(1, qi)

    @pl.when(jnp.logical_and(qi % 2 == 0, qi > 0))
    def _():
        scores_to(1, qi - 1)
        softmax_pv(0, qi - 2)
        scores_to(0, qi, diagonal=True)
        softmax_pv(1, qi - 1)
        softmax_pv(0, qi)
        scores_to(0, 0, qn_ref)

    lam = (jnp.exp(jnp.sum(lq1_ref[...] * lk1_ref[...], axis=-1, keepdims=True))
           - jnp.exp(jnp.sum(lq2_ref[...] * lk2_ref[...], axis=-1, keepdims=True))
           + lam_init)
    inv_l = [1.0 / acc_ref[i, width:width + 1, :] for i in range(2)]
    ot = acc_ref[0, :width, :] * inv_l[0] - lam * (acc_ref[1, :width, :] * inv_l[1])
    ms = jnp.mean(ot * ot, axis=0, keepdims=True)
    yt = (ot * lax.rsqrt(ms + EPS)) * g_ref[...]
    o_ref[...] = (yt * (1.0 - lam_init)).T.astype(o_ref.dtype)

    wo_iter = iter(wo_refs)
    for w_ref, cols in zip(w_refs, cast_cols):
        for start, size in cols:
            wo_ref = next(wo_iter)
            wo_ref[...] = w_ref[:, start:start + size].astype(wo_ref.dtype)


def _attn_b(qk, vt, lam_vecs, subln_g, lam_init, cast_weights):
    _, s, hd = qk.shape
    _, n_kv, rows, tk = vt.shape
    width = rows - ONES_ROWS
    tq = tk
    n_q = s // tq
    n_steps = B_HEADS * n_q
    vec = pl.BlockSpec((1, hd), lambda h, i: (0, 0))
    chunk = lambda h, i: (h * n_q + i, 0)
    w_specs, wo_specs, wo_shapes = [], [], []
    for w, cols in cast_weights:
        assert w.shape[0] % (n_steps * BF16_SUBLANES) == 0, w.shape
        rows_per_step = w.shape[0] // n_steps
        w_specs.append(pl.BlockSpec((rows_per_step, w.shape[1]), chunk))
        for _, size in cols:
            wo_specs.append(pl.BlockSpec((rows_per_step, size), chunk))
            wo_shapes.append(jax.ShapeDtypeStruct((w.shape[0], size), BF16))
    return pl.pallas_call(
        functools.partial(_attn_b_kernel, lam_init=lam_init,
                          cast_cols=tuple(tuple(cols) for _, cols in cast_weights)),
        grid=(B_HEADS, n_q),
        in_specs=[vec, vec, vec, vec,
                  pl.BlockSpec((width, 1), lambda h, i: (0, 0)),
                  pl.BlockSpec((2, tq, hd), lambda h, i: (h, i, 0)),
                  pl.BlockSpec((2, tq, hd), lambda h, i: (h, jnp.minimum(i + 1, n_q - 1), 0)),
                  pl.BlockSpec((2, s, hd), lambda h, i: (B_HEADS + h, 0, 0)),
                  pl.BlockSpec((1, n_kv, rows, tk), lambda h, i: (h, 0, 0, 0))] + w_specs,
        out_specs=[pl.BlockSpec((tq, width), lambda h, i: (i, h))] + wo_specs,
        out_shape=[jax.ShapeDtypeStruct((s, B_HEADS * width), BF16)] + wo_shapes,
        scratch_shapes=[pltpu.VMEM((2, 1, tq), F32),
                        pltpu.VMEM((2, rows, tq), F32),
                        pltpu.VMEM((2, 2, tk, tq), F32),
                        pltpu.VMEM((2, 2, 1, tq), F32)],
        compiler_params=_params("arbitrary", "arbitrary"),
        name="attn_b",
    )(*[x.reshape(1, hd) for x in lam_vecs], subln_g.reshape(width, 1), qk, qk, qk, vt,
      *[w for w, _ in cast_weights])


def _merge_kernel(ya_ref, yb_ref, wa_ref, wb_ref, ga_ref, gb_ref, o_ref):
    pa = jnp.dot(ya_ref[...], wa_ref[...].astype(BF16), preferred_element_type=F32)
    pb = jnp.dot(yb_ref[...], wb_ref[...], preferred_element_type=F32)
    o_ref[...] = (ga_ref[...].astype(F32) * pa + gb_ref[...].astype(F32) * pb).astype(o_ref.dtype)


def _merge(ya, yb, wa, wb, gates, tm=1024, tn=1024):
    s = ya.shape[0]
    n = wa.shape[1]
    return pl.pallas_call(
        _merge_kernel,
        grid=(s // tm, n // tn),
        in_specs=[pl.BlockSpec((tm, ya.shape[1]), lambda i, j: (i, 0)),
                  pl.BlockSpec((tm, yb.shape[1]), lambda i, j: (i, 0)),
                  pl.BlockSpec((wa.shape[0], tn), lambda i, j: (0, j)),
                  pl.BlockSpec((wb.shape[0], tn), lambda i, j: (0, j)),
                  pl.BlockSpec((tm, tn), lambda i, j: (i, j)),
                  pl.BlockSpec((tm, tn), lambda i, j: (i, n // tn + j))],
        out_specs=pl.BlockSpec((tm, tn), lambda i, j: (i, j)),
        out_shape=jax.ShapeDtypeStruct((s, n), BF16),
        compiler_params=_params("parallel", "arbitrary"),
        name="merge",
    )(ya, yb, wa, wb, gates, gates)


def _out_proj_kernel(a_ref, b_ref, x_ref, g_ref, x1_ref, xg_ref, r2_ref, *, n_cols):
    j = pl.program_id(1)
    x1 = x_ref[...] + jnp.dot(a_ref[...], b_ref[...], preferred_element_type=F32)
    x1_ref[...] = x1
    xg_ref[...] = (x1 * g_ref[...]).astype(xg_ref.dtype)
    part = jnp.broadcast_to(jnp.sum(x1 * x1, axis=-1, keepdims=True), r2_ref.shape)

    @pl.when(j == 0)
    def _():
        r2_ref[...] = part

    @pl.when(j > 0)
    def _():
        r2_ref[...] += part

    @pl.when(j == n_cols // x_ref.shape[1] - 1)
    def _():
        r2_ref[...] = 1.0 / (r2_ref[...] * (1.0 / n_cols) + EPS)


def _out_proj(a, b, x, g, tm=1024, tn=512):
    s, kdim = a.shape
    n = b.shape[1]
    lanes = HEAD_DIM
    return pl.pallas_call(
        functools.partial(_out_proj_kernel, n_cols=n),
        grid=(s // tm, n // tn),
        in_specs=[pl.BlockSpec((tm, kdim), lambda i, j: (i, 0)),
                  pl.BlockSpec((kdim, tn), lambda i, j: (0, j)),
                  pl.BlockSpec((tm, tn), lambda i, j: (i, j)),
                  pl.BlockSpec((1, tn), lambda i, j: (0, j))],
        out_specs=[pl.BlockSpec((tm, tn), lambda i, j: (i, j)),
                   pl.BlockSpec((tm, tn), lambda i, j: (i, j)),
                   pl.BlockSpec((tm, lanes), lambda i, j: (i, 0))],
        out_shape=[jax.ShapeDtypeStruct((s, n), F32),
                   jax.ShapeDtypeStruct((s, n), BF16),
                   jax.ShapeDtypeStruct((s, lanes), F32)],
        compiler_params=_params("parallel", "arbitrary"),
        name="out_proj",
    )(a, b, x, g.reshape(1, n))


def _ff1_kernel(xg_ref, r2_ref, w_hbm, o_hbm, wbuf, obuf, w_sem, o_sem):
    i = pl.program_id(0)
    tm = xg_ref.shape[0]
    tn = wbuf.shape[2]
    n_tiles = w_hbm.shape[1] // tn
    lanes = r2_ref.shape[1]

    def w_copy(j, slot):
        cols = pl.ds(pl.multiple_of(j * tn, tn), tn)
        return pltpu.make_async_copy(w_hbm.at[:, cols], wbuf.at[slot], w_sem.at[slot])

    def o_copy(j, slot):
        rows = pl.ds(pl.multiple_of(i * tm, tm), tm)
        cols = pl.ds(pl.multiple_of(j * tn, tn), tn)
        return pltpu.make_async_copy(obuf.at[slot], o_hbm.at[rows, cols], o_sem.at[slot])

    def compute(slot):
        u = jnp.maximum(jnp.dot(xg_ref[...], wbuf[slot], preferred_element_type=F32), 0.0)
        for c in range(tn // lanes):
            uc = u[:, c * lanes:(c + 1) * lanes]
            obuf[slot, :, c * lanes:(c + 1) * lanes] = (uc * uc * r2_ref[...]).astype(obuf.dtype)

    @pl.when(i == 0)
    def _():
        w_copy(0, 0).start()

    def two_tiles(jj, carry):
        a = 2 * jj
        b = a + 1

        @pl.when(jj > 0)
        def _():
            o_copy(a - 2, 0).wait()
            o_copy(b - 2, 1).wait()

        w_copy(a, 0).wait()
        w_copy(b, 1).start()
        compute(0)
        o_copy(a, 0).start()
        w_copy((a + 2) % n_tiles, 0).start()
        w_copy(b, 1).wait()
        compute(1)
        o_copy(b, 1).start()
        return carry

    lax.fori_loop(0, n_tiles // 2, two_tiles, 0)
    o_copy(n_tiles - 2, 0).wait()
    o_copy(n_tiles - 1, 1).wait()

    @pl.when(i == pl.num_programs(0) - 1)
    def _():
        w_copy(0, 0).wait()


def _ff1(xg, r2, b, tm=1024, tn=1024):
    s, kdim = xg.shape
    n = b.shape[1]
    assert (n // tn) % 2 == 0
    return pl.pallas_call(
        _ff1_kernel,
        grid=(s // tm,),
        in_specs=[pl.BlockSpec((tm, kdim), lambda i: (i, 0)),
                  pl.BlockSpec((tm, r2.shape[1]), lambda i: (i, 0)),
                  pl.BlockSpec(memory_space=pl.ANY)],
        out_specs=pl.BlockSpec(memory_space=pl.ANY),
        out_shape=jax.ShapeDtypeStruct((s, n), BF16),
        scratch_shapes=[pltpu.VMEM((2, kdim, tn), b.dtype),
                        pltpu.VMEM((2, tm, tn), BF16),
                        pltpu.SemaphoreType.DMA((2,)),
                        pltpu.SemaphoreType.DMA((2,))],
        compiler_params=_params("arbitrary"),
        name="ff1",
    )(xg, r2, b)


def _ff2_kernel(a_ref, b_ref, x_ref, o_ref):
    @pl.when(pl.program_id(2) == 0)
    def _():
        o_ref[...] = x_ref[...]

    o_ref[...] += jnp.dot(a_ref[...], b_ref[...], preferred_element_type=F32)


def _ff2(a, b, x, tm=1024, tn=1024, tk=4096):
    s, kdim = a.shape
    n = b.shape[1]
    return pl.pallas_call(
        _ff2_kernel,
        grid=(s // tm, n // tn, kdim // tk),
        in_specs=[pl.BlockSpec((tm, tk), lambda i, j, k: (i, k)),
                  pl.BlockSpec((tk, tn), lambda i, j, k: (k, j)),
                  pl.BlockSpec((tm, tn), lambda i, j, k: (i, j))],
        out_specs=pl.BlockSpec((tm, tn), lambda i, j, k: (i, j)),
        out_shape=jax.ShapeDtypeStruct((s, n), F32),
        compiler_params=_params("parallel", "parallel", "arbitrary"),
        name="ff2",
    )(a, b, x)


def _lambda_init(layer_idx):
    return 0.8 - 0.6 * math.exp(-0.3 * layer_idx)


def kernel(x, norm_mix, w_in, b_gate, w_proj_a, w_proj_b, w_out, lambda_q1, lambda_k1,
           lambda_q2, lambda_k2, subln_g, norm_mlp, w_ff1, w_ff2, norm_final):
    batch, seq, d = x.shape
    depth = w_in.shape[0]
    tables = _rope_tables(seq)
    outs = []
    for b in range(batch):
        xb = x.reshape(seq, d) if batch == 1 else x[b]
        for l in range(depth):
            w_in_l = w_in[l]
            h = _rmsnorm(xb, norm_mix[l], BF16)
            qkb = _proj_bqk(h, w_in_l, tables)
            vb = _proj_bv(h, w_in_l)
            lam_init = _lambda_init(l)
            whole = lambda w: (w, [(0, w.shape[1])])
            yb, wa_in_bf, wg_bf, wb_bf, wo_bf, w1_bf, w2_bf = _attn_b(
                qkb, vb, (lambda_q1[l], lambda_k1[l], lambda_q2[l], lambda_k2[l]),
                subln_g[l], lam_init,
                [(w_in_l, [(COL_QA, 3 * A_WIDTH), (COL_GATE, 2 * D_MODEL)]),
                 whole(w_proj_b[l]), whole(w_out[l]), whole(w_ff1[l]), whole(w_ff2[l])])
            groups = []
            for g in range(N_GROUPS):
                groups.append(_attn_a(_proj_a(h, wa_in_bf, tables, g)))
            ya = _mix_a(groups, seq)
            gates = _proj_gate(h, wg_bf, b_gate[l])
            merged = _merge(ya, yb, w_proj_a[l], wb_bf, gates)
            xb, xg, r2 = _out_proj(merged, wo_bf, xb, norm_mlp[l])
            u = _ff1(xg, r2, w1_bf)
            xb = _ff2(u, w2_bf, xb)
        outs.append(_rmsnorm(xb, norm_final, x.dtype))
    if batch == 1:
        return outs[0].reshape(1, seq, d)
    return jnp.stack(outs, axis=0)
```

```python
import functools
import math

import jax
import jax.numpy as jnp
from jax import lax
from jax.experimental import pallas as pl
from jax.experimental.pallas import tpu as pltpu

D_MODEL = 4096
HEAD_DIM = 128
DIL_PAIRS = ((128, 1), (512, 4), (2048, 16))
N_GROUPS = len(DIL_PAIRS)
HEADS_PER_GROUP = 4
GROUP_COLS = HEADS_PER_GROUP * HEAD_DIM
A_WIDTH = N_GROUPS * GROUP_COLS
A_OUT = GROUP_COLS
B_HEADS = 8
B_QK = B_HEADS * 2 * HEAD_DIM
B_V = B_HEADS * 2 * HEAD_DIM
ROPE_THETA = 500000.0
ROT_DIM = HEAD_DIM // 4
ROT_HALF = ROT_DIM // 2
EPS = 1e-6
QK_SCALE = 1.0 / math.sqrt(HEAD_DIM)
LOG2_E = math.log2(math.e)
WINDOW_STEPS = 128
BF16_SUBLANES = 16
ONES_ROWS = BF16_SUBLANES

COL_QA = 0
COL_QB = 3 * A_WIDTH
COL_VB = COL_QB + 2 * B_QK
COL_GATE = COL_VB + B_V

VMEM_LIMIT_BYTES = 56 * 1024 * 1024

F32 = jnp.float32
BF16 = jnp.bfloat16


def _params(*semantics):
    return pltpu.CompilerParams(dimension_semantics=semantics,
                                vmem_limit_bytes=VMEM_LIMIT_BYTES)


def _rmsnorm_kernel(x_ref, g_ref, o_ref):
    x = x_ref[...]
    ms = jnp.mean(x * x, axis=-1, keepdims=True)
    o_ref[...] = ((x * lax.rsqrt(ms + EPS)) * g_ref[...]).astype(o_ref.dtype)


def _rmsnorm(x, g, out_dtype, rows=512):
    s, d = x.shape
    return pl.pallas_call(
        _rmsnorm_kernel,
        grid=(s // rows,),
        in_specs=[pl.BlockSpec((rows, d), lambda i: (i, 0)),
                  pl.BlockSpec((1, d), lambda i: (0, 0))],
        out_specs=pl.BlockSpec((rows, d), lambda i: (i, 0)),
        out_shape=jax.ShapeDtypeStruct((s, d), out_dtype),
        compiler_params=_params("parallel"),
        name="rmsnorm",
    )(x, g.reshape(1, d))


def _rope_tables(seq):
    pos = jnp.arange(seq, dtype=F32)
    inv = ROPE_THETA ** (-jnp.arange(0, ROT_DIM, 2, dtype=F32) / ROT_DIM)
    ang = pos[:, None] * inv[None, :]
    cos, sin = jnp.cos(ang), jnp.sin(ang)
    ones = jnp.ones((seq, HEAD_DIM - ROT_DIM), F32)
    zeros = jnp.zeros((seq, HEAD_DIM - ROT_DIM), F32)
    zh = jnp.zeros((seq, ROT_HALF), F32)
    cos_t = jnp.concatenate([cos, cos, ones], axis=1)
    sin_lo = jnp.concatenate([-sin, zh, zeros], axis=1)
    sin_hi = jnp.concatenate([zh, sin, zeros], axis=1)
    rot = jnp.stack([cos_t, sin_lo, sin_hi])
    ident = jnp.stack([jnp.ones_like(cos_t), jnp.zeros_like(cos_t), jnp.zeros_like(cos_t)])
    return jnp.stack([rot, ident])


def _rope_head(y, tab_ref, scale):
    x2 = pltpu.roll(y, HEAD_DIM - ROT_HALF, 1)
    x1 = pltpu.roll(y, ROT_HALF, 1)
    return (y * tab_ref[0, 0] + x2 * tab_ref[0, 1] + x1 * tab_ref[0, 2]) * scale


def _store_product(a_ref, w_ref, acc_ref, *_):
    acc_ref[...] = jnp.dot(a_ref[...], w_ref[...].astype(BF16), preferred_element_type=F32)


def _lagged_kernel(a_ref, w_ref, *refs, n_ep, n_out, n_tiles, n_j, epilogue):
    ep_refs = refs[:n_ep]
    out_refs = refs[n_ep:n_ep + n_out]
    scratch = refs[n_ep + n_out:]
    t = pl.program_id(0)
    jp = jnp.maximum(t - 1, 0) % n_j

    @pl.when(t == 0)
    def _():
        _store_product(a_ref, w_ref, *scratch)

    @pl.when(jnp.logical_and(t > 0, t < n_tiles))
    def _():
        epilogue(jp, *ep_refs, *out_refs, *scratch)
        _store_product(a_ref, w_ref, *scratch)

    @pl.when(t == n_tiles)
    def _():
        epilogue(jp, *ep_refs, *out_refs, *scratch)


def _lagged_matmul(name, epilogue, *, n_i, n_j, mm_args, mm_specs, ep_args, ep_specs,
                   out_shapes, out_specs, scratch_shapes):
    n_tiles = n_i * n_j

    def cur(f):
        return lambda t: f(*divmod(jnp.minimum(t, n_tiles - 1), n_j))

    def prev(f):
        return lambda t: f(*divmod(jnp.maximum(t - 1, 0), n_j))

    in_specs = ([pl.BlockSpec(shape, cur(f)) for shape, f in mm_specs]
                + [pl.BlockSpec(shape, prev(f)) for shape, f in ep_specs])
    return pl.pallas_call(
        functools.partial(_lagged_kernel, n_ep=len(ep_args), n_out=len(out_shapes),
                          n_tiles=n_tiles, n_j=n_j, epilogue=epilogue),
        grid=(n_tiles + 1,),
        in_specs=in_specs,
        out_specs=[pl.BlockSpec(shape, prev(f)) for shape, f in out_specs],
        out_shape=out_shapes,
        scratch_shapes=scratch_shapes,
        compiler_params=_params("arbitrary"),
        name=name,
    )(*mm_args, *ep_args)


def _proj_a_epilogue(jp, tab_ref, o_ref, acc_ref, y_ref, *, dilation):
    tm = acc_ref.shape[0]
    n = tm // dilation
    scale = jnp.where(jp == 0, QK_SCALE, 1.0).astype(F32)
    for hh in range(HEADS_PER_GROUP):
        y = _rope_head(acc_ref[:, hh * HEAD_DIM:(hh + 1) * HEAD_DIM], tab_ref, scale)
        if dilation == 1:
            o_ref[0, hh, 0] = y.astype(o_ref.dtype)
            continue
        y_ref[hh] = y
        for rho in range(dilation):
            o_ref[0, hh, rho] = y_ref[hh, pl.ds(rho, n, stride=dilation), :].astype(o_ref.dtype)


def _proj_a(h, w_in, tables, group, tm=1024):
    s, d = h.shape
    dilation = DIL_PAIRS[group][1]
    sub_len = s // dilation
    blocks_per_section = A_WIDTH // GROUP_COLS
    return _lagged_matmul(
        f"proj_a_g{group}", functools.partial(_proj_a_epilogue, dilation=dilation),
        n_i=s // tm, n_j=3,
        mm_args=[h, w_in],
        mm_specs=[((tm, d), lambda i, j: (i, 0)),
                  ((d, GROUP_COLS), lambda i, j: (0, j * blocks_per_section + group))],
        ep_args=[tables],
        ep_specs=[((1, 3, tm, HEAD_DIM), lambda i, j: (j // 2, 0, i, 0))],
        out_shapes=[jax.ShapeDtypeStruct((3, HEADS_PER_GROUP, dilation, sub_len, HEAD_DIM), BF16)],
        out_specs=[((1, HEADS_PER_GROUP, dilation, tm // dilation, HEAD_DIM),
                    lambda i, j: (j, 0, 0, i, 0))],
        scratch_shapes=[pltpu.VMEM((tm, GROUP_COLS), F32),
                        pltpu.VMEM((HEADS_PER_GROUP, tm, HEAD_DIM), F32)])[0]


def _proj_bqk_epilogue(jp, tab_ref, o_ref, acc_ref, *, q_tiles):
    scale = jnp.where(jp < q_tiles, QK_SCALE * LOG2_E, 1.0).astype(F32)
    for hh in range(o_ref.shape[0]):
        y = _rope_head(acc_ref[:, hh * HEAD_DIM:(hh + 1) * HEAD_DIM], tab_ref, scale)
        o_ref[hh] = y.astype(o_ref.dtype)


def _proj_bqk(h, w_in, tables, tm=1024, tn=512):
    s, d = h.shape
    heads_per_tile = tn // HEAD_DIM
    n = B_QK + B_QK
    return _lagged_matmul(
        "proj_b_qk", functools.partial(_proj_bqk_epilogue, q_tiles=B_QK // tn),
        n_i=s // tm, n_j=n // tn,
        mm_args=[h, w_in],
        mm_specs=[((tm, d), lambda i, j: (i, 0)),
                  ((d, tn), lambda i, j: (0, COL_QB // tn + j))],
        ep_args=[tables],
        ep_specs=[((1, 3, tm, HEAD_DIM), lambda i, j: (0, 0, i, 0))],
        out_shapes=[jax.ShapeDtypeStruct((n // HEAD_DIM, s, HEAD_DIM), BF16)],
        out_specs=[((heads_per_tile, tm, HEAD_DIM), lambda i, j: (j, i, 0))],
        scratch_shapes=[pltpu.VMEM((tm, tn), F32)])[0]


def _proj_bv_kernel(h_ref, w_ref, o_ref):
    acc = jnp.dot(h_ref[...], w_ref[...].astype(BF16), preferred_element_type=F32)
    heads, blocks, rows, tk = o_ref.shape
    width = rows - ONES_ROWS
    for hv in range(heads):
        vt = acc[:, hv * width:(hv + 1) * width].T
        for c in range(blocks):
            o_ref[hv, c, :width, :] = vt[:, c * tk:(c + 1) * tk].astype(o_ref.dtype)
            o_ref[hv, c, width:, :] = jnp.ones((ONES_ROWS, tk), o_ref.dtype)


def _proj_bv(h, w_in, tm=1024, tn=512, tk=512):
    s, d = h.shape
    width = 2 * HEAD_DIM
    rows = width + ONES_ROWS
    return pl.pallas_call(
        _proj_bv_kernel,
        grid=(s // tm, B_V // tn),
        in_specs=[pl.BlockSpec((tm, d), lambda i, j: (i, 0)),
                  pl.BlockSpec((d, tn), lambda i, j: (0, COL_VB // tn + j))],
        out_specs=pl.BlockSpec((tn // width, tm // tk, rows, tk), lambda i, j: (j, i, 0, 0)),
        out_shape=jax.ShapeDtypeStruct((B_HEADS, s // tk, rows, tk), BF16),
        compiler_params=_params("parallel", "arbitrary"),
        name="proj_b_v",
    )(h, w_in)


def _proj_gate_kernel(h_ref, w_ref, b_ref, o_ref):
    z = jnp.dot(h_ref[...], w_ref[...], preferred_element_type=F32) + b_ref[...]
    o_ref[...] = (0.5 * jnp.tanh(0.5 * z) + 0.5).astype(o_ref.dtype)


def _proj_gate(h, w_gate, b_gate, tm=1024, tn=1024):
    s, d = h.shape
    n = w_gate.shape[1]
    return pl.pallas_call(
        _proj_gate_kernel,
        grid=(s // tm, n // tn),
        in_specs=[pl.BlockSpec((tm, d), lambda i, j: (i, 0)),
                  pl.BlockSpec((d, tn), lambda i, j: (0, j)),
                  pl.BlockSpec((1, tn), lambda i, j: (0, j))],
        out_specs=pl.BlockSpec((tm, tn), lambda i, j: (i, j)),
        out_shape=jax.ShapeDtypeStruct((s, n), BF16),
        compiler_params=_params("parallel", "arbitrary"),
        name="proj_gate",
    )(h, w_gate, b_gate.reshape(1, n))


def _attn_a_kernel(q_ref, kc_ref, kp_ref, vc_ref, vp_ref, o_ref, lse_ref, kk_ref, vv_ref,
                   *, sub_len):
    w = WINDOW_STEPS
    t = q_ref.shape[1]
    c = pl.program_id(0)
    kk_ref[0:w] = kp_ref[0]
    kk_ref[w:] = kc_ref[0]
    vv_ref[0:w] = vp_ref[0]
    vv_ref[w:] = vc_ref[0]
    qp = lax.broadcasted_iota(jnp.int32, (w, 2 * w), 0)
    kp = lax.broadcasted_iota(jnp.int32, (w, 2 * w), 1)
    band = (kp >= qp) & (kp <= qp + w)
    band_start = band & (kp >= w)
    if t >= sub_len:
        masks = [band_start if (b * w) % sub_len == 0 else band for b in range(t // w)]
    else:
        is_start = (c % (sub_len // t)) == 0
        masks = [band & (kp >= jnp.where(is_start, w, 0))] + [band] * (t // w - 1)
    for b in range(t // w):
        q = q_ref[0, b * w:(b + 1) * w]
        k = kk_ref[b * w:(b + 2) * w]
        v = vv_ref[b * w:(b + 2) * w]
        s = lax.dot_general(q, k, (((1,), (1,)), ((), ())), preferred_element_type=F32)
        s = jnp.where(masks[b], s, -jnp.inf)
        m = jnp.max(s, axis=-1, keepdims=True)
        p = jnp.exp(s - m)
        l = jnp.sum(p, axis=-1, keepdims=True)
        o = jnp.dot(p.astype(v.dtype), v, preferred_element_type=F32)
        o_ref[b * w:(b + 1) * w] = o / l
        lse_ref[b * w:(b + 1) * w] = jnp.broadcast_to(m + jnp.log(l), (w, HEAD_DIM))


def _attn_a(qkv, t=4096):
    _, heads, dilation, sub_len, hd = qkv.shape
    rows = heads * dilation * sub_len
    flat = qkv.reshape(3, rows, hd)
    w = WINDOW_STEPS
    assert t % sub_len == 0 or sub_len % t == 0
    bpc = t // w
    cur = lambda sec: pl.BlockSpec((1, t, hd), lambda c: (sec, c, 0))
    prev = lambda sec: pl.BlockSpec((1, w, hd), lambda c: (sec, jnp.maximum(c * bpc - 1, 0), 0))
    out_spec = pl.BlockSpec((t, hd), lambda c: (c, 0))
    out, lse = pl.pallas_call(
        functools.partial(_attn_a_kernel, sub_len=sub_len),
        grid=(rows // t,),
        in_specs=[cur(0), cur(1), prev(1), cur(2), prev(2)],
        out_specs=[out_spec, out_spec],
        out_shape=[jax.ShapeDtypeStruct((rows, hd), F32)] * 2,
        scratch_shapes=[pltpu.VMEM((t + w, hd), BF16)] * 2,
        compiler_params=_params("parallel"),
        name=f"attn_a_r{dilation}",
    )(flat, flat, flat, flat, flat)
    shape = (heads, dilation, sub_len, hd)
    return out.reshape(shape), lse.reshape(shape)


def _mix_a_kernel(o0_ref, l0_ref, o1_ref, l1_ref, o2_ref, l2_ref, y_ref, nat_ref):
    tm = y_ref.shape[0]
    refs = ((o0_ref, l0_ref), (o1_ref, l1_ref), (o2_ref, l2_ref))
    r_max = DIL_PAIRS[-1][1]
    n = tm // r_max
    for hh in range(HEADS_PER_GROUP):
        for rho in range(r_max):
            outs, lses = [], []
            for (o_ref, l_ref), (_, r) in zip(refs, DIL_PAIRS):
                step = r_max // r
                rows = pl.ds(rho // r, n, stride=step) if step > 1 else pl.ds(0, n)
                outs.append(o_ref[hh, rho % r, rows, :])
                lses.append(l_ref[hh, rho % r, rows, :])
            m = jnp.maximum(jnp.maximum(lses[0], lses[1]), lses[2])
            ws = [jnp.exp(l - m) for l in lses]
            num = ws[0] * outs[0] + ws[1] * outs[1] + ws[2] * outs[2]
            den = ws[0] + ws[1] + ws[2]
            nat_ref[hh, pl.ds(rho, n, stride=r_max), :] = num / den
        y_ref[:, hh * HEAD_DIM:(hh + 1) * HEAD_DIM] = nat_ref[hh].astype(y_ref.dtype)


def _mix_a(group_outs, seq, tm=1024):
    in_specs, args = [], []
    for (o, l), (_, r) in zip(group_outs, DIL_PAIRS):
        spec = pl.BlockSpec((HEADS_PER_GROUP, r, tm // r, HEAD_DIM), lambda i: (0, 0, i, 0))
        in_specs += [spec, spec]
        args += [o, l]
    return pl.pallas_call(
        _mix_a_kernel,
        grid=(seq // tm,),
        in_specs=in_specs,
        out_specs=pl.BlockSpec((tm, A_OUT), lambda i: (i, 0)),
        out_shape=jax.ShapeDtypeStruct((seq, A_OUT), BF16),
        scratch_shapes=[pltpu.VMEM((HEADS_PER_GROUP, tm, HEAD_DIM), F32)],
        compiler_params=_params("parallel"),
        name="mix_a",
    )(*args)


def _attn_b_kernel(lq1_ref, lk1_ref, lq2_ref, lk2_ref, g_ref, q_ref, qn_ref, k_ref, vt_ref,
                   *rest, lam_init, cast_cols):
    n_in = len(cast_cols)
    n_out = sum(len(cols) for cols in cast_cols)
    w_refs = rest[:n_in]
    o_ref = rest[n_in]
    wo_refs = rest[n_in + 1:n_in + 1 + n_out]
    m_ref, acc_ref, st_ref, mb_ref = rest[n_in + 1 + n_out:]
    tq = q_ref.shape[1]
    tk = vt_ref.shape[3]
    width = vt_ref.shape[2] - ONES_ROWS
    qi = pl.program_id(1)
    m_ref[...] = jnp.full(m_ref.shape, -jnp.inf, F32)
    acc_ref[...] = jnp.zeros(acc_ref.shape, F32)

    def scores_to(slot, kv, q_src=q_ref, diagonal=False):
        start = pl.multiple_of(kv * tk, tk)
        for i in range(2):
            st = lax.dot_general(
                k_ref[i, pl.ds(start, tk), :], q_src[i], (((1,), (1,)), ((), ())),
                preferred_element_type=F32)
            if diagonal:
                key = lax.broadcasted_iota(jnp.int32, st.shape, 0)
                query = lax.broadcasted_iota(jnp.int32, st.shape, 1)
                st = jnp.where(key <= query, st, -jnp.inf)
            st_ref[slot, i] = st
            mb_ref[slot, i] = jnp.max(st, axis=0, keepdims=True)

    def softmax_pv(slot, kv):
        vt = vt_ref[0, kv]
        for i in range(2):
            st = st_ref[slot, i]
            m_prev = m_ref[i]
            m_new = jnp.maximum(m_prev, mb_ref[slot, i])
            alpha = jnp.exp2(m_prev - m_new)
            p = jnp.exp2((st - m_new).astype(vt.dtype))
            acc_ref[i] = alpha * acc_ref[i] + jnp.dot(vt, p, preferred_element_type=F32)
            m_ref[i] = m_new

    @pl.when(qi == 0)
    def _():
        scores_to(0, 0, diagonal=True)

    def pair(j):
        scores_to(1, 2 * j + 1)
        softmax_pv(0, 2 * j)
        scores_to(0, 2 * j + 2)
        softmax_pv(1, 2 * j + 1)

    def two_pairs(j, carry):
        pair(2 * j)
        pair(2 * j + 1)
        return carry

    n_pairs = jnp.maximum(qi - 1, 0) // 2
    lax.fori_loop(0, n_pairs // 2, two_pairs, 0)

    @pl.when(n_pairs % 2 == 1)
    def _():
        pair(n_pairs - 1)

    @pl.when(qi == 0)
    def _():
        softmax_pv(0, 0)
        scores_to(0, 0, qn_ref)

    @pl.when(qi % 2 == 1)
    def _():
        scores_to(1, qi, diagonal=True)
        softmax_pv(0, qi - 1)
        scores_to(0, 0, qn_ref)
        softmax_pv(1, qi)

    @pl.when(jnp.logical_and(qi % 2 == 0, qi > 0))
    def _():
        scores_to(1, qi - 1)
        softmax_pv(0, qi - 2)
        scores_to(0, qi, diagonal=True)
        softmax_pv(1, qi - 1)
        softmax_pv(0, qi)
        scores_to(0, 0, qn_ref)

    lam = (jnp.exp(jnp.sum(lq1_ref[...] * lk1_ref[...], axis=-1, keepdims=True))
           - jnp.exp(jnp.sum(lq2_ref[...] * lk2_ref[...], axis=-1, keepdims=True))
           + lam_init)
    inv_l = [1.0 / acc_ref[i, width:width + 1, :] for i in range(2)]
    ot = acc_ref[0, :width, :] * inv_l[0] - lam * (acc_ref[1, :width, :] * inv_l[1])
    ms = jnp.mean(ot * ot, axis=0, keepdims=True)
    yt = (ot * lax.rsqrt(ms + EPS)) * g_ref[...]
    o_ref[...] = (yt * (1.0 - lam_init)).T.astype(o_ref.dtype)

    wo_iter = iter(wo_refs)
    for w_ref, cols in zip(w_refs, cast_cols):
        for start, size in cols:
            wo_ref = next(wo_iter)
            wo_ref[...] = w_ref[:, start:start + size].astype(wo_ref.dtype)


def _attn_b(qk, vt, lam_vecs, subln_g, lam_init, cast_weights):
    _, s, hd = qk.shape
    _, n_kv, rows, tk = vt.shape
    width = rows - ONES_ROWS
    tq = tk
    n_q = s // tq
    n_steps = B_HEADS * n_q
    vec = pl.BlockSpec((1, hd), lambda h, i: (0, 0))
    chunk = lambda h, i: (h * n_q + i, 0)
    w_specs, wo_specs, wo_shapes = [], [], []
    for w, cols in cast_weights:
        assert w.shape[0] % (n_steps * BF16_SUBLANES) == 0, w.shape
        rows_per_step = w.shape[0] // n_steps
        w_specs.append(pl.BlockSpec((rows_per_step, w.shape[1]), chunk))
        for _, size in cols:
            wo_specs.append(pl.BlockSpec((rows_per_step, size), chunk))
            wo_shapes.append(jax.ShapeDtypeStruct((w.shape[0], size), BF16))
    return pl.pallas_call(
        functools.partial(_attn_b_kernel, lam_init=lam_init,
                          cast_cols=tuple(tuple(cols) for _, cols in cast_weights)),
        grid=(B_HEADS, n_q),
        in_specs=[vec, vec, vec, vec,
                  pl.BlockSpec((width, 1), lambda h, i: (0, 0)),
                  pl.BlockSpec((2, tq, hd), lambda h, i: (h, i, 0)),
                  pl.BlockSpec((2, tq, hd), lambda h, i: (h, jnp.minimum(i + 1, n_q - 1), 0)),
                  pl.BlockSpec((2, s, hd), lambda h, i: (B_HEADS + h, 0, 0)),
                  pl.BlockSpec((1, n_kv, rows, tk), lambda h, i: (h, 0, 0, 0))] + w_specs,
        out_specs=[pl.BlockSpec((tq, width), lambda h, i: (i, h))] + wo_specs,
        out_shape=[jax.ShapeDtypeStruct((s, B_HEADS * width), BF16)] + wo_shapes,
        scratch_shapes=[pltpu.VMEM((2, 1, tq), F32),
                        pltpu.VMEM((2, rows, tq), F32),
                        pltpu.VMEM((2, 2, tk, tq), F32),
                        pltpu.VMEM((2, 2, 1, tq), F32)],
        compiler_params=_params("arbitrary", "arbitrary"),
        name="attn_b",
    )(*[x.reshape(1, hd) for x in lam_vecs], subln_g.reshape(width, 1), qk, qk, qk, vt,
      *[w for w, _ in cast_weights])


def _merge_kernel(ya_ref, yb_ref, wa_ref, wb_ref, ga_ref, gb_ref, o_ref):
    pa = jnp.dot(ya_ref[...], wa_ref[...].astype(BF16), preferred_element_type=F32)
    pb = jnp.dot(yb_ref[...], wb_ref[...], preferred_element_type=F32)
    o_ref[...] = (ga_ref[...].astype(F32) * pa + gb_ref[...].astype(F32) * pb).astype(o_ref.dtype)


def _merge(ya, yb, wa, wb, gates, tm=1024, tn=1024):
    s = ya.shape[0]
    n = wa.shape[1]
    return pl.pallas_call(
        _merge_kernel,
        grid=(s // tm, n // tn),
        in_specs=[pl.BlockSpec((tm, ya.shape[1]), lambda i, j: (i, 0)),
                  pl.BlockSpec((tm, yb.shape[1]), lambda i, j: (i, 0)),
                  pl.BlockSpec((wa.shape[0], tn), lambda i, j: (0, j)),
                  pl.BlockSpec((wb.shape[0], tn), lambda i, j: (0, j)),
                  pl.BlockSpec((tm, tn), lambda i, j: (i, j)),
                  pl.BlockSpec((tm, tn), lambda i, j: (i, n // tn + j))],
        out_specs=pl.BlockSpec((tm, tn), lambda i, j: (i, j)),
        out_shape=jax.ShapeDtypeStruct((s, n), BF16),
        compiler_params=_params("parallel", "arbitrary"),
        name="merge",
    )(ya, yb, wa, wb, gates, gates)


def _out_proj_kernel(a_ref, b_ref, x_ref, g_ref, x1_ref, xg_ref, r2_ref, *, n_cols):
    j = pl.program_id(1)
    x1 = x_ref[...] + jnp.dot(a_ref[...], b_ref[...], preferred_element_type=F32)
    x1_ref[...] = x1
    xg_ref[...] = (x1 * g_ref[...]).astype(xg_ref.dtype)
    part = jnp.broadcast_to(jnp.sum(x1 * x1, axis=-1, keepdims=True), r2_ref.shape)

    @pl.when(j == 0)
    def _():
        r2_ref[...] = part

    @pl.when(j > 0)
    def _():
        r2_ref[...] += part

    @pl.when(j == n_cols // x_ref.shape[1] - 1)
    def _():
        r2_ref[...] = 1.0 / (r2_ref[...] * (1.0 / n_cols) + EPS)


def _out_proj(a, b, x, g, tm=1024, tn=1024):
    s, kdim = a.shape
    n = b.shape[1]
    lanes = HEAD_DIM
    return pl.pallas_call(
        functools.partial(_out_proj_kernel, n_cols=n),
        grid=(s // tm, n // tn),
        in_specs=[pl.BlockSpec((tm, kdim), lambda i, j: (i, 0), pipeline_mode=pl.Buffered(1)),
                  pl.BlockSpec((kdim, tn), lambda i, j: (0, j)),
                  pl.BlockSpec((tm, tn), lambda i, j: (i, j)),
                  pl.BlockSpec((1, tn), lambda i, j: (0, j))],
        out_specs=[pl.BlockSpec((tm, tn), lambda i, j: (i, j)),
                   pl.BlockSpec((tm, tn), lambda i, j: (i, j)),
                   pl.BlockSpec((tm, lanes), lambda i, j: (i, 0))],
        out_shape=[jax.ShapeDtypeStruct((s, n), F32),
                   jax.ShapeDtypeStruct((s, n), BF16),
                   jax.ShapeDtypeStruct((s, lanes), F32)],
        compiler_params=_params("parallel", "arbitrary"),
        name="out_proj",
    )(a, b, x, g.reshape(1, n))


def _ff1_kernel(xg_ref, r2_ref, b_ref, o_ref):
    lanes = r2_ref.shape[1]
    u = jnp.maximum(jnp.dot(xg_ref[...], b_ref[...], preferred_element_type=F32), 0.0)
    for c in range(o_ref.shape[1] // lanes):
        uc = u[:, c * lanes:(c + 1) * lanes]
        o_ref[:, c * lanes:(c + 1) * lanes] = (uc * uc * r2_ref[...]).astype(o_ref.dtype)


def _ff1(xg, r2, b, tm=1024, tn=1024):
    s, kdim = xg.shape
    n = b.shape[1]
    return pl.pallas_call(
        _ff1_kernel,
        grid=(s // tm, n // tn),
        in_specs=[pl.BlockSpec((tm, kdim), lambda i, j: (i, 0)),
                  pl.BlockSpec((tm, r2.shape[1]), lambda i, j: (i, 0)),
                  pl.BlockSpec((kdim, tn), lambda i, j: (0, j))],
        out_specs=pl.BlockSpec((tm, tn), lambda i, j: (i, j)),
        out_shape=jax.ShapeDtypeStruct((s, n), BF16),
        compiler_params=_params("parallel", "arbitrary"),
        name="ff1",
    )(xg, r2, b)


def _ff2_kernel(a_ref, b_ref, x_ref, o_ref):
    @pl.when(pl.program_id(2) == 0)
    def _():
        o_ref[...] = x_ref[...]

    o_ref[...] += jnp.dot(a_ref[...], b_ref[...], preferred_element_type=F32)


def _ff2(a, b, x, tm=1024, tn=1024, tk=4096):
    s, kdim = a.shape
    n = b.shape[1]
    return pl.pallas_call(
        _ff2_kernel,
        grid=(s // tm, n // tn, kdim // tk),
        in_specs=[pl.BlockSpec((tm, tk), lambda i, j, k: (i, k)),
                  pl.BlockSpec((tk, tn), lambda i, j, k: (k, j)),
                  pl.BlockSpec((tm, tn), lambda i, j, k: (i, j))],
        out_specs=pl.BlockSpec((tm, tn), lambda i, j, k: (i, j)),
        out_shape=jax.ShapeDtypeStruct((s, n), F32),
        compiler_params=_params("parallel", "parallel", "arbitrary"),
        name="ff2",
    )(a, b, x)


def _lambda_init(layer_idx):
    return 0.8 - 0.6 * math.exp(-0.3 * layer_idx)


def kernel(x, norm_mix, w_in, b_gate, w_proj_a, w_proj_b, w_out, lambda_q1, lambda_k1,
           lambda_q2, lambda_k2, subln_g, norm_mlp, w_ff1, w_ff2, norm_final):
    batch, seq, d = x.shape
    depth = w_in.shape[0]
    tables = _rope_tables(seq)
    outs = []
    for b in range(batch):
        xb = x.reshape(seq, d) if batch == 1 else x[b]
        for l in range(depth):
            w_in_l = w_in[l]
            h = _rmsnorm(xb, norm_mix[l], BF16)
            qkb = _proj_bqk(h, w_in_l, tables)
            vb = _proj_bv(h, w_in_l)
            lam_init = _lambda_init(l)
            whole = lambda w: (w, [(0, w.shape[1])])
            yb, wa_in_bf, wg_bf, wb_bf, wo_bf, w1_bf, w2_bf = _attn_b(
                qkb, vb, (lambda_q1[l], lambda_k1[l], lambda_q2[l], lambda_k2[l]),
                subln_g[l], lam_init,
                [(w_in_l, [(COL_QA, 3 * A_WIDTH), (COL_GATE, 2 * D_MODEL)]),
                 whole(w_proj_b[l]), whole(w_out[l]), whole(w_ff1[l]), whole(w_ff2[l])])
            groups = []
            for g in range(N_GROUPS):
                groups.append(_attn_a(_proj_a(h, wa_in_bf, tables, g)))
            ya = _mix_a(groups, seq)
            gates = _proj_gate(h, wg_bf, b_gate[l])
            merged = _merge(ya, yb, w_proj_a[l], wb_bf, gates)
            xb, xg, r2 = _out_proj(merged, wo_bf, xb, norm_mlp[l])
            u = _ff1(xg, r2, w1_bf)
            xb = _ff2(u, w2_bf, xb)
        outs.append(_rmsnorm(xb, norm_final, x.dtype))
    if batch == 1:
        return outs[0].reshape(1, seq, d)
    return jnp.stack(outs, axis=0)
```
